```python
import math
import jax
import jax.numpy as jnp
from jax import lax
import numpy as np

D_MODEL = 4096
BATCH = 2
SEQ = 8192
DEPTH = 2

CTX_LEN = 256
GRID_W = 64
N_SUB = 3
N_MOD = 3 * N_SUB
D_FF = 3 * D_MODEL // 2
ATT_HEAD_DIM = 128
ATT_WIDTH = D_MODEL // 2
ATT_HEADS = ATT_WIDTH // ATT_HEAD_DIM
WIN_H = 8
WIN_W = 16
ROPE_BASE = 10000.0
GMLP_WIDTH = D_MODEL // 4
CHUNK = 128
GMLP_GROUP_DIM = 128
GMLP_GROUPS = GMLP_WIDTH // GMLP_GROUP_DIM
SSM_WIDTH = D_MODEL // 4
SSM_GROUP = 16
SSM_GROUPS = SSM_WIDTH // SSM_GROUP
SSM_STATE = 64
DT_MIN = 1e-3
DT_MAX = 1e-1
N_BRANCH = 3
Q_OFF = 0
K_OFF = ATT_WIDTH
V_OFF = 2 * ATT_WIDTH
GM_OFF = 3 * ATT_WIDTH
SSM_OFF = GM_OFF + 2 * GMLP_WIDTH
GATE_OFF = SSM_OFF + SSM_WIDTH
IN_COLS = GATE_OFF + N_BRANCH * D_MODEL
ALPHA = (2.0 * DEPTH) ** 0.25
BETA = (8.0 * DEPTH) ** -0.25
LN_EPS = 1e-6
NEG_INF = -1e30

kernel_name = 'hybrid_natten_gmlp_s5_diffusion_block'


def layer_norm(x, g=None, b=None):
    xf = x.astype(jnp.float32)
    mu = jnp.mean(xf, axis=-1, keepdims=True)
    var = jnp.mean(jnp.square(xf - mu), axis=-1, keepdims=True)
    y = (xf - mu) * lax.rsqrt(var + LN_EPS)
    if g is not None:
        y = y * g.astype(jnp.float32) + b.astype(jnp.float32)
    return y.astype(x.dtype)


def modulate(x, shift, scale):
    return layer_norm(x) * (1.0 + scale) + shift


def post_norm(x, y, g, b):
    return layer_norm(ALPHA * x + y, g, b)


def swiglu(h, w_in, w_out):
    g, u = jnp.split(h @ w_in, 2, axis=-1)
    return (jax.nn.silu(g) * u) @ w_out


def rope_1d(t, pos):
    half = t.shape[-1] // 2
    inv = ROPE_BASE ** (-jnp.arange(half, dtype=jnp.float32) / half)
    ang = pos.astype(jnp.float32)[:, None] * inv[None, :]
    cos = jnp.cos(ang)[:, None, :]
    sin = jnp.sin(ang)[:, None, :]
    t1 = t[..., :half].astype(jnp.float32)
    t2 = t[..., half:].astype(jnp.float32)
    return jnp.concatenate([t1 * cos - t2 * sin, t1 * sin + t2 * cos], axis=-1).astype(t.dtype)


def rope_2d(t, rows, cols):
    h = t.shape[-1] // 2
    return jnp.concatenate([rope_1d(t[..., :h], rows), rope_1d(t[..., h:], cols)], axis=-1)


def neighborhood_attention(q, q_rot, k_rot, v, k_ctx, v_ctx, rpb):
    Bn, L, H, Dh = q.shape
    rows = L // GRID_W
    kh = min(WIN_H, rows)
    scale = Dh ** -0.5
    qg = q.reshape(Bn, rows, GRID_W, H, Dh)
    qrg = q_rot.reshape(Bn, rows, GRID_W, H, Dh)
    kg = k_rot.reshape(Bn, rows, GRID_W, H, Dh)
    vg = v.reshape(Bn, rows, GRID_W, H, Dh)
    col = jnp.arange(GRID_W)
    col_start = jnp.clip(col - WIN_W // 2, 0, GRID_W - WIN_W)
    col_mask = (col[None, :] >= col_start[:, None]) & (col[None, :] < col_start[:, None] + WIN_W)
    dc = jnp.clip(col[None, :] - col[:, None] + WIN_W - 1, 0, 2 * WIN_W - 2)

    def one_row(r):
        rs = jnp.clip(r - kh // 2, 0, rows - kh)
        kb = lax.dynamic_slice_in_dim(kg, rs, kh, axis=1)
        vb = lax.dynamic_slice_in_dim(vg, rs, kh, axis=1)
        qr = lax.dynamic_index_in_dim(qrg, r, axis=1, keepdims=False)
        qp = lax.dynamic_index_in_dim(qg, r, axis=1, keepdims=False)
        s_loc = jnp.einsum('bqhd,bikhd->bhqik', qr, kb, preferred_element_type=jnp.float32) * scale
        dr = rs + jnp.arange(kh) - r + (WIN_H - 1)
        bias = rpb[:, dr][:, :, dc].astype(jnp.float32)
        s_loc = s_loc + jnp.transpose(bias, (0, 2, 1, 3))[None]
        s_loc = jnp.where(col_mask[None, None, :, None, :], s_loc, NEG_INF)
        s_ctx = jnp.einsum('bqhd,bchd->bhqc', qp, k_ctx, preferred_element_type=jnp.float32) * scale
        s = jnp.concatenate([s_loc.reshape(Bn, H, GRID_W, kh * GRID_W), s_ctx], axis=-1)
        p = jax.nn.softmax(s, axis=-1).astype(v.dtype)
        p_loc = p[..., :kh * GRID_W].reshape(Bn, H, GRID_W, kh, GRID_W)
        p_ctx = p[..., kh * GRID_W:]
        return (jnp.einsum('bhqik,bikhd->bqhd', p_loc, vb)
                + jnp.einsum('bhqc,bchd->bqhd', p_ctx, v_ctx))

    out = lax.map(one_row, jnp.arange(rows))
    return jnp.transpose(out, (1, 0, 2, 3, 4)).reshape(Bn, L, H * Dh)


def context_attention(q, k, v):
    s = jnp.einsum('bqhd,bkhd->bhqk', q, k, preferred_element_type=jnp.float32) * (ATT_HEAD_DIM ** -0.5)
    p = jax.nn.softmax(s, axis=-1).astype(v.dtype)
    o = jnp.einsum('bhqk,bkhd->bqhd', p, v)
    return o.reshape(o.shape[0], o.shape[1], ATT_WIDTH)


def chunk_gmlp(z, ln_g, ln_b, w_s, b_s):
    z = jax.nn.gelu(z)
    u, v = jnp.split(z, 2, axis=-1)
    v = layer_norm(v, ln_g, ln_b)
    Bn, L, _ = v.shape
    vc = v.reshape(Bn, L // CHUNK, CHUNK, GMLP_GROUPS, GMLP_GROUP_DIM)
    mixed = jnp.einsum('gqp,bnpgc->bnqgc', w_s, vc) + jnp.transpose(b_s)[None, None, :, :, None]
    return u * mixed.reshape(Bn, L, GMLP_WIDTH)


def s5_discretize(lam_re, lam_im, log_dt, b_re, b_im):
    dt = jnp.exp(log_dt.astype(jnp.float32))[:, None]
    lr = lam_re.astype(jnp.float32)
    li = lam_im.astype(jnp.float32)
    mag = jnp.exp(lr * dt)
    ar = mag * jnp.cos(li * dt)
    ai = mag * jnp.sin(li * dt)
    den = lr * lr + li * li
    fr = ((ar - 1.0) * lr + ai * li) / den
    fi = (ai * lr - (ar - 1.0) * li) / den
    br = b_re.astype(jnp.float32)
    bi = b_im.astype(jnp.float32)
    bbr = fr[..., None] * br - fi[..., None] * bi
    bbi = fr[..., None] * bi + fi[..., None] * br
    return ar, ai, bbr, bbi


def _cplx_combine(e1, e2):
    a1r, a1i, b1r, b1i = e1
    a2r, a2i, b2r, b2i = e2
    return (a2r * a1r - a2i * a1i, a2r * a1i + a2i * a1r,
            a2r * b1r - a2i * b1i + b2r, a2r * b1i + a2i * b1r + b2i)


def s5_states(ug, h0r, h0i, ar, ai, bbr, bbi):
    bur = jnp.einsum('blgc,gpc->blgp', ug, bbr)
    bui = jnp.einsum('blgc,gpc->blgp', ug, bbi)
    bur = bur.at[:, 0].add(ar * h0r - ai * h0i)
    bui = bui.at[:, 0].add(ar * h0i + ai * h0r)
    L = ug.shape[1]
    a_r = jnp.broadcast_to(ar, (1, L) + ar.shape)
    a_i = jnp.broadcast_to(ai, (1, L) + ai.shape)
    _, _, hr, hi = lax.associative_scan(_cplx_combine, (a_r, a_i, bur, bui), axis=1)
    return hr, hi


def s5_readout(hr, hi, c_re, c_im):
    y = (jnp.einsum('blgp,gcp->blgc', hr, c_re.astype(jnp.float32))
         - jnp.einsum('blgp,gcp->blgc', hi, c_im.astype(jnp.float32)))
    return y.reshape(y.shape[0], y.shape[1], SSM_WIDTH)


def dir_flip(t, d):
    return jnp.flip(t, axis=1) if d == 1 else t


def s5_output(y, u, d_skip, w_glu, b_glu):
    y = (y + d_skip.astype(jnp.float32) * u.astype(jnp.float32)).astype(u.dtype)
    y = jax.nn.gelu(y)
    return y * jax.nn.sigmoid(y @ w_glu + b_glu)


def merge_branches(o_a, o_b, o_c, gates, w_pa, w_pb, w_pc, w_o):
    g = jax.nn.sigmoid(gates.astype(jnp.float32)).astype(gates.dtype)
    ga, gb, gc = jnp.split(g, N_BRANCH, axis=-1)
    return (ga * (o_a @ w_pa) + gb * (o_b @ w_pb) + gc * (o_c @ w_pc)) @ w_o


def token_mixer(h, hc, w_in, rpb, gm_ln_g, gm_ln_b, gm_ws, gm_bs,
                lam_re, lam_im, log_dt, b_re, b_im, c_re, c_im, d_skip, w_glu, b_glu,
                w_pa, w_pb, w_pc, w_o, need_ctx_out):
    Bn, L, _ = h.shape
    Bc = hc.shape[0]

    def heads(t):
        return t.reshape(t.shape[0], t.shape[1], ATT_HEADS, ATT_HEAD_DIM)

    p = h @ w_in
    q = heads(p[..., Q_OFF:K_OFF])
    k = heads(p[..., K_OFF:V_OFF])
    v = heads(p[..., V_OFF:GM_OFF])
    z = p[..., GM_OFF:SSM_OFF]
    us = p[..., SSM_OFF:GATE_OFF]
    gates = p[..., GATE_OFF:]
    kvc = hc @ w_in[:, K_OFF:GM_OFF]
    kc = heads(kvc[..., :ATT_WIDTH])
    vc = heads(kvc[..., ATT_WIDTH:])
    usc = hc @ w_in[:, SSM_OFF:GATE_OFF]

    t = jnp.arange(L)
    rows = t // GRID_W
    cols = t % GRID_W
    o_att = neighborhood_attention(q, rope_2d(q, rows, cols), rope_2d(k, rows, cols), v, kc, vc, rpb)
    o_gm = chunk_gmlp(z, gm_ln_g, gm_ln_b, gm_ws, gm_bs)
    ug = us.astype(jnp.float32).reshape(Bn, L, SSM_GROUPS, SSM_GROUP)
    ugc = usc.astype(jnp.float32).reshape(Bc, hc.shape[1], SSM_GROUPS, SSM_GROUP)
    zeros = jnp.zeros((Bc, SSM_GROUPS, SSM_STATE), jnp.float32)
    y_s = 0.0
    y_sc = 0.0
    for d in range(2):
        ar, ai, bbr, bbi = s5_discretize(lam_re[d], lam_im[d], log_dt[d], b_re[d], b_im[d])
        scr, sci = s5_states(dir_flip(ugc, d), zeros, zeros, ar, ai, bbr, bbi)
        slr, sli = s5_states(dir_flip(ug, d), scr[:, -1], sci[:, -1], ar, ai, bbr, bbi)
        y_s = y_s + dir_flip(s5_readout(slr, sli, c_re[d], c_im[d]), d)
        if need_ctx_out:
            y_sc = y_sc + dir_flip(s5_readout(scr, sci, c_re[d], c_im[d]), d)
    o_ssm = s5_output(y_s, us, d_skip, w_glu, b_glu)
    y_lat = merge_branches(o_gm, o_att, o_ssm, gates, w_pa, w_pb, w_pc, w_o)
    if not need_ctx_out:
        return y_lat, None

    qc = heads(hc @ w_in[:, Q_OFF:K_OFF])
    zc = hc @ w_in[:, GM_OFF:SSM_OFF]
    gates_c = hc @ w_in[:, GATE_OFF:]
    oc_att = context_attention(qc, kc, vc)
    oc_gm = chunk_gmlp(zc, gm_ln_g, gm_ln_b, gm_ws, gm_bs)
    oc_ssm = s5_output(y_sc, usc, d_skip, w_glu, b_glu)
    y_ctx = merge_branches(oc_gm, oc_att, oc_ssm, gates_c, w_pa, w_pb, w_pc, w_o)
    return y_lat, y_ctx


def setup_inputs(seed: int = 0) -> dict:
    key = jax.random.key(seed)
    ks = iter(jax.random.split(key, 40))
    f32 = jnp.float32

    def nrm(shape, s):
        return jax.random.normal(next(ks), shape, f32) * s

    D = D_MODEL
    G, P = SSM_GROUPS, SSM_STATE
    x = nrm((BATCH, SEQ, D), 1.0)
    c = nrm((BATCH, D), 1.0)
    ctx = nrm((BATCH, CTX_LEN, D), 1.0)
    c_ctx = nrm((D,), 1.0)
    w_mod = nrm((DEPTH, D, N_MOD * D), D ** -0.5)
    b_mod = nrm((DEPTH, N_MOD * D), 0.02)
    ln_g = 1.0 + nrm((DEPTH, N_SUB, D), 0.02)
    ln_b = nrm((DEPTH, N_SUB, D), 0.02)
    ffn1_w_in = nrm((DEPTH, D, 2 * D_FF), D ** -0.5)
    ffn1_w_out = nrm((DEPTH, D_FF, D), BETA * D_FF ** -0.5)
    ffn2_w_in = nrm((DEPTH, D, 2 * D_FF), D ** -0.5)
    ffn2_w_out = nrm((DEPTH, D_FF, D), BETA * D_FF ** -0.5)
    w_in = nrm((DEPTH, D, IN_COLS), D ** -0.5)
    rpb = nrm((DEPTH, ATT_HEADS, 2 * WIN_H - 1, 2 * WIN_W - 1), 0.05)
    gm_ln_g = 1.0 + nrm((DEPTH, GMLP_WIDTH), 0.02)
    gm_ln_b = nrm((DEPTH, GMLP_WIDTH), 0.02)
    gm_ws = nrm((DEPTH, GMLP_GROUPS, CHUNK, CHUNK), CHUNK ** -0.5)
    gm_bs = 1.0 + nrm((DEPTH, GMLP_GROUPS, CHUNK), 0.02)
    ssm_lam_re = -0.5 + nrm((DEPTH, 2, G, P), 0.01)
    ssm_lam_im = math.pi * jnp.arange(P, dtype=f32) + nrm((DEPTH, 2, G, P), 0.01)
    ssm_log_dt = jax.random.uniform(next(ks), (DEPTH, 2, G), f32, math.log(DT_MIN), math.log(DT_MAX))
    ssm_b_re = nrm((DEPTH, 2, G, P, SSM_GROUP), (2 * SSM_GROUP) ** -0.5)
    ssm_b_im = nrm((DEPTH, 2, G, P, SSM_GROUP), (2 * SSM_GROUP) ** -0.5)
    ssm_c_re = nrm((DEPTH, 2, G, SSM_GROUP, P), P ** -0.5)
    ssm_c_im = nrm((DEPTH, 2, G, SSM_GROUP, P), P ** -0.5)
    ssm_d = nrm((DEPTH, SSM_WIDTH), 1.0)
    ssm_w_glu = nrm((DEPTH, SSM_WIDTH, SSM_WIDTH), SSM_WIDTH ** -0.5)
    ssm_b_glu = nrm((DEPTH, SSM_WIDTH), 0.02)
    w_pa = nrm((DEPTH, GMLP_WIDTH, D), GMLP_WIDTH ** -0.5)
    w_pb = nrm((DEPTH, ATT_WIDTH, D), ATT_WIDTH ** -0.5)
    w_pc = nrm((DEPTH, SSM_WIDTH, D), SSM_WIDTH ** -0.5)
    w_o = nrm((DEPTH, D, D), BETA * D ** -0.5)
    return {'x': x, 'c': c, 'ctx': ctx, 'c_ctx': c_ctx, 'w_mod': w_mod, 'b_mod': b_mod,
            'ln_g': ln_g, 'ln_b': ln_b, 'ffn1_w_in': ffn1_w_in, 'ffn1_w_out': ffn1_w_out,
            'ffn2_w_in': ffn2_w_in, 'ffn2_w_out': ffn2_w_out, 'w_in': w_in, 'rpb': rpb,
            'gm_ln_g': gm_ln_g, 'gm_ln_b': gm_ln_b, 'gm_ws': gm_ws, 'gm_bs': gm_bs,
            'ssm_lam_re': ssm_lam_re, 'ssm_lam_im': ssm_lam_im, 'ssm_log_dt': ssm_log_dt,
            'ssm_b_re': ssm_b_re, 'ssm_b_im': ssm_b_im, 'ssm_c_re': ssm_c_re, 'ssm_c_im': ssm_c_im,
            'ssm_d': ssm_d, 'ssm_w_glu': ssm_w_glu, 'ssm_b_glu': ssm_b_glu,
            'w_pa': w_pa, 'w_pb': w_pb, 'w_pc': w_pc, 'w_o': w_o}


def reference(x, c, ctx, c_ctx, w_mod, b_mod, ln_g, ln_b, ffn1_w_in, ffn1_w_out, ffn2_w_in, ffn2_w_out,
              w_in, rpb, gm_ln_g, gm_ln_b, gm_ws, gm_bs, ssm_lam_re, ssm_lam_im, ssm_log_dt,
              ssm_b_re, ssm_b_im, ssm_c_re, ssm_c_im, ssm_d, ssm_w_glu, ssm_b_glu,
              w_pa, w_pb, w_pc, w_o):
    Bn = x.shape[0]
    for l in range(DEPTH):
        need_ctx_out = l < DEPTH - 1
        mx = (jax.nn.silu(c) @ w_mod[l] + b_mod[l]).reshape(Bn, N_MOD, 1, D_MODEL)
        mx = [mx[:, i] for i in range(N_MOD)]
        mc = (jax.nn.silu(c_ctx) @ w_mod[l] + b_mod[l]).reshape(N_MOD, D_MODEL)
        x = post_norm(x, 0.5 * mx[2] * swiglu(modulate(x, mx[0], mx[1]), ffn1_w_in[l], ffn1_w_out[l]),
                      ln_g[l, 0], ln_b[l, 0])
        ctx = post_norm(ctx, 0.5 * mc[2] * swiglu(modulate(ctx, mc[0], mc[1]), ffn1_w_in[l], ffn1_w_out[l]),
                        ln_g[l, 0], ln_b[l, 0])
        y_lat, y_ctx = token_mixer(
            modulate(x, mx[3], mx[4]), modulate(ctx, mc[3], mc[4]), w_in[l], rpb[l],
            gm_ln_g[l], gm_ln_b[l], gm_ws[l], gm_bs[l],
            ssm_lam_re[l], ssm_lam_im[l], ssm_log_dt[l], ssm_b_re[l], ssm_b_im[l],
            ssm_c_re[l], ssm_c_im[l], ssm_d[l], ssm_w_glu[l], ssm_b_glu[l],
            w_pa[l], w_pb[l], w_pc[l], w_o[l], need_ctx_out)
        x = post_norm(x, mx[5] * y_lat, ln_g[l, 1], ln_b[l, 1])
        x = post_norm(x, 0.5 * mx[8] * swiglu(modulate(x, mx[6], mx[7]), ffn2_w_in[l], ffn2_w_out[l]),
                      ln_g[l, 2], ln_b[l, 2])
        if need_ctx_out:
            ctx = post_norm(ctx, mc[5] * y_ctx, ln_g[l, 1], ln_b[l, 1])
            ctx = post_norm(ctx, 0.5 * mc[8] * swiglu(modulate(ctx, mc[6], mc[7]), ffn2_w_in[l], ffn2_w_out[l]),
                            ln_g[l, 2], ln_b[l, 2])
    return x
```

```python
import functools
import math

import jax
import jax.numpy as jnp
from jax import lax
from jax.experimental import pallas as pl
from jax.experimental.pallas import tpu as pltpu

F32 = jnp.float32
BF16 = jnp.bfloat16

GRID_W = 64
N_SUB = 3
N_MOD = 3 * N_SUB
HEAD_DIM = 128
WIN_H = 8
WIN_W = 16
ROPE_BASE = 10000.0
CHUNK = 128
GMLP_GROUP_DIM = 128
SSM_GROUP = 16
SSM_STATE = 64
N_BRANCH = 3
LN_EPS = 1e-6
NEG_INF = -1e30

SSM_BLOCK = 16
SSM_ROW = SSM_BLOCK * SSM_GROUP
MOD_ROWS = 8
VMEM_LIMIT_BYTES = 56 * 1024 * 1024

TM = 512
TM_ROW = 256
TN = 512
ATT_ROWS = 8


def _params(*sem):
    return pltpu.CompilerParams(dimension_semantics=sem, vmem_limit_bytes=VMEM_LIMIT_BYTES)


def _ln(x):
    mu = jnp.mean(x, axis=-1, keepdims=True)
    xc = x - mu
    var = jnp.mean(xc * xc, axis=-1, keepdims=True)
    return xc * lax.rsqrt(var + LN_EPS)


def _col_tile(n_cols, col_off=0):
    lane = 128
    for tn in range(TN, 0, -lane):
        if n_cols % tn == 0 and col_off % tn == 0:
            return tn
    raise ValueError(f"column group ({col_off}, {n_cols}) is not lane aligned")


def _seg_map(tiles_per_batch, n_batch):
    return lambda i: jnp.minimum(i // tiles_per_batch, n_batch)


def _mods_kernel(c_ref, w_ref, b_ref, o_ref):
    s = jax.nn.silu(c_ref[...]).astype(BF16)
    o_ref[...] = jnp.dot(s, w_ref[...].astype(BF16), preferred_element_type=F32) + b_ref[...]


def _mods(c_all, w_mod, b_mod):
    depth, d, nm = w_mod.shape
    tn = _col_tile(nm)
    return pl.pallas_call(
        _mods_kernel,
        out_shape=jax.ShapeDtypeStruct((depth, MOD_ROWS, nm), F32),
        grid=(depth, nm // tn),
        in_specs=[pl.BlockSpec((MOD_ROWS, d), lambda l, j: (0, 0)),
                  pl.BlockSpec((None, d, tn), lambda l, j: (l, 0, j)),
                  pl.BlockSpec((None, 1, tn), lambda l, j: (l, 0, j))],
        out_specs=pl.BlockSpec((None, MOD_ROWS, tn), lambda l, j: (l, 0, j)),
        compiler_params=_params("parallel", "parallel"),
    )(c_all, w_mod, b_mod.reshape(depth, 1, nm))


def _modulate_kernel(x_ref, mod_ref, h_ref, *, i_shift, i_scale):
    h = _ln(x_ref[...]) * (1.0 + mod_ref[i_scale:i_scale + 1, :]) + mod_ref[i_shift:i_shift + 1, :]
    h_ref[...] = h.astype(BF16)


def _modulate(x, mods, i_shift, i_scale, seg):
    n, d = x.shape
    return pl.pallas_call(
        functools.partial(_modulate_kernel, i_shift=i_shift, i_scale=i_scale),
        out_shape=jax.ShapeDtypeStruct((n, d), BF16),
        grid=(n // TM_ROW,),
        in_specs=[pl.BlockSpec((TM_ROW, d), lambda i: (i, 0)),
                  pl.BlockSpec((None, N_MOD, d), lambda i: (seg(i), 0, 0))],
        out_specs=pl.BlockSpec((TM_ROW, d), lambda i: (i, 0)),
        compiler_params=_params("parallel"),
    )(x, mods)


def _norm_kernel(x_ref, y_ref, modg_ref, g_ref, b_ref, *rest, alpha, coef, i_gate, i_shift, i_scale):
    t = alpha * x_ref[...] + (coef * modg_ref[i_gate:i_gate + 1, :]) * y_ref[...]
    xn = _ln(t) * g_ref[...] + b_ref[...]
    if i_shift is None:
        (xo_ref,) = rest
        xo_ref[...] = xn
    else:
        modn_ref, xo_ref, h_ref = rest
        xo_ref[...] = xn
        h = _ln(xn) * (1.0 + modn_ref[i_scale:i_scale + 1, :]) + modn_ref[i_shift:i_shift + 1, :]
        h_ref[...] = h.astype(BF16)


def _norm(x, y, mods_gate, i_gate, coef, g, b, alpha, seg, n_rows, mods_next=None, i_shift=None, i_scale=None):
    n, d = x.shape
    row = pl.BlockSpec((TM_ROW, d), lambda i: (i, 0))
    vec = pl.BlockSpec((1, d), lambda i: (0, 0))
    mod = pl.BlockSpec((None, N_MOD, d), lambda i: (seg(i), 0, 0))
    in_specs = [row, row, mod, vec, vec]
    args = [x, y, mods_gate, g.reshape(1, d), b.reshape(1, d)]
    out_shape = [jax.ShapeDtypeStruct((n, d), F32)]
    out_specs = [row]
    if i_shift is not None:
        in_specs.append(mod)
        args.append(mods_next)
        out_shape.append(jax.ShapeDtypeStruct((n, d), BF16))
        out_specs.append(row)
    out = pl.pallas_call(
        functools.partial(_norm_kernel, alpha=alpha, coef=coef, i_gate=i_gate, i_shift=i_shift, i_scale=i_scale),
        out_shape=out_shape,
        grid=(n_rows // TM_ROW,),
        in_specs=in_specs,
        out_specs=out_specs,
        compiler_params=_params("parallel"),
    )(*args)
    return out if i_shift is not None else (out[0], None)


def _ffn_in_kernel(h_ref, wg_ref, wu_ref, o_ref):
    h = h_ref[...]
    g = jnp.dot(h, wg_ref[...], preferred_element_type=F32)
    u = jnp.dot(h, wu_ref[...], preferred_element_type=F32)
    o_ref[...] = (jax.nn.silu(g) * u).astype(o_ref.dtype)


def _ffn_in(h, w, n_rows):
    n, d = h.shape
    dff = w.shape[1] // 2
    tn = _col_tile(dff)
    return pl.pallas_call(
        _ffn_in_kernel,
        out_shape=jax.ShapeDtypeStruct((n, dff), BF16),
        grid=(dff // tn, n_rows // TM),
        in_specs=[pl.BlockSpec((TM, d), lambda j, i: (i, 0)),
                  pl.BlockSpec((d, tn), lambda j, i: (0, j)),
                  pl.BlockSpec((d, tn), lambda j, i: (0, j + dff // tn))],
        out_specs=pl.BlockSpec((TM, tn), lambda j, i: (i, j)),
        compiler_params=_params("parallel", "parallel"),
    )(h, w, w)


def _mm_kernel(a_ref, w_ref, o_ref, *, act):
    acc = jnp.dot(a_ref[...], w_ref[...], preferred_element_type=F32)
    if act == "sigmoid":
        acc = jax.nn.sigmoid(acc)
    o_ref[...] = acc.astype(o_ref.dtype)


def _mm(a, w, col_off, n_cols, out_dtype, n_rows, act=None):
    n, k = a.shape
    tn = _col_tile(n_cols, col_off)
    joff = col_off // tn
    return pl.pallas_call(
        functools.partial(_mm_kernel, act=act),
        out_shape=jax.ShapeDtypeStruct((n, n_cols), out_dtype),
        grid=(n_cols // tn, n_rows // TM),
        in_specs=[pl.BlockSpec((TM, k), lambda j, i: (i, 0)),
                  pl.BlockSpec((k, tn), lambda j, i: (0, j + joff))],
        out_specs=pl.BlockSpec((TM, tn), lambda j, i: (i, j)),
        compiler_params=_params("parallel", "parallel"),
    )(a, w)


def _mm_rope_kernel(a_ref, w_ref, cos_ref, sin_ref, o_ref, orot_ref):
    acc = jnp.dot(a_ref[...], w_ref[...], preferred_element_type=F32)
    o_ref[...] = acc.astype(o_ref.dtype)
    tn = acc.shape[1]
    reps = tn // HEAD_DIM
    cos = jnp.concatenate([cos_ref[...]] * reps, axis=1)
    sin = jnp.concatenate([sin_ref[...]] * reps, axis=1)
    lane = lax.broadcasted_iota(jnp.int32, acc.shape, 1)
    quarter = HEAD_DIM // 4
    first = (lane % (2 * quarter)) < quarter
    partner = jnp.where(first, pltpu.roll(acc, tn - quarter, 1), pltpu.roll(acc, quarter, 1))
    orot_ref[...] = (acc * cos + partner * sin).astype(orot_ref.dtype)


def _mm_rope(a, w, col_off, n_cols, cos, sin, tiles_per_batch, n_rows):
    n, k = a.shape
    tn = _col_tile(n_cols, col_off)
    joff = col_off // tn
    out = jax.ShapeDtypeStruct((n, n_cols), BF16)
    tab = pl.BlockSpec((TM, HEAD_DIM), lambda j, i: (i % tiles_per_batch, 0))
    blk = pl.BlockSpec((TM, tn), lambda j, i: (i, j))
    return pl.pallas_call(
        _mm_rope_kernel,
        out_shape=[out, out],
        grid=(n_cols // tn, n_rows // TM),
        in_specs=[pl.BlockSpec((TM, k), lambda j, i: (i, 0)),
                  pl.BlockSpec((k, tn), lambda j, i: (0, j + joff)),
                  tab, tab],
        out_specs=[blk, blk],
        compiler_params=_params("parallel", "parallel"),
    )(a, w, cos, sin)


def _merge_kernel(oa_ref, ob_ref, oc_ref, wa_ref, wb_ref, wc_ref, ga_ref, gb_ref, gc_ref, o_ref):
    t = ga_ref[...].astype(F32) * jnp.dot(oa_ref[...], wa_ref[...], preferred_element_type=F32)
    t += gb_ref[...].astype(F32) * jnp.dot(ob_ref[...], wb_ref[...], preferred_element_type=F32)
    t += gc_ref[...].astype(F32) * jnp.dot(oc_ref[...], wc_ref[...], preferred_element_type=F32)
    o_ref[...] = t.astype(o_ref.dtype)


def _merge(o_a, o_b, o_c, w_pa, w_pb, w_pc, gates, n_rows):
    n = o_a.shape[0]
    d = w_pa.shape[1]
    tn = _col_tile(d)
    nj = d // tn

    def act(o):
        return pl.BlockSpec((TM, o.shape[1]), lambda j, i: (i, 0))

    def wgt(w):
        return pl.BlockSpec((w.shape[0], tn), lambda j, i: (0, j))

    def gate(k):
        return pl.BlockSpec((TM, tn), lambda j, i: (i, j + k * nj))

    return pl.pallas_call(
        _merge_kernel,
        out_shape=jax.ShapeDtypeStruct((n, d), BF16),
        grid=(nj, n_rows // TM),
        in_specs=[act(o_a), act(o_b), act(o_c), wgt(w_pa), wgt(w_pb), wgt(w_pc), gate(0), gate(1), gate(2)],
        out_specs=pl.BlockSpec((TM, tn), lambda j, i: (i, j)),
        compiler_params=_params("parallel", "parallel"),
    )(o_a, o_b, o_c, w_pa, w_pb, w_pc, gates, gates, gates)


def _rpb_table_kernel(rpb_ref, o_ref, *, n_bias_cols):
    h = pl.program_id(0)
    shape = (GRID_W, 2 * GRID_W)
    cq = lax.broadcasted_iota(jnp.int32, shape, 0)
    lane = lax.broadcasted_iota(jnp.int32, shape, 1)
    ck = lane % GRID_W
    second = lane >= GRID_W
    start = jnp.clip(cq - WIN_W // 2, 0, GRID_W - WIN_W)
    in_win = (ck >= start) & (ck < start + WIN_W)
    dc = jnp.clip(ck - cq + WIN_W - 1, 0, 2 * WIN_W - 2)
    n_rows_bias = 2 * WIN_H - 1
    base = h * (n_rows_bias * n_bias_cols)
    for alo in range(WIN_H):
        for pair in range(WIN_H // 2):
            a0 = alo + 2 * pair

            def body(b, acc, a0=a0):
                v0 = rpb_ref[base + a0 * n_bias_cols + b]
                v1 = rpb_ref[base + (a0 + 1) * n_bias_cols + b]
                return jnp.where(dc == b, jnp.where(second, v1, v0), acc)

            acc = lax.fori_loop(0, n_bias_cols, body, jnp.zeros(shape, F32))
            o_ref[alo, :, pair * 2 * GRID_W:(pair + 1) * 2 * GRID_W] = jnp.where(in_win, acc, NEG_INF)


def _rpb_table(rpb):
    heads, nr, nc = rpb.shape
    return pl.pallas_call(
        functools.partial(_rpb_table_kernel, n_bias_cols=nc),
        out_shape=jax.ShapeDtypeStruct((heads, WIN_H, GRID_W, WIN_H * GRID_W), F32),
        grid=(heads,),
        in_specs=[pl.BlockSpec(memory_space=pltpu.SMEM)],
        out_specs=pl.BlockSpec((None, WIN_H, GRID_W, WIN_H * GRID_W), lambda h: (h, 0, 0, 0)),
        compiler_params=_params("parallel"),
    )(rpb.reshape(-1))


def _attn_kernel(q_ref, qr_ref, kr_ref, v_ref, kc_ref, vc_ref, bias_ref, o_ref, *, n_grid_rows, scale):
    rb = pl.program_id(2)
    win = WIN_H * GRID_W
    kc = kc_ref[...]
    vc = vc_ref[...]
    nt = (((1,), (1,)), ((), ()))
    for i in range(ATT_ROWS):
        r = rb * ATT_ROWS + i
        rs = jnp.clip(r - WIN_H // 2, 0, n_grid_rows - WIN_H)
        alo = rs - r + (WIN_H - 1)
        start = pl.multiple_of(rs * GRID_W, GRID_W)
        kw = kr_ref[pl.ds(start, win), :]
        vw = v_ref[pl.ds(start, win), :]
        rows = pl.ds(i * GRID_W, GRID_W)
        s = lax.dot_general(qr_ref[rows, :], kw, nt, preferred_element_type=F32) * scale + bias_ref[alo]
        sc = lax.dot_general(q_ref[rows, :], kc, nt, preferred_element_type=F32) * scale
        m = jnp.maximum(jnp.max(s, axis=-1, keepdims=True), jnp.max(sc, axis=-1, keepdims=True))
        p = jnp.exp(s - m)
        pc = jnp.exp(sc - m)
        denom = jnp.sum(p, axis=-1, keepdims=True) + jnp.sum(pc, axis=-1, keepdims=True)
        o = jnp.dot(p.astype(BF16), vw, preferred_element_type=F32)
        o += jnp.dot(pc.astype(BF16), vc, preferred_element_type=F32)
        o_ref[rows, :] = (o / denom).astype(o_ref.dtype)


def _attention(q, q_rot, k, k_rot, v, bias, n_batch, seq, ctx_len):
    n, width = q.shape
    heads = width // HEAD_DIM
    n_grid_rows = seq // GRID_W
    tq = ATT_ROWS * GRID_W
    blocks = seq // tq
    ctx0 = n_batch * seq // ctx_len
    qspec = pl.BlockSpec((tq, HEAD_DIM), lambda b, h, r: (b * blocks + r, h))
    kspec = pl.BlockSpec((seq, HEAD_DIM), lambda b, h, r: (b, h))
    cspec = pl.BlockSpec((ctx_len, HEAD_DIM), lambda b, h, r: (ctx0 + b, h))
    return pl.pallas_call(
        functools.partial(_attn_kernel, n_grid_rows=n_grid_rows, scale=HEAD_DIM ** -0.5),
        out_shape=jax.ShapeDtypeStruct((n, width), BF16),
        grid=(n_batch, heads, blocks),
        in_specs=[qspec, qspec, kspec, kspec, cspec, cspec,
                  pl.BlockSpec((None, WIN_H, GRID_W, WIN_H * GRID_W), lambda b, h, r: (h, 0, 0, 0))],
        out_specs=qspec,
        compiler_params=_params("parallel", "parallel", "arbitrary"),
    )(q, q_rot, k_rot, v, k, v, bias)


def _ctx_attn_kernel(q_ref, k_ref, v_ref, prev_ref, o_ref, *, scale):
    del prev_ref
    s = lax.dot_general(q_ref[...], k_ref[...], (((1,), (1,)), ((), ())), preferred_element_type=F32) * scale
    m = jnp.max(s, axis=-1, keepdims=True)
    p = jnp.exp(s - m)
    denom = jnp.sum(p, axis=-1, keepdims=True)
    o = jnp.dot(p.astype(BF16), v_ref[...], preferred_element_type=F32)
    o_ref[...] = (o / denom).astype(o_ref.dtype)


def _ctx_attention(q, k, v, o_att, n_batch, seq, ctx_len):
    n, width = q.shape
    heads = width // HEAD_DIM
    ctx0 = n_batch * seq // ctx_len
    spec = pl.BlockSpec((ctx_len, HEAD_DIM), lambda b, h: (ctx0 + b, h))
    return pl.pallas_call(
        functools.partial(_ctx_attn_kernel, scale=HEAD_DIM ** -0.5),
        out_shape=jax.ShapeDtypeStruct((n, width), BF16),
        grid=(n_batch, heads),
        in_specs=[spec, spec, spec, pl.BlockSpec(memory_space=pl.ANY)],
        out_specs=spec,
        input_output_aliases={3: 0},
        compiler_params=_params("parallel", "parallel"),
    )(q, k, v, o_att)


def _gmlp_kernel(z_ref, g_ref, b_ref, ws_ref, bs_ref, o_ref, *, width, chunks):
    z = jax.nn.gelu(z_ref[...])
    u = z[:, :width]
    v = (_ln(z[:, width:]) * g_ref[...] + b_ref[...]).astype(BF16)
    for c in range(chunks):
        rows = slice(c * CHUNK, (c + 1) * CHUNK)
        for g in range(width // GMLP_GROUP_DIM):
            cols = slice(g * GMLP_GROUP_DIM, (g + 1) * GMLP_GROUP_DIM)
            mixed = jnp.dot(ws_ref[g], v[rows, cols], preferred_element_type=F32) + bs_ref[:, cols]
            o_ref[rows, cols] = (u[rows, cols] * mixed).astype(o_ref.dtype)


def _gmlp(z, ln_g, ln_b, w_s, b_s, n_rows):
    n, two_w = z.shape
    width = two_w // 2
    groups = width // GMLP_GROUP_DIM
    chunks = TM // CHUNK
    bs = jnp.repeat(b_s.T, GMLP_GROUP_DIM, axis=1)
    vec = pl.BlockSpec((1, width), lambda i: (0, 0))
    return pl.pallas_call(
        functools.partial(_gmlp_kernel, width=width, chunks=chunks),
        out_shape=jax.ShapeDtypeStruct((n, width), BF16),
        grid=(n_rows // TM,),
        in_specs=[pl.BlockSpec((TM, two_w), lambda i: (i, 0)), vec, vec,
                  pl.BlockSpec((groups, CHUNK, CHUNK), lambda i: (0, 0, 0)),
                  pl.BlockSpec((CHUNK, width), lambda i: (0, 0))],
        out_specs=pl.BlockSpec((TM, width), lambda i: (i, 0)),
        compiler_params=_params("parallel"),
    )(z, ln_g.reshape(1, width), ln_b.reshape(1, width), w_s.astype(BF16), bs)


def _s5_gen_kernel(lr_ref, li_ref, ldt_ref, bre_ref, bim_ref, cre_ref, cim_ref, m_ref, wst_ref, et_ref, a_ref):
    nt = (((1,), (1,)), ((), ()))
    hi = lax.Precision.HIGHEST
    nb = SSM_BLOCK
    lane = lax.broadcasted_iota(jnp.int32, (SSM_GROUP, SSM_ROW), 1)
    m_rows = [None] * nb
    for d in range(2):
        lr = lr_ref[d:d + 1, :]
        li = li_ref[d:d + 1, :]
        dt = jnp.exp(ldt_ref[d:d + 1, :])
        mag = jnp.exp(lr * dt)
        ar = mag * jnp.cos(li * dt)
        ai = mag * jnp.sin(li * dt)
        den = lr * lr + li * li
        fr = ((ar - 1.0) * lr + ai * li) / den
        fi = (ai * lr - (ar - 1.0) * li) / den
        br = bre_ref[d]
        bi = bim_ref[d]
        bbr = fr * br - fi * bi
        bbi = fr * bi + fi * br
        cr = cre_ref[d]
        ci = cim_ref[d]
        pr = [jnp.ones_like(ar)]
        pi = [jnp.zeros_like(ai)]
        for _ in range(nb):
            pr.append(pr[-1] * ar - pi[-1] * ai)
            pi.append(pr[-2] * ai + pi[-1] * ar)
        a_ref[2 * d] = pr[nb]
        a_ref[2 * d + 1] = pi[nb]
        e_state = [(t + 1) if d == 0 else (nb - t) for t in range(nb)]
        e_in = [(nb - 1 - s) if d == 0 else s for s in range(nb)]
        e_lag = [e if d == 0 else (nb - 1 - e) for e in range(nb)]
        for s in range(nb):
            rows = slice(s * SSM_GROUP, (s + 1) * SSM_GROUP)
            wr, wi = pr[e_in[s]], pi[e_in[s]]
            wst_ref[2 * d, rows, :] = (wr * bbr - wi * bbi).astype(wst_ref.dtype)
            wst_ref[2 * d + 1, rows, :] = (wr * bbi + wi * bbr).astype(wst_ref.dtype)
            er, ei = pr[e_state[s]], pi[e_state[s]]
            et_ref[2 * d, rows, :] = (cr * er - ci * ei).astype(et_ref.dtype)
            et_ref[2 * d + 1, rows, :] = (-(cr * ei + ci * er)).astype(et_ref.dtype)
        lag_r = jnp.concatenate([cr * pr[e] - ci * pi[e] for e in e_lag], axis=0)
        lag_i = jnp.concatenate([-(cr * pi[e] + ci * pr[e]) for e in e_lag], axis=0)
        kbase = (lax.dot_general(bbr, lag_r, nt, precision=hi, preferred_element_type=F32)
                 + lax.dot_general(bbi, lag_i, nt, precision=hi, preferred_element_type=F32))
        for s in range(nb):
            if d == 0:
                sh = s * SSM_GROUP
                row = kbase if sh == 0 else jnp.where(lane >= sh, pltpu.roll(kbase, sh, 1), 0.0)
                m_rows[s] = row
            else:
                sh = (nb - 1 - s) * SSM_GROUP
                row = kbase if sh == 0 else jnp.where(lane < SSM_ROW - sh, pltpu.roll(kbase, SSM_ROW - sh, 1), 0.0)
                m_rows[s] = m_rows[s] + row
    for s in range(nb):
        m_ref[s * SSM_GROUP:(s + 1) * SSM_GROUP, :] = m_rows[s].astype(m_ref.dtype)


def _s5_gen(lam_re, lam_im, log_dt, b_re, b_im, c_re, c_im):
    _, groups, p = lam_re.shape

    def gp(x):
        return jnp.swapaxes(x, 0, 1)

    ldt = jnp.broadcast_to(log_dt[:, :, None], lam_re.shape)
    vec = pl.BlockSpec((None, 2, p), lambda g: (g, 0, 0))
    mat = pl.BlockSpec((None, 2, SSM_GROUP, p), lambda g: (g, 0, 0, 0))
    return pl.pallas_call(
        _s5_gen_kernel,
        out_shape=[jax.ShapeDtypeStruct((groups, SSM_ROW, SSM_ROW), BF16),
                   jax.ShapeDtypeStruct((groups, 4, SSM_ROW, p), BF16),
                   jax.ShapeDtypeStruct((groups, 4, SSM_ROW, p), BF16),
                   jax.ShapeDtypeStruct((groups, 4, 1, p), F32)],
        grid=(groups,),
        in_specs=[vec, vec, vec, mat, mat, mat, mat],
        out_specs=[pl.BlockSpec((None, SSM_ROW, SSM_ROW), lambda g: (g, 0, 0)),
                   pl.BlockSpec((None, 4, SSM_ROW, p), lambda g: (g, 0, 0, 0)),
                   pl.BlockSpec((None, 4, SSM_ROW, p), lambda g: (g, 0, 0, 0)),
                   pl.BlockSpec((None, 4, 1, p), lambda g: (g, 0, 0, 0))],
        compiler_params=_params("parallel"),
    )(gp(lam_re), gp(lam_im), gp(ldt), gp(jnp.swapaxes(b_re, 2, 3)), gp(jnp.swapaxes(b_im, 2, 3)),
      gp(c_re), gp(c_im))


def _pair_diag(w):
    g, k, r, p = w.shape
    w = w.reshape(g // 2, 2, k, r, p)
    eye = jnp.eye(2, dtype=w.dtype)
    return jnp.einsum("gjkrp,jJ->gkjrJp", w, eye).reshape(g // 2, k, 2 * r, 2 * p)


def _s5_state_kernel(u_ref, wst_ref, sfr_ref, sfi_ref, sbr_ref, sbi_ref):
    u = u_ref[...]
    for k, o_ref in enumerate((sfr_ref, sfi_ref, sbr_ref, sbi_ref)):
        o_ref[...] = jnp.dot(u, wst_ref[k], preferred_element_type=F32)


def _s5_states(u, wst_pair):
    rows = u.shape[0]
    pairs, _, kdim, lanes = wst_pair.shape
    out = jax.ShapeDtypeStruct((rows, pairs * lanes), F32)
    ospec = pl.BlockSpec((rows, lanes), lambda g: (0, g))
    return pl.pallas_call(
        _s5_state_kernel,
        out_shape=[out] * 4,
        grid=(pairs,),
        in_specs=[pl.BlockSpec((rows, kdim), lambda g: (0, g)),
                  pl.BlockSpec((None, 4, kdim, lanes), lambda g: (g, 0, 0, 0))],
        out_specs=[ospec] * 4,
        compiler_params=_params("parallel"),
    )(u, wst_pair)


def _s5_scan_kernel(sfr_ref, sfi_ref, sbr_ref, sbi_ref, a_ref, hfr_ref, hfi_ref, hbr_ref, hbi_ref,
                    *, n_batch, n_blocks, n_ctx_blocks):
    afr, afi, abr, abi = (a_ref[k:k + 1, :] for k in range(4))
    zero = jnp.zeros_like(afr)

    def step(t, carry):
        new = []
        kb = jnp.where(t < n_ctx_blocks, n_ctx_blocks - 1 - t, n_blocks - 1 - (t - n_ctx_blocks))
        for b in range(n_batch):
            fr, fi, br, bi = carry[4 * b:4 * b + 4]
            rf = pl.ds(b * n_blocks + t, 1)
            rb = pl.ds(b * n_blocks + kb, 1)
            hfr_ref[rf, :] = fr
            hfi_ref[rf, :] = fi
            hbr_ref[rb, :] = br
            hbi_ref[rb, :] = bi
            new += [afr * fr - afi * fi + sfr_ref[rf, :], afr * fi + afi * fr + sfi_ref[rf, :],
                    abr * br - abi * bi + sbr_ref[rb, :], abr * bi + abi * br + sbi_ref[rb, :]]
        return tuple(new)

    lax.fori_loop(0, n_blocks, step, (zero,) * (4 * n_batch))


def _s5_scan(s_parts, a_rows, n_batch, n_blocks, n_ctx_blocks):
    rows, lanes = s_parts[0].shape
    tl = min(lanes, 256)
    spec = pl.BlockSpec((rows, tl), lambda j: (0, j))
    return pl.pallas_call(
        functools.partial(_s5_scan_kernel, n_batch=n_batch, n_blocks=n_blocks, n_ctx_blocks=n_ctx_blocks),
        out_shape=[jax.ShapeDtypeStruct((rows, lanes), F32)] * 4,
        grid=(lanes // tl,),
        in_specs=[spec] * 4 + [pl.BlockSpec((4, tl), lambda j: (0, j))],
        out_specs=[spec] * 4,
        compiler_params=_params("parallel"),
    )(*s_parts, a_rows)


def _s5_out_kernel(u_ref, m_ref, et_ref, hfr_ref, hfi_ref, hbr_ref, hbi_ref, y_ref):
    nt = (((1,), (1,)), ((), ()))
    y = jnp.dot(u_ref[...], m_ref[...], preferred_element_type=F32)
    for k, h_ref in enumerate((hfr_ref, hfi_ref, hbr_ref, hbi_ref)):
        y += lax.dot_general(h_ref[...].astype(BF16), et_ref[k], nt, preferred_element_type=F32)
    y_ref[...] = y


def _s5_outputs(u, m_pair, et_pair, h_parts):
    rows = u.shape[0]
    pairs, _, kdim, lanes = et_pair.shape
    hspec = pl.BlockSpec((rows, lanes), lambda g: (0, g))
    return pl.pallas_call(
        _s5_out_kernel,
        out_shape=jax.ShapeDtypeStruct(u.shape, F32),
        grid=(pairs,),
        in_specs=[pl.BlockSpec((rows, kdim), lambda g: (0, g)),
                  pl.BlockSpec((None, kdim, kdim), lambda g: (g, 0, 0)),
                  pl.BlockSpec((None, 4, kdim, lanes), lambda g: (g, 0, 0, 0)),
                  hspec, hspec, hspec, hspec],
        out_specs=pl.BlockSpec((rows, kdim), lambda g: (0, g)),
        compiler_params=_params("parallel"),
    )(u, m_pair, et_pair, *h_parts)


def _s5_glu_kernel(ys_ref, us_ref, d_ref, w_ref, b_ref, o_ref):
    y = jax.nn.gelu(ys_ref[...] + d_ref[...] * us_ref[...])
    gate = jax.nn.sigmoid(jnp.dot(y.astype(BF16), w_ref[...], preferred_element_type=F32) + b_ref[...])
    o_ref[...] = (y * gate).astype(o_ref.dtype)


def _s5_glu(y_s, us, d_skip, w_glu, b_glu, n_rows):
    n, w = us.shape
    row = pl.BlockSpec((TM, w), lambda i: (i, 0))
    vec = pl.BlockSpec((1, w), lambda i: (0, 0))
    return pl.pallas_call(
        _s5_glu_kernel,
        out_shape=jax.ShapeDtypeStruct((n, w), BF16),
        grid=(n_rows // TM,),
        in_specs=[row, row, vec, pl.BlockSpec((w, w), lambda i: (0, 0)), vec],
        out_specs=row,
        compiler_params=_params("parallel"),
    )(y_s, us, d_skip.reshape(1, w), w_glu, b_glu.reshape(1, w))


def _s5(us, n_batch, seq, ctx_len, lam_re, lam_im, log_dt, b_re, b_im, c_re, c_im):
    width = us.shape[1]
    groups = width // SSM_GROUP
    n_lat = n_batch * seq
    t_all = ctx_len + seq
    n_blocks = t_all // SSM_BLOCK
    m, wst, et, a16 = _s5_gen(lam_re, lam_im, log_dt, b_re, b_im, c_re, c_im)
    m_pair = _pair_diag(m[:, None])[:, 0]
    wst_pair = _pair_diag(wst)
    et_pair = _pair_diag(et)
    a_rows = jnp.transpose(a16[:, :, 0, :], (1, 0, 2)).reshape(4, groups * SSM_STATE)
    seq_all = jnp.concatenate([us[n_lat:].reshape(n_batch, ctx_len, width),
                               us[:n_lat].reshape(n_batch, seq, width)], axis=1)
    u = seq_all.reshape(n_batch, n_blocks, SSM_BLOCK, groups, SSM_GROUP)
    u = jnp.transpose(u, (0, 1, 3, 2, 4)).reshape(n_batch * n_blocks, groups * SSM_ROW).astype(BF16)
    s_parts = _s5_states(u, wst_pair)
    h_parts = _s5_scan(s_parts, a_rows, n_batch, n_blocks, ctx_len // SSM_BLOCK)
    y = _s5_outputs(u, m_pair, et_pair, h_parts)
    y = y.reshape(n_batch, n_blocks, groups, SSM_BLOCK, SSM_GROUP)
    y = jnp.transpose(y, (0, 1, 3, 2, 4)).reshape(n_batch, t_all, width)
    return jnp.concatenate([y[:, ctx_len:].reshape(n_lat, width), y[:, :ctx_len].reshape(n_batch * ctx_len, width)],
                           axis=0)


def _rope_tables(seq):
    t = jnp.arange(seq)
    quarter = HEAD_DIM // 4
    inv = ROPE_BASE ** (-jnp.arange(quarter, dtype=F32) / quarter)
    ang_r = (t // GRID_W).astype(F32)[:, None] * inv[None, :]
    ang_c = (t % GRID_W).astype(F32)[:, None] * inv[None, :]
    cos = jnp.concatenate([jnp.cos(ang_r)] * 2 + [jnp.cos(ang_c)] * 2, axis=1)
    sin = jnp.concatenate([-jnp.sin(ang_r), jnp.sin(ang_r), -jnp.sin(ang_c), jnp.sin(ang_c)], axis=1)
    return cos, sin


def kernel(x, c, ctx, c_ctx, w_mod, b_mod, ln_g, ln_b, ffn1_w_in, ffn1_w_out, ffn2_w_in, ffn2_w_out, w_in, rpb, gm_ln_g, gm_ln_b, gm_ws, gm_bs, ssm_lam_re, ssm_lam_im, ssm_log_dt, ssm_b_re, ssm_b_im, ssm_c_re, ssm_c_im, ssm_d, ssm_w_glu, ssm_b_glu, w_pa, w_pb, w_pc, w_o):
    n_batch, seq, d = x.shape
    ctx_len = ctx.shape[1]
    depth = w_mod.shape[0]
    att_w = w_pb.shape[1]
    gm_w = w_pa.shape[1]
    ssm_w = w_pc.shape[1]
    q_off, k_off, v_off = 0, att_w, 2 * att_w
    gm_off = 3 * att_w
    ssm_off = gm_off + 2 * gm_w
    gate_off = ssm_off + ssm_w
    alpha = (2.0 * depth) ** 0.25
    n_lat = n_batch * seq
    n_all = n_lat + n_batch * ctx_len
    assert seq % TM == 0 and (n_batch * ctx_len) % TM == 0 and seq % (ATT_ROWS * GRID_W) == 0
    assert n_batch + 1 <= MOD_ROWS and ctx_len % CHUNK == 0 and n_lat % ctx_len == 0

    seg = _seg_map(seq // TM_ROW, n_batch)
    tiles_per_batch = seq // TM

    xs = jnp.concatenate([x.reshape(n_lat, d), ctx.reshape(n_batch * ctx_len, d)], axis=0)
    c_all = jnp.zeros((MOD_ROWS, d), F32).at[:n_batch].set(c).at[n_batch].set(c_ctx)
    mods = _mods(c_all, w_mod, b_mod)[:, :n_batch + 1].reshape(depth, n_batch + 1, N_MOD, d)
    cos, sin = _rope_tables(seq)

    h = _modulate(xs, mods[0], 0, 1, seg)
    for l in range(depth):
        last = l == depth - 1
        n_mix = n_lat if last else n_all
        w1_in, w1_out = ffn1_w_in[l].astype(BF16), ffn1_w_out[l].astype(BF16)
        w2_in, w2_out = ffn2_w_in[l].astype(BF16), ffn2_w_out[l].astype(BF16)
        w_in_l = w_in[l].astype(BF16)

        y = _mm(_ffn_in(h, w1_in, n_all), w1_out, 0, d, F32, n_all)
        xs, h = _norm(xs, y, mods[l], 2, 0.5, ln_g[l, 0], ln_b[l, 0], alpha, seg, n_all,
                      mods_next=mods[l], i_shift=3, i_scale=4)

        q, q_rot = _mm_rope(h, w_in_l, q_off, att_w, cos, sin, tiles_per_batch, n_mix)
        k, k_rot = _mm_rope(h, w_in_l, k_off, att_w, cos, sin, tiles_per_batch, n_all)
        v = _mm(h, w_in_l, v_off, att_w, BF16, n_all)
        z = _mm(h, w_in_l, gm_off, 2 * gm_w, F32, n_mix)
        us = _mm(h, w_in_l, ssm_off, ssm_w, F32, n_all)
        gates = _mm(h, w_in_l, gate_off, N_BRANCH * d, BF16, n_mix, act="sigmoid")

        o_att = _attention(q, q_rot, k, k_rot, v, _rpb_table(rpb[l]), n_batch, seq, ctx_len)
        if not last:
            o_att = _ctx_attention(q, k, v, o_att, n_batch, seq, ctx_len)
        o_gm = _gmlp(z, gm_ln_g[l], gm_ln_b[l], gm_ws[l], gm_bs[l], n_mix)
        y_s = _s5(us, n_batch, seq, ctx_len, ssm_lam_re[l], ssm_lam_im[l], ssm_log_dt[l],
                  ssm_b_re[l], ssm_b_im[l], ssm_c_re[l], ssm_c_im[l])
        o_ssm = _s5_glu(y_s, us, ssm_d[l], ssm_w_glu[l].astype(BF16), ssm_b_glu[l], n_mix)
        merged = _merge(o_gm, o_att, o_ssm, w_pa[l].astype(BF16), w_pb[l].astype(BF16), w_pc[l].astype(BF16),
                        gates, n_mix)
        y = _mm(merged, w_o[l].astype(BF16), 0, d, F32, n_mix)
        xs, h = _norm(xs, y, mods[l], 5, 1.0, ln_g[l, 1], ln_b[l, 1], alpha, seg, n_mix,
                      mods_next=mods[l], i_shift=6, i_scale=7)

        y = _mm(_ffn_in(h, w2_in, n_mix), w2_out, 0, d, F32, n_mix)
        if last:
            xs, _ = _norm(xs, y, mods[l], 8, 0.5, ln_g[l, 2], ln_b[l, 2], alpha, seg, n_mix)
        else:
            xs, h = _norm(xs, y, mods[l], 8, 0.5, ln_g[l, 2], ln_b[l, 2], alpha, seg, n_mix,
                          mods_next=mods[l + 1], i_shift=0, i_scale=1)
    return xs[:n_lat].reshape(n_batch, seq, d)
```

```python
import functools

import jax
import jax.numpy as jnp
from jax import lax
from jax.experimental import pallas as pl
from jax.experimental.pallas import tpu as pltpu

F32 = jnp.float32
BF16 = jnp.bfloat16

GRID_W = 64
N_SUB = 3
N_MOD = 3 * N_SUB
HEAD_DIM = 128
WIN_H = 8
WIN_W = 16
ROPE_BASE = 10000.0
CHUNK = 128
GMLP_GROUP_DIM = 128
SSM_GROUP = 16
SSM_STATE = 64
N_BRANCH = 3
LN_EPS = 1e-6
NEG_INF = -1e30

LANES = 128
SSM_BLOCK = 16
SSM_ROW = SSM_BLOCK * SSM_GROUP
MOD_ROWS = 8
VMEM_LIMIT_BYTES = 56 * 1024 * 1024

TM = 512
TM_ROW = 256
TN = 512
ATT_ROWS = 8
ATT_KROWS = ATT_ROWS + WIN_H
NT_DIMS = (((1,), (1,)), ((), ()))


def _params(*sem):
    return pltpu.CompilerParams(dimension_semantics=sem, vmem_limit_bytes=VMEM_LIMIT_BYTES)


def _ln(x):
    mu = jnp.mean(x, axis=-1, keepdims=True)
    xc = x - mu
    var = jnp.mean(xc * xc, axis=-1, keepdims=True)
    return xc * lax.rsqrt(var + LN_EPS)


def _col_tile(n_cols, col_off=0):
    for tn in range(TN, 0, -LANES):
        if n_cols % tn == 0 and col_off % tn == 0:
            return tn
    raise ValueError(f"column group ({col_off}, {n_cols}) is not lane aligned")


def _seg_map(tiles_per_batch, n_batch):
    return lambda i: jnp.minimum(i // tiles_per_batch, n_batch)


def _cast_weight_once(w_ref, wb_ref):
    @pl.when(pl.program_id(1) == 0)
    def _():
        wb_ref[...] = w_ref[...].astype(BF16)


def _mods_kernel(c_ref, w_ref, b_ref, o_ref):
    s = jax.nn.silu(c_ref[...]).astype(BF16)
    o_ref[...] = jnp.dot(s, w_ref[...].astype(BF16), preferred_element_type=F32) + b_ref[...]


def _mods(c_all, w_mod, b_mod):
    depth, d, nm = w_mod.shape
    tn = _col_tile(nm)
    return pl.pallas_call(
        _mods_kernel,
        name="mods",
        out_shape=jax.ShapeDtypeStruct((depth, MOD_ROWS, nm), F32),
        grid=(depth, nm // tn),
        in_specs=[pl.BlockSpec((MOD_ROWS, d), lambda l, j: (0, 0)),
                  pl.BlockSpec((None, d, tn), lambda l, j: (l, 0, j)),
                  pl.BlockSpec((None, 1, tn), lambda l, j: (l, 0, j))],
        out_specs=pl.BlockSpec((None, MOD_ROWS, tn), lambda l, j: (l, 0, j)),
        compiler_params=_params("parallel", "parallel"),
    )(c_all, w_mod, b_mod.reshape(depth, 1, nm))


def _split_specs(n_lat_tiles, d):
    return [pl.BlockSpec((TM_ROW, d), lambda i: (jnp.minimum(i, n_lat_tiles - 1), 0)),
            pl.BlockSpec((TM_ROW, d), lambda i: (jnp.maximum(i - n_lat_tiles, 0), 0))]


def _modulate_kernel(xl_ref, xc_ref, mod_ref, h_ref, *, n_lat_tiles, i_shift, i_scale):
    x = jnp.where(pl.program_id(0) < n_lat_tiles, xl_ref[...], xc_ref[...])
    h = _ln(x) * (1.0 + mod_ref[i_scale:i_scale + 1, :]) + mod_ref[i_shift:i_shift + 1, :]
    h_ref[...] = h.astype(BF16)


def _modulate(x_lat, x_ctx, mods, i_shift, i_scale, seg):
    n_lat, d = x_lat.shape
    n = n_lat + x_ctx.shape[0]
    n_lat_tiles = n_lat // TM_ROW
    return pl.pallas_call(
        functools.partial(_modulate_kernel, n_lat_tiles=n_lat_tiles, i_shift=i_shift, i_scale=i_scale),
        name="modulate",
        out_shape=jax.ShapeDtypeStruct((n, d), BF16),
        grid=(n // TM_ROW,),
        in_specs=_split_specs(n_lat_tiles, d) + [pl.BlockSpec((None, N_MOD, d), lambda i: (seg(i), 0, 0))],
        out_specs=pl.BlockSpec((TM_ROW, d), lambda i: (i, 0)),
        compiler_params=_params("parallel"),
    )(x_lat, x_ctx, mods)


def _norm_kernel(*refs, n_lat_tiles, alpha, coef, i_gate, i_shift, i_scale):
    if n_lat_tiles is None:
        x_ref, y_ref, modg_ref, g_ref, b_ref, *rest = refs
        x = x_ref[...]
    else:
        xl_ref, xc_ref, y_ref, modg_ref, g_ref, b_ref, *rest = refs
        x = jnp.where(pl.program_id(0) < n_lat_tiles, xl_ref[...], xc_ref[...])
    t = alpha * x + (coef * modg_ref[i_gate:i_gate + 1, :]) * y_ref[...].astype(F32)
    xn = _ln(t) * g_ref[...] + b_ref[...]
    if i_shift is None:
        (xo_ref,) = rest
        xo_ref[...] = xn
    else:
        modn_ref, xo_ref, h_ref = rest
        xo_ref[...] = xn
        h = _ln(xn) * (1.0 + modn_ref[i_scale:i_scale + 1, :]) + modn_ref[i_shift:i_shift + 1, :]
        h_ref[...] = h.astype(BF16)


def _norm(x, y, mods_gate, i_gate, coef, g, b, alpha, seg, n_rows, mods_next=None, i_shift=None, i_scale=None,
          n_out=None):
    split = isinstance(x, tuple)
    d = y.shape[1]
    n_out = y.shape[0] if n_out is None else n_out
    row = pl.BlockSpec((TM_ROW, d), lambda i: (i, 0))
    vec = pl.BlockSpec((1, d), lambda i: (0, 0))
    mod = pl.BlockSpec((None, N_MOD, d), lambda i: (seg(i), 0, 0))
    n_lat_tiles = x[0].shape[0] // TM_ROW if split else None
    in_specs = (_split_specs(n_lat_tiles, d) if split else [row]) + [row, mod, vec, vec]
    args = (list(x) if split else [x]) + [y, mods_gate, g.reshape(1, d), b.reshape(1, d)]
    out_shape = [jax.ShapeDtypeStruct((n_out, d), F32)]
    out_specs = [row]
    if i_shift is not None:
        in_specs.append(mod)
        args.append(mods_next)
        out_shape.append(jax.ShapeDtypeStruct((n_out, d), BF16))
        out_specs.append(row)
    out = pl.pallas_call(
        functools.partial(_norm_kernel, n_lat_tiles=n_lat_tiles, alpha=alpha, coef=coef, i_gate=i_gate,
                          i_shift=i_shift, i_scale=i_scale),
        name="norm",
        out_shape=out_shape,
        grid=(n_rows // TM_ROW,),
        in_specs=in_specs,
        out_specs=out_specs,
        compiler_params=_params("parallel"),
    )(*args)
    return out if i_shift is not None else (out[0], None)


def _ffn_in_kernel(h_ref, wg_ref, wu_ref, o_ref, wgb_ref, wub_ref):
    _cast_weight_once(wg_ref, wgb_ref)
    _cast_weight_once(wu_ref, wub_ref)
    h = h_ref[...]
    g = jnp.dot(h, wgb_ref[...], preferred_element_type=F32)
    u = jnp.dot(h, wub_ref[...], preferred_element_type=F32)
    o_ref[...] = (jax.nn.silu(g) * u).astype(o_ref.dtype)


def _ffn_in(h, w, n_rows):
    n, d = h.shape
    dff = w.shape[1] // 2
    tn = _col_tile(dff)
    return pl.pallas_call(
        _ffn_in_kernel,
        name="ffn_in",
        out_shape=jax.ShapeDtypeStruct((n, dff), BF16),
        grid=(dff // tn, n_rows // TM),
        in_specs=[pl.BlockSpec((TM, d), lambda j, i: (i, 0)),
                  pl.BlockSpec((d, tn), lambda j, i: (0, j)),
                  pl.BlockSpec((d, tn), lambda j, i: (0, j + dff // tn))],
        out_specs=pl.BlockSpec((TM, tn), lambda j, i: (i, j)),
        scratch_shapes=[pltpu.VMEM((d, tn), BF16), pltpu.VMEM((d, tn), BF16)],
        compiler_params=_params("arbitrary", "arbitrary"),
    )(h, w, w)


def _mm_kernel(a_ref, w_ref, o_ref, wb_ref, *, act):
    _cast_weight_once(w_ref, wb_ref)
    acc = jnp.dot(a_ref[...], wb_ref[...], preferred_element_type=F32)
    if act == "sigmoid":
        acc = jax.nn.sigmoid(acc)
    o_ref[...] = acc.astype(o_ref.dtype)


def _mm(a, w, col_off, n_cols, out_dtype, n_rows, name, act=None):
    n, k = a.shape
    tn = _col_tile(n_cols, col_off)
    joff = col_off // tn
    return pl.pallas_call(
        functools.partial(_mm_kernel, act=act),
        name=name,
        out_shape=jax.ShapeDtypeStruct((n, n_cols), out_dtype),
        grid=(n_cols // tn, n_rows // TM),
        in_specs=[pl.BlockSpec((TM, k), lambda j, i: (i, 0)),
                  pl.BlockSpec((k, tn), lambda j, i: (0, j + joff))],
        out_specs=pl.BlockSpec((TM, tn), lambda j, i: (i, j)),
        scratch_shapes=[pltpu.VMEM((k, tn), BF16)],
        compiler_params=_params("arbitrary", "arbitrary"),
    )(a, w)


def _mm_rope_kernel(a_ref, w_ref, cos_ref, sin_ref, o_ref, orot_ref, wb_ref):
    _cast_weight_once(w_ref, wb_ref)
    acc = jnp.dot(a_ref[...], wb_ref[...], preferred_element_type=F32)
    o_ref[...] = acc.astype(o_ref.dtype)
    tn = acc.shape[1]
    reps = tn // HEAD_DIM
    cos = jnp.concatenate([cos_ref[...]] * reps, axis=1)
    sin = jnp.concatenate([sin_ref[...]] * reps, axis=1)
    lane = lax.broadcasted_iota(jnp.int32, acc.shape, 1)
    quarter = HEAD_DIM // 4
    first = (lane % (2 * quarter)) < quarter
    partner = jnp.where(first, pltpu.roll(acc, tn - quarter, 1), pltpu.roll(acc, quarter, 1))
    orot_ref[...] = (acc * cos + partner * sin).astype(orot_ref.dtype)


def _mm_rope(a, w, col_off, n_cols, cos, sin, tiles_per_batch, n_rows):
    n, k = a.shape
    tn = _col_tile(n_cols, col_off)
    joff = col_off // tn
    out = jax.ShapeDtypeStruct((n, n_cols), BF16)
    tab = pl.BlockSpec((TM, HEAD_DIM), lambda j, i: (i % tiles_per_batch, 0))
    blk = pl.BlockSpec((TM, tn), lambda j, i: (i, j))
    return pl.pallas_call(
        _mm_rope_kernel,
        name="proj_rope",
        out_shape=[out, out],
        grid=(n_cols // tn, n_rows // TM),
        in_specs=[pl.BlockSpec((TM, k), lambda j, i: (i, 0)),
                  pl.BlockSpec((k, tn), lambda j, i: (0, j + joff)),
                  tab, tab],
        out_specs=[blk, blk],
        scratch_shapes=[pltpu.VMEM((k, tn), BF16)],
        compiler_params=_params("arbitrary", "arbitrary"),
    )(a, w, cos, sin)


def _merge_kernel(oa_ref, ob_ref, oc_ref, wa_ref, wb_ref, wc_ref, ga_ref, gb_ref, gc_ref, o_ref,
                  wab_ref, wbb_ref, wcb_ref):
    _cast_weight_once(wa_ref, wab_ref)
    _cast_weight_once(wb_ref, wbb_ref)
    _cast_weight_once(wc_ref, wcb_ref)
    t = ga_ref[...].astype(F32) * jnp.dot(oa_ref[...], wab_ref[...], preferred_element_type=F32)
    t += gb_ref[...].astype(F32) * jnp.dot(ob_ref[...], wbb_ref[...], preferred_element_type=F32)
    t += gc_ref[...].astype(F32) * jnp.dot(oc_ref[...], wcb_ref[...], preferred_element_type=F32)
    o_ref[...] = t.astype(o_ref.dtype)


def _merge(o_a, o_b, o_c, w_pa, w_pb, w_pc, gates, n_rows):
    n = o_a.shape[0]
    d = w_pa.shape[1]
    tn = _col_tile(d)
    nj = d // tn

    def act(o):
        return pl.BlockSpec((TM, o.shape[1]), lambda j, i: (i, 0))

    def wgt(w):
        return pl.BlockSpec((w.shape[0], tn), lambda j, i: (0, j))

    def gate(k):
        return pl.BlockSpec((TM, tn), lambda j, i: (i, j + k * nj))

    return pl.pallas_call(
        _merge_kernel,
        name="merge",
        out_shape=jax.ShapeDtypeStruct((n, d), BF16),
        grid=(nj, n_rows // TM),
        in_specs=[act(o_a), act(o_b), act(o_c), wgt(w_pa), wgt(w_pb), wgt(w_pc), gate(0), gate(1), gate(2)],
        out_specs=pl.BlockSpec((TM, tn), lambda j, i: (i, j)),
        scratch_shapes=[pltpu.VMEM((w.shape[0], tn), BF16) for w in (w_pa, w_pb, w_pc)],
        compiler_params=_params("arbitrary", "arbitrary"),
    )(o_a, o_b, o_c, w_pa, w_pb, w_pc, gates, gates, gates)


def _attn_block_geometry(kind):
    half = WIN_H // 2
    if kind == 0:
        return 0, [max(i - half, 0) for i in range(ATT_ROWS)]
    if kind == 1:
        return -half, list(range(ATT_ROWS))
    return -(ATT_KROWS - ATT_ROWS), [min(half + i, ATT_KROWS - WIN_H) for i in range(ATT_ROWS)]


def _rpb_table_kernel(rpb_ref, o_ref):
    shape = (GRID_W, 2 * GRID_W)
    cq = lax.broadcasted_iota(jnp.int32, shape, 0)
    lane = lax.broadcasted_iota(jnp.int32, shape, 1)
    ck = lane % GRID_W
    second = lane >= GRID_W
    start = jnp.clip(cq - WIN_W // 2, 0, GRID_W - WIN_W)
    in_win = (ck >= start) & (ck < start + WIN_W)
    neg = jnp.full(shape, NEG_INF, F32)
    n_bias_rows = 2 * WIN_H - 1

    def half_tile(a, lane_off):
        row = jnp.broadcast_to(rpb_ref[a:a + 1, :], shape)
        return pltpu.roll(row, (lane_off - (WIN_W - 1)) % LANES, 1, stride=1, stride_axis=0)

    lo_half = [half_tile(a, 0) for a in range(n_bias_rows)]
    hi_half = [half_tile(a, GRID_W) for a in range(n_bias_rows)]
    for kind in range(3):
        diff, first_key_row = _attn_block_geometry(kind)
        for i in range(ATT_ROWS):
            lo = first_key_row[i]
            for pair in range(ATT_KROWS // 2):
                j0, j1 = 2 * pair, 2 * pair + 1
                t0 = lo_half[j0 - i + diff + WIN_H - 1] if lo <= j0 < lo + WIN_H else neg
                t1 = hi_half[j1 - i + diff + WIN_H - 1] if lo <= j1 < lo + WIN_H else neg
                tile = jnp.where(in_win, jnp.where(second, t1, t0), neg)
                o_ref[kind, i * GRID_W:(i + 1) * GRID_W, pair * 2 * GRID_W:(pair + 1) * 2 * GRID_W] = tile


def _rpb_table(rpb):
    heads, nr, nc = rpb.shape
    rpb_pad = jnp.zeros((heads, 2 * WIN_H, LANES), F32).at[:, :nr, :nc].set(rpb)
    tq, tk = ATT_ROWS * GRID_W, ATT_KROWS * GRID_W
    return pl.pallas_call(
        _rpb_table_kernel,
        name="rpb_table",
        out_shape=jax.ShapeDtypeStruct((heads, 3, tq, tk), F32),
        grid=(heads,),
        in_specs=[pl.BlockSpec((None, 2 * WIN_H, LANES), lambda h: (h, 0, 0))],
        out_specs=pl.BlockSpec((None, 3, tq, tk), lambda h: (h, 0, 0, 0)),
        compiler_params=_params("parallel"),
    )(rpb_pad)


def _attn_kernel(q_ref, qr_ref, kr_ref, v_ref, kc_ref, vc_ref, bias_ref, o_ref, *, n_grid_rows, scale):
    rb = pl.program_id(2)
    kr0 = jnp.clip(rb * ATT_ROWS - WIN_H // 2, 0, n_grid_rows - ATT_KROWS)
    keys = pl.ds(pl.multiple_of(kr0 * GRID_W, GRID_W), ATT_KROWS * GRID_W)
    s = lax.dot_general(qr_ref[...], kr_ref[keys, :], NT_DIMS, preferred_element_type=F32) * scale + bias_ref[...]
    sc = lax.dot_general(q_ref[...], kc_ref[...], NT_DIMS, preferred_element_type=F32) * scale
    m = jnp.maximum(jnp.max(s, axis=-1, keepdims=True), jnp.max(sc, axis=-1, keepdims=True))
    p = jnp.exp(s - m)
    pc = jnp.exp(sc - m)
    denom = jnp.sum(p, axis=-1, keepdims=True) + jnp.sum(pc, axis=-1, keepdims=True)
    o = jnp.dot(p.astype(BF16), v_ref[keys, :], preferred_element_type=F32)
    o += jnp.dot(pc.astype(BF16), vc_ref[...], preferred_element_type=F32)
    o_ref[...] = (o / denom).astype(o_ref.dtype)


def _attention(q, q_rot, k, k_rot, v, bias, n_batch, seq, ctx_len):
    n, width = q.shape
    heads = width // HEAD_DIM
    n_grid_rows = seq // GRID_W
    tq, tk = ATT_ROWS * GRID_W, ATT_KROWS * GRID_W
    blocks = seq // tq
    ctx0 = n_batch * seq // ctx_len
    qspec = pl.BlockSpec((tq, HEAD_DIM), lambda b, h, r: (b * blocks + r, h))
    kspec = pl.BlockSpec((seq, HEAD_DIM), lambda b, h, r: (b, h))
    cspec = pl.BlockSpec((ctx_len, HEAD_DIM), lambda b, h, r: (ctx0 + b, h))

    def kind(r):
        return jnp.where(r == 0, 0, jnp.where(r == blocks - 1, 2, 1))

    return pl.pallas_call(
        functools.partial(_attn_kernel, n_grid_rows=n_grid_rows, scale=HEAD_DIM ** -0.5),
        name="attn",
        out_shape=jax.ShapeDtypeStruct((n, width), BF16),
        grid=(n_batch, heads, blocks),
        in_specs=[qspec, qspec, kspec, kspec, cspec, cspec,
                  pl.BlockSpec((None, None, tq, tk), lambda b, h, r: (h, kind(r), 0, 0))],
        out_specs=qspec,
        compiler_params=_params("parallel", "parallel", "arbitrary"),
    )(q, q_rot, k_rot, v, k, v, bias)


def _ctx_attn_kernel(q_ref, k_ref, v_ref, prev_ref, o_ref, *, scale):
    del prev_ref
    s = lax.dot_general(q_ref[...], k_ref[...], NT_DIMS, preferred_element_type=F32) * scale
    m = jnp.max(s, axis=-1, keepdims=True)
    p = jnp.exp(s - m)
    denom = jnp.sum(p, axis=-1, keepdims=True)
    o = jnp.dot(p.astype(BF16), v_ref[...], preferred_element_type=F32)
    o_ref[...] = (o / denom).astype(o_ref.dtype)


def _ctx_attention(q, k, v, o_att, n_batch, seq, ctx_len):
    n, width = q.shape
    heads = width // HEAD_DIM
    ctx0 = n_batch * seq // ctx_len
    spec = pl.BlockSpec((ctx_len, HEAD_DIM), lambda b, h: (ctx0 + b, h))
    return pl.pallas_call(
        functools.partial(_ctx_attn_kernel, scale=HEAD_DIM ** -0.5),
        name="ctx_attn",
        out_shape=jax.ShapeDtypeStruct((n, width), BF16),
        grid=(n_batch, heads),
        in_specs=[spec, spec, spec, pl.BlockSpec(memory_space=pl.ANY)],
        out_specs=spec,
        input_output_aliases={3: 0},
        compiler_params=_params("parallel", "parallel"),
    )(q, k, v, o_att)


def _gmlp_kernel(z_ref, g_ref, b_ref, ws_ref, bs_ref, o_ref, *, width, chunks):
    z = jax.nn.gelu(z_ref[...])
    u = z[:, :width]
    v = (_ln(z[:, width:]) * g_ref[...] + b_ref[...]).astype(BF16)
    for c in range(chunks):
        rows = slice(c * CHUNK, (c + 1) * CHUNK)
        for g in range(width // GMLP_GROUP_DIM):
            cols = slice(g * GMLP_GROUP_DIM, (g + 1) * GMLP_GROUP_DIM)
            mixed = jnp.dot(ws_ref[g], v[rows, cols], preferred_element_type=F32) + bs_ref[:, cols]
            o_ref[rows, cols] = (u[rows, cols] * mixed).astype(o_ref.dtype)


def _gmlp(z, ln_g, ln_b, w_s, b_s, n_rows):
    n, two_w = z.shape
    width = two_w // 2
    groups = width // GMLP_GROUP_DIM
    chunks = TM // CHUNK
    bs = jnp.repeat(b_s.T, GMLP_GROUP_DIM, axis=1)
    vec = pl.BlockSpec((1, width), lambda i: (0, 0))
    return pl.pallas_call(
        functools.partial(_gmlp_kernel, width=width, chunks=chunks),
        name="gmlp",
        out_shape=jax.ShapeDtypeStruct((n, width), BF16),
        grid=(n_rows // TM,),
        in_specs=[pl.BlockSpec((TM, two_w), lambda i: (i, 0)), vec, vec,
                  pl.BlockSpec((groups, CHUNK, CHUNK), lambda i: (0, 0, 0)),
                  pl.BlockSpec((CHUNK, width), lambda i: (0, 0))],
        out_specs=pl.BlockSpec((TM, width), lambda i: (i, 0)),
        compiler_params=_params("parallel"),
    )(z, ln_g.reshape(1, width), ln_b.reshape(1, width), w_s.astype(BF16), bs)


def _s5_gen_kernel(lr_ref, li_ref, ldt_ref, bre_ref, bim_ref, cre_ref, cim_ref, m_ref, wst_ref, et_ref, a_ref):
    hi = lax.Precision.HIGHEST
    nb = SSM_BLOCK
    lane = lax.broadcasted_iota(jnp.int32, (SSM_GROUP, SSM_ROW), 1)
    m_rows = [None] * nb
    for d in range(2):
        lr = lr_ref[d:d + 1, :]
        li = li_ref[d:d + 1, :]
        dt = jnp.exp(ldt_ref[d:d + 1, :])
        mag = jnp.exp(lr * dt)
        ar = mag * jnp.cos(li * dt)
        ai = mag * jnp.sin(li * dt)
        den = lr * lr + li * li
        fr = ((ar - 1.0) * lr + ai * li) / den
        fi = (ai * lr - (ar - 1.0) * li) / den
        br = bre_ref[d]
        bi = bim_ref[d]
        bbr = fr * br - fi * bi
        bbi = fr * bi + fi * br
        cr = cre_ref[d]
        ci = cim_ref[d]
        pr = [jnp.ones_like(ar)]
        pi = [jnp.zeros_like(ai)]
        for _ in range(nb):
            pr.append(pr[-1] * ar - pi[-1] * ai)
            pi.append(pr[-2] * ai + pi[-1] * ar)
        a_ref[2 * d] = pr[nb]
        a_ref[2 * d + 1] = pi[nb]
        e_state = [(t + 1) if d == 0 else (nb - t) for t in range(nb)]
        e_in = [(nb - 1 - s) if d == 0 else s for s in range(nb)]
        e_lag = [e if d == 0 else (nb - 1 - e) for e in range(nb)]
        for s in range(nb):
            rows = slice(s * SSM_GROUP, (s + 1) * SSM_GROUP)
            wr, wi = pr[e_in[s]], pi[e_in[s]]
            wst_ref[2 * d, rows, :] = (wr * bbr - wi * bbi).astype(wst_ref.dtype)
            wst_ref[2 * d + 1, rows, :] = (wr * bbi + wi * bbr).astype(wst_ref.dtype)
            er, ei = pr[e_state[s]], pi[e_state[s]]
            et_ref[2 * d, rows, :] = (cr * er - ci * ei).astype(et_ref.dtype)
            et_ref[2 * d + 1, rows, :] = (-(cr * ei + ci * er)).astype(et_ref.dtype)
        lag_r = jnp.concatenate([cr * pr[e] - ci * pi[e] for e in e_lag], axis=0)
        lag_i = jnp.concatenate([-(cr * pi[e] + ci * pr[e]) for e in e_lag], axis=0)
        kbase = (lax.dot_general(bbr, lag_r, NT_DIMS, precision=hi, preferred_element_type=F32)
                 + lax.dot_general(bbi, lag_i, NT_DIMS, precision=hi, preferred_element_type=F32))
        for s in range(nb):
            if d == 0:
                sh = s * SSM_GROUP
                row = kbase if sh == 0 else jnp.where(lane >= sh, pltpu.roll(kbase, sh, 1), 0.0)
                m_rows[s] = row
            else:
                sh = (nb - 1 - s) * SSM_GROUP
                row = kbase if sh == 0 else jnp.where(lane < SSM_ROW - sh, pltpu.roll(kbase, SSM_ROW - sh, 1), 0.0)
                m_rows[s] = m_rows[s] + row
    for s in range(nb):
        m_ref[s * SSM_GROUP:(s + 1) * SSM_GROUP, :] = m_rows[s].astype(m_ref.dtype)


def _s5_gen(lam_re, lam_im, log_dt, b_re, b_im, c_re, c_im):
    _, groups, p = lam_re.shape

    def gp(x):
        return jnp.swapaxes(x, 0, 1)

    ldt = jnp.broadcast_to(log_dt[:, :, None], lam_re.shape)
    vec = pl.BlockSpec((None, 2, p), lambda g: (g, 0, 0))
    mat = pl.BlockSpec((None, 2, SSM_GROUP, p), lambda g: (g, 0, 0, 0))
    return pl.pallas_call(
        _s5_gen_kernel,
        name="s5_gen",
        out_shape=[jax.ShapeDtypeStruct((groups, SSM_ROW, SSM_ROW), BF16),
                   jax.ShapeDtypeStruct((groups, 4, SSM_ROW, p), BF16),
                   jax.ShapeDtypeStruct((groups, 4, SSM_ROW, p), BF16),
                   jax.ShapeDtypeStruct((groups, 4, 1, p), F32)],
        grid=(groups,),
        in_specs=[vec, vec, vec, mat, mat, mat, mat],
        out_specs=[pl.BlockSpec((None, SSM_ROW, SSM_ROW), lambda g: (g, 0, 0)),
                   pl.BlockSpec((None, 4, SSM_ROW, p), lambda g: (g, 0, 0, 0)),
                   pl.BlockSpec((None, 4, SSM_ROW, p), lambda g: (g, 0, 0, 0)),
                   pl.BlockSpec((None, 4, 1, p), lambda g: (g, 0, 0, 0))],
        compiler_params=_params("parallel"),
    )(gp(lam_re), gp(lam_im), gp(ldt), gp(jnp.swapaxes(b_re, 2, 3)), gp(jnp.swapaxes(b_im, 2, 3)),
      gp(c_re), gp(c_im))


def _pair_diag(w):
    g, k, r, p = w.shape
    w = w.reshape(g // 2, 2, k, r, p)
    eye = jnp.eye(2, dtype=w.dtype)
    return jnp.einsum("gjkrp,jJ->gkjrJp", w, eye).reshape(g // 2, k, 2 * r, 2 * p)


def _s5_state_kernel(u_ref, wst_ref, sfr_ref, sfi_ref, sbr_ref, sbi_ref):
    u = u_ref[...]
    for k, o_ref in enumerate((sfr_ref, sfi_ref, sbr_ref, sbi_ref)):
        o_ref[...] = jnp.dot(u, wst_ref[k], preferred_element_type=F32)


def _s5_states(u, wst_pair):
    rows = u.shape[0]
    pairs, _, kdim, lanes = wst_pair.shape
    out = jax.ShapeDtypeStruct((rows, pairs * lanes), F32)
    ospec = pl.BlockSpec((rows, lanes), lambda g: (0, g))
    return pl.pallas_call(
        _s5_state_kernel,
        name="s5_states",
        out_shape=[out] * 4,
        grid=(pairs,),
        in_specs=[pl.BlockSpec((rows, kdim), lambda g: (0, g)),
                  pl.BlockSpec((None, 4, kdim, lanes), lambda g: (g, 0, 0, 0))],
        out_specs=[ospec] * 4,
        compiler_params=_params("parallel"),
    )(u, wst_pair)


def _s5_scan_kernel(sfr_ref, sfi_ref, sbr_ref, sbi_ref, a_ref, hfr_ref, hfi_ref, hbr_ref, hbi_ref,
                    *, n_batch, n_lat_blocks, n_ctx_blocks):
    afr, afi, abr, abi = (a_ref[k:k + 1, :] for k in range(4))
    zero = jnp.zeros_like(afr)
    nl, nc = n_lat_blocks, n_ctx_blocks

    def step(t, carry):
        new = []
        in_ctx = t < nc
        for b in range(n_batch):
            lat0, ctx0 = b * nl, n_batch * nl + b * nc
            fr, fi, br, bi = carry[4 * b:4 * b + 4]
            rf = pl.ds(jnp.where(in_ctx, ctx0 + t, lat0 + t - nc), 1)
            rb = pl.ds(jnp.where(in_ctx, ctx0 + nc - 1 - t, lat0 + nl - 1 - (t - nc)), 1)
            hfr_ref[rf, :] = fr
            hfi_ref[rf, :] = fi
            hbr_ref[rb, :] = br
            hbi_ref[rb, :] = bi
            new += [afr * fr - afi * fi + sfr_ref[rf, :], afr * fi + afi * fr + sfi_ref[rf, :],
                    abr * br - abi * bi + sbr_ref[rb, :], abr * bi + abi * br + sbi_ref[rb, :]]
        return tuple(new)

    lax.fori_loop(0, nl + nc, step, (zero,) * (4 * n_batch))


def _s5_scan(s_parts, a_rows, n_batch, n_lat_blocks, n_ctx_blocks):
    rows, lanes = s_parts[0].shape
    tl = min(lanes, 2 * LANES)
    spec = pl.BlockSpec((rows, tl), lambda j: (0, j))
    return pl.pallas_call(
        functools.partial(_s5_scan_kernel, n_batch=n_batch, n_lat_blocks=n_lat_blocks, n_ctx_blocks=n_ctx_blocks),
        name="s5_scan",
        out_shape=[jax.ShapeDtypeStruct((rows, lanes), F32)] * 4,
        grid=(lanes // tl,),
        in_specs=[spec] * 4 + [pl.BlockSpec((4, tl), lambda j: (0, j))],
        out_specs=[spec] * 4,
        compiler_params=_params("parallel"),
    )(*s_parts, a_rows)


def _s5_out_kernel(u_ref, m_ref, et_ref, hfr_ref, hfi_ref, hbr_ref, hbi_ref, y_ref):
    y = jnp.dot(u_ref[...], m_ref[...], preferred_element_type=F32)
    for k, h_ref in enumerate((hfr_ref, hfi_ref, hbr_ref, hbi_ref)):
        y += lax.dot_general(h_ref[...].astype(BF16), et_ref[k], NT_DIMS, preferred_element_type=F32)
    y_ref[...] = y


def _s5_outputs(u, m_pair, et_pair, h_parts):
    rows = u.shape[0]
    pairs, _, kdim, lanes = et_pair.shape
    hspec = pl.BlockSpec((rows, lanes), lambda g: (0, g))
    return pl.pallas_call(
        _s5_out_kernel,
        name="s5_out",
        out_shape=jax.ShapeDtypeStruct(u.shape, F32),
        grid=(pairs,),
        in_specs=[pl.BlockSpec((rows, kdim), lambda g: (0, g)),
                  pl.BlockSpec((None, kdim, kdim), lambda g: (g, 0, 0)),
                  pl.BlockSpec((None, 4, kdim, lanes), lambda g: (g, 0, 0, 0)),
                  hspec, hspec, hspec, hspec],
        out_specs=pl.BlockSpec((rows, kdim), lambda g: (0, g)),
        compiler_params=_params("parallel"),
    )(u, m_pair, et_pair, *h_parts)


def _s5_glu_kernel(ys_ref, us_ref, d_ref, w_ref, b_ref, o_ref, wb_ref):
    @pl.when(pl.program_id(0) == 0)
    def _():
        wb_ref[...] = w_ref[...].astype(BF16)

    y = jax.nn.gelu(ys_ref[...] + d_ref[...] * us_ref[...])
    gate = jax.nn.sigmoid(jnp.dot(y.astype(BF16), wb_ref[...], preferred_element_type=F32) + b_ref[...])
    o_ref[...] = (y * gate).astype(o_ref.dtype)


def _s5_glu(y_s, us, d_skip, w_glu, b_glu, n_rows):
    n, w = us.shape
    row = pl.BlockSpec((TM, w), lambda i: (i, 0))
    vec = pl.BlockSpec((1, w), lambda i: (0, 0))
    return pl.pallas_call(
        _s5_glu_kernel,
        name="s5_glu",
        out_shape=jax.ShapeDtypeStruct((n, w), BF16),
        grid=(n_rows // TM,),
        in_specs=[row, row, vec, pl.BlockSpec((w, w), lambda i: (0, 0)), vec],
        out_specs=row,
        scratch_shapes=[pltpu.VMEM((w, w), BF16)],
        compiler_params=_params("arbitrary"),
    )(y_s, us, d_skip.reshape(1, w), w_glu, b_glu.reshape(1, w))


def _s5(us, n_batch, seq, ctx_len, lam_re, lam_im, log_dt, b_re, b_im, c_re, c_im):
    n_all, width = us.shape
    groups = width // SSM_GROUP
    n_rows = n_all // SSM_BLOCK
    m, wst, et, a16 = _s5_gen(lam_re, lam_im, log_dt, b_re, b_im, c_re, c_im)
    m_pair = _pair_diag(m[:, None])[:, 0]
    wst_pair = _pair_diag(wst)
    et_pair = _pair_diag(et)
    a_rows = jnp.transpose(a16[:, :, 0, :], (1, 0, 2)).reshape(4, groups * SSM_STATE)
    u = us.reshape(n_rows, SSM_BLOCK, groups, SSM_GROUP)
    u = jnp.transpose(u, (0, 2, 1, 3)).reshape(n_rows, groups * SSM_ROW).astype(BF16)
    s_parts = _s5_states(u, wst_pair)
    h_parts = _s5_scan(s_parts, a_rows, n_batch, seq // SSM_BLOCK, ctx_len // SSM_BLOCK)
    y = _s5_outputs(u, m_pair, et_pair, h_parts)
    y = y.reshape(n_rows, groups, SSM_BLOCK, SSM_GROUP)
    return jnp.transpose(y, (0, 2, 1, 3)).reshape(n_all, width)


def _rope_tables(seq):
    t = jnp.arange(seq)
    quarter = HEAD_DIM // 4
    inv = ROPE_BASE ** (-jnp.arange(quarter, dtype=F32) / quarter)
    ang_r = (t // GRID_W).astype(F32)[:, None] * inv[None, :]
    ang_c = (t % GRID_W).astype(F32)[:, None] * inv[None, :]
    cos = jnp.concatenate([jnp.cos(ang_r)] * 2 + [jnp.cos(ang_c)] * 2, axis=1)
    sin = jnp.concatenate([-jnp.sin(ang_r), jnp.sin(ang_r), -jnp.sin(ang_c), jnp.sin(ang_c)], axis=1)
    return cos, sin


def kernel(x, c, ctx, c_ctx, w_mod, b_mod, ln_g, ln_b, ffn1_w_in, ffn1_w_out, ffn2_w_in, ffn2_w_out, w_in, rpb, gm_ln_g, gm_ln_b, gm_ws, gm_bs, ssm_lam_re, ssm_lam_im, ssm_log_dt, ssm_b_re, ssm_b_im, ssm_c_re, ssm_c_im, ssm_d, ssm_w_glu, ssm_b_glu, w_pa, w_pb, w_pc, w_o):
    n_batch, seq, d = x.shape
    ctx_len = ctx.shape[1]
    depth = w_mod.shape[0]
    att_w = w_pb.shape[1]
    gm_w = w_pa.shape[1]
    ssm_w = w_pc.shape[1]
    q_off, k_off, v_off = 0, att_w, 2 * att_w
    gm_off = 3 * att_w
    ssm_off = gm_off + 2 * gm_w
    gate_off = ssm_off + ssm_w
    alpha = (2.0 * depth) ** 0.25
    n_lat = n_batch * seq
    n_all = n_lat + n_batch * ctx_len
    assert seq % TM == 0 and (n_batch * ctx_len) % TM == 0 and seq % (ATT_ROWS * GRID_W) == 0
    assert seq // GRID_W >= ATT_KROWS and ctx_len % CHUNK == 0 and n_lat % ctx_len == 0
    assert n_batch + 1 <= MOD_ROWS and rpb.shape[2:] == (2 * WIN_H - 1, 2 * WIN_W - 1)

    seg = _seg_map(seq // TM_ROW, n_batch)
    tiles_per_batch = seq // TM

    x_lat, x_ctx = x.reshape(n_lat, d), ctx.reshape(n_batch * ctx_len, d)
    c_all = jnp.zeros((MOD_ROWS, d), F32).at[:n_batch].set(c).at[n_batch].set(c_ctx)
    mods = _mods(c_all, w_mod, b_mod)[:, :n_batch + 1].reshape(depth, n_batch + 1, N_MOD, d)
    cos, sin = _rope_tables(seq)

    xs = (x_lat, x_ctx)
    h = _modulate(x_lat, x_ctx, mods[0], 0, 1, seg)
    for l in range(depth):
        last = l == depth - 1
        n_mix = n_lat if last else n_all

        y = _mm(_ffn_in(h, ffn1_w_in[l], n_all), ffn1_w_out[l], 0, d, BF16, n_all, "ffn_out")
        xs, h = _norm(xs, y, mods[l], 2, 0.5, ln_g[l, 0], ln_b[l, 0], alpha, seg, n_all,
                      mods_next=mods[l], i_shift=3, i_scale=4)

        q, q_rot = _mm_rope(h, w_in[l], q_off, att_w, cos, sin, tiles_per_batch, n_mix)
        k, k_rot = _mm_rope(h, w_in[l], k_off, att_w, cos, sin, tiles_per_batch, n_all)
        v = _mm(h, w_in[l], v_off, att_w, BF16, n_all, "proj_v")
        z = _mm(h, w_in[l], gm_off, 2 * gm_w, F32, n_mix, "proj_z")
        us = _mm(h, w_in[l], ssm_off, ssm_w, F32, n_all, "proj_us")
        gates = _mm(h, w_in[l], gate_off, N_BRANCH * d, BF16, n_mix, "proj_gates", act="sigmoid")

        o_att = _attention(q, q_rot, k, k_rot, v, _rpb_table(rpb[l]), n_batch, seq, ctx_len)
        if not last:
            o_att = _ctx_attention(q, k, v, o_att, n_batch, seq, ctx_len)
        o_gm = _gmlp(z, gm_ln_g[l], gm_ln_b[l], gm_ws[l], gm_bs[l], n_mix)
        y_s = _s5(us, n_batch, seq, ctx_len, ssm_lam_re[l], ssm_lam_im[l], ssm_log_dt[l],
                  ssm_b_re[l], ssm_b_im[l], ssm_c_re[l], ssm_c_im[l])
        o_ssm = _s5_glu(y_s, us, ssm_d[l], ssm_w_glu[l], ssm_b_glu[l], n_mix)
        merged = _merge(o_gm, o_att, o_ssm, w_pa[l], w_pb[l], w_pc[l], gates, n_mix)
        y = _mm(merged, w_o[l], 0, d, BF16, n_mix, "out_proj")
        xs, h = _norm(xs, y, mods[l], 5, 1.0, ln_g[l, 1], ln_b[l, 1], alpha, seg, n_mix,
                      mods_next=mods[l], i_shift=6, i_scale=7)

        y = _mm(_ffn_in(h, ffn2_w_in[l], n_mix), ffn2_w_out[l], 0, d, BF16, n_mix, "ffn_out")
        if last:
            xs, _ = _norm(xs, y, mods[l], 8, 0.5, ln_g[l, 2], ln_b[l, 2], alpha, seg, n_mix, n_out=n_mix)
        else:
            xs, h = _norm(xs, y, mods[l], 8, 0.5, ln_g[l, 2], ln_b[l, 2], alpha, seg, n_mix,
                          mods_next=mods[l + 1], i_shift=0, i_scale=1)
    return xs.reshape(n_batch, seq, d)
```

```python
import functools

import jax
import jax.numpy as jnp
from jax import lax
from jax.experimental import pallas as pl
from jax.experimental.pallas import tpu as pltpu

F32 = jnp.float32
BF16 = jnp.bfloat16

GRID_W = 64
N_SUB = 3
N_MOD = 3 * N_SUB
HEAD_DIM = 128
WIN_H = 8
WIN_W = 16
ROPE_BASE = 10000.0
CHUNK = 128
GMLP_GROUP_DIM = 128
SSM_GROUP = 16
SSM_STATE = 64
N_BRANCH = 3
LN_EPS = 1e-6
NEG_INF = -1e30

LANES = 128
SSM_BLOCK = 16
SSM_ROW = SSM_BLOCK * SSM_GROUP
SSM_CHUNK_GROUPS = LANES // SSM_GROUP
MOD_ROWS = 8
VMEM_LIMIT_BYTES = 56 * 1024 * 1024

TM = 512
TM_ROW = 256
TN = 512
ATT_ROWS = 8
ATT_KROWS = ATT_ROWS + WIN_H
ATT_SUB = 4
ATT_SUB_KROWS = ATT_SUB + WIN_H
NT_DIMS = (((1,), (1,)), ((), ()))


def _params(*sem):
    return pltpu.CompilerParams(dimension_semantics=sem, vmem_limit_bytes=VMEM_LIMIT_BYTES)


def _ln(x):
    mu = jnp.mean(x, axis=-1, keepdims=True)
    xc = x - mu
    var = jnp.mean(xc * xc, axis=-1, keepdims=True)
    return xc * lax.rsqrt(var + LN_EPS)


def _sigmoid(x):
    return 0.5 * jnp.tanh(0.5 * x) + 0.5


def _col_tile(n_cols, col_off=0):
    for tn in range(TN, 0, -LANES):
        if n_cols % tn == 0 and col_off % tn == 0:
            return tn
    raise ValueError(f"column group ({col_off}, {n_cols}) is not lane aligned")


def _seg_map(tiles_per_batch, n_batch):
    return lambda i: jnp.minimum(i // tiles_per_batch, n_batch)


def _cast_weight_once(w_ref, wb_ref):
    @pl.when(pl.program_id(1) == 0)
    def _():
        wb_ref[...] = w_ref[...].astype(BF16)


def _mods_kernel(c_ref, w_ref, b_ref, o_ref):
    s = jax.nn.silu(c_ref[...]).astype(BF16)
    o_ref[...] = jnp.dot(s, w_ref[...].astype(BF16), preferred_element_type=F32) + b_ref[...]


def _mods(c_all, w_mod, b_mod):
    depth, d, nm = w_mod.shape
    tn = _col_tile(nm)
    return pl.pallas_call(
        _mods_kernel,
        name="mods",
        out_shape=jax.ShapeDtypeStruct((depth, MOD_ROWS, nm), F32),
        grid=(depth, nm // tn),
        in_specs=[pl.BlockSpec((MOD_ROWS, d), lambda l, j: (0, 0)),
                  pl.BlockSpec((None, d, tn), lambda l, j: (l, 0, j)),
                  pl.BlockSpec((None, 1, tn), lambda l, j: (l, 0, j))],
        out_specs=pl.BlockSpec((None, MOD_ROWS, tn), lambda l, j: (l, 0, j)),
        compiler_params=_params("parallel", "parallel"),
    )(c_all, w_mod, b_mod.reshape(depth, 1, nm))


def _split_specs(n_lat_tiles, d):
    return [pl.BlockSpec((TM_ROW, d), lambda i: (jnp.minimum(i, n_lat_tiles - 1), 0)),
            pl.BlockSpec((TM_ROW, d), lambda i: (jnp.maximum(i - n_lat_tiles, 0), 0))]


def _modulate_kernel(xl_ref, xc_ref, mod_ref, h_ref, *, n_lat_tiles, i_shift, i_scale):
    x = jnp.where(pl.program_id(0) < n_lat_tiles, xl_ref[...], xc_ref[...])
    h = _ln(x) * (1.0 + mod_ref[i_scale:i_scale + 1, :]) + mod_ref[i_shift:i_shift + 1, :]
    h_ref[...] = h.astype(BF16)


def _modulate(x_lat, x_ctx, mods, i_shift, i_scale, seg):
    n_lat, d = x_lat.shape
    n = n_lat + x_ctx.shape[0]
    n_lat_tiles = n_lat // TM_ROW
    return pl.pallas_call(
        functools.partial(_modulate_kernel, n_lat_tiles=n_lat_tiles, i_shift=i_shift, i_scale=i_scale),
        name="modulate",
        out_shape=jax.ShapeDtypeStruct((n, d), BF16),
        grid=(n // TM_ROW,),
        in_specs=_split_specs(n_lat_tiles, d) + [pl.BlockSpec((None, N_MOD, d), lambda i: (seg(i), 0, 0))],
        out_specs=pl.BlockSpec((TM_ROW, d), lambda i: (i, 0)),
        compiler_params=_params("parallel"),
    )(x_lat, x_ctx, mods)


def _norm_kernel(*refs, n_lat_tiles, alpha, coef, i_gate, i_shift, i_scale):
    if n_lat_tiles is None:
        x_ref, y_ref, modg_ref, g_ref, b_ref, *rest = refs
        x = x_ref[...]
    else:
        xl_ref, xc_ref, y_ref, modg_ref, g_ref, b_ref, *rest = refs
        x = jnp.where(pl.program_id(0) < n_lat_tiles, xl_ref[...], xc_ref[...])
    t = alpha * x + (coef * modg_ref[i_gate:i_gate + 1, :]) * y_ref[...].astype(F32)
    xn = _ln(t) * g_ref[...] + b_ref[...]
    if i_shift is None:
        (xo_ref,) = rest
        xo_ref[...] = xn
    else:
        modn_ref, xo_ref, h_ref = rest
        xo_ref[...] = xn
        h = _ln(xn) * (1.0 + modn_ref[i_scale:i_scale + 1, :]) + modn_ref[i_shift:i_shift + 1, :]
        h_ref[...] = h.astype(BF16)


def _norm(x, y, mods_gate, i_gate, coef, g, b, alpha, seg, n_rows, mods_next=None, i_shift=None, i_scale=None,
          n_out=None):
    split = isinstance(x, tuple)
    d = y.shape[1]
    n_out = y.shape[0] if n_out is None else n_out
    row = pl.BlockSpec((TM_ROW, d), lambda i: (i, 0))
    vec = pl.BlockSpec((1, d), lambda i: (0, 0))
    mod = pl.BlockSpec((None, N_MOD, d), lambda i: (seg(i), 0, 0))
    n_lat_tiles = x[0].shape[0] // TM_ROW if split else None
    in_specs = (_split_specs(n_lat_tiles, d) if split else [row]) + [row, mod, vec, vec]
    args = (list(x) if split else [x]) + [y, mods_gate, g.reshape(1, d), b.reshape(1, d)]
    out_shape = [jax.ShapeDtypeStruct((n_out, d), F32)]
    out_specs = [row]
    if i_shift is not None:
        in_specs.append(mod)
        args.append(mods_next)
        out_shape.append(jax.ShapeDtypeStruct((n_out, d), BF16))
        out_specs.append(row)
    out = pl.pallas_call(
        functools.partial(_norm_kernel, n_lat_tiles=n_lat_tiles, alpha=alpha, coef=coef, i_gate=i_gate,
                          i_shift=i_shift, i_scale=i_scale),
        name="norm",
        out_shape=out_shape,
        grid=(n_rows // TM_ROW,),
        in_specs=in_specs,
        out_specs=out_specs,
        compiler_params=_params("parallel"),
    )(*args)
    return out if i_shift is not None else (out[0], None)


def _ffn_in_kernel(h_ref, wg_ref, wu_ref, o_ref, wgb_ref, wub_ref):
    _cast_weight_once(wg_ref, wgb_ref)
    _cast_weight_once(wu_ref, wub_ref)
    h = h_ref[...]
    g = jnp.dot(h, wgb_ref[...], preferred_element_type=F32)
    u = jnp.dot(h, wub_ref[...], preferred_element_type=F32)
    o_ref[...] = (g * _sigmoid(g) * u).astype(o_ref.dtype)


def _ffn_in(h, w, layer, n_rows):
    n, d = h.shape
    dff = w.shape[2] // 2
    tn = _col_tile(dff)
    return pl.pallas_call(
        _ffn_in_kernel,
        name="ffn_in",
        out_shape=jax.ShapeDtypeStruct((n, dff), BF16),
        grid=(dff // tn, n_rows // TM),
        in_specs=[pl.BlockSpec((TM, d), lambda j, i: (i, 0)),
                  pl.BlockSpec((None, d, tn), lambda j, i: (layer, 0, j)),
                  pl.BlockSpec((None, d, tn), lambda j, i: (layer, 0, j + dff // tn))],
        out_specs=pl.BlockSpec((TM, tn), lambda j, i: (i, j)),
        scratch_shapes=[pltpu.VMEM((d, tn), BF16), pltpu.VMEM((d, tn), BF16)],
        compiler_params=_params("arbitrary", "arbitrary"),
    )(h, w, w)


def _mm_kernel(a_ref, w_ref, o_ref, wb_ref, *, act):
    _cast_weight_once(w_ref, wb_ref)
    acc = jnp.dot(a_ref[...], wb_ref[...], preferred_element_type=F32)
    if act == "sigmoid":
        acc = _sigmoid(acc)
    o_ref[...] = acc.astype(o_ref.dtype)


def _mm(a, w, layer, col_off, n_cols, out_dtype, n_rows, name, act=None):
    n, k = a.shape
    tn = _col_tile(n_cols, col_off)
    joff = col_off // tn
    return pl.pallas_call(
        functools.partial(_mm_kernel, act=act),
        name=name,
        out_shape=jax.ShapeDtypeStruct((n, n_cols), out_dtype),
        grid=(n_cols // tn, n_rows // TM),
        in_specs=[pl.BlockSpec((TM, k), lambda j, i: (i, 0)),
                  pl.BlockSpec((None, k, tn), lambda j, i: (layer, 0, j + joff))],
        out_specs=pl.BlockSpec((TM, tn), lambda j, i: (i, j)),
        scratch_shapes=[pltpu.VMEM((k, tn), BF16)],
        compiler_params=_params("arbitrary", "arbitrary"),
    )(a, w)


def _mm_rope_kernel(a_ref, w_ref, cos_ref, sin_ref, o_ref, orot_ref, wb_ref):
    _cast_weight_once(w_ref, wb_ref)
    acc = jnp.dot(a_ref[...], wb_ref[...], preferred_element_type=F32)
    o_ref[...] = acc.astype(o_ref.dtype)
    tn = acc.shape[1]
    reps = tn // HEAD_DIM
    cos = jnp.concatenate([cos_ref[...]] * reps, axis=1)
    sin = jnp.concatenate([sin_ref[...]] * reps, axis=1)
    lane = lax.broadcasted_iota(jnp.int32, acc.shape, 1)
    quarter = HEAD_DIM // 4
    first = (lane % (2 * quarter)) < quarter
    partner = jnp.where(first, pltpu.roll(acc, tn - quarter, 1), pltpu.roll(acc, quarter, 1))
    orot_ref[...] = (acc * cos + partner * sin).astype(orot_ref.dtype)


def _mm_rope(a, w, layer, col_off, n_cols, cos, sin, tiles_per_batch, n_rows):
    n, k = a.shape
    tn = _col_tile(n_cols, col_off)
    joff = col_off // tn
    out = jax.ShapeDtypeStruct((n, n_cols), BF16)
    tab = pl.BlockSpec((TM, HEAD_DIM), lambda j, i: (i % tiles_per_batch, 0))
    blk = pl.BlockSpec((TM, tn), lambda j, i: (i, j))
    return pl.pallas_call(
        _mm_rope_kernel,
        name="proj_rope",
        out_shape=[out, out],
        grid=(n_cols // tn, n_rows // TM),
        in_specs=[pl.BlockSpec((TM, k), lambda j, i: (i, 0)),
                  pl.BlockSpec((None, k, tn), lambda j, i: (layer, 0, j + joff)),
                  tab, tab],
        out_specs=[blk, blk],
        scratch_shapes=[pltpu.VMEM((k, tn), BF16)],
        compiler_params=_params("arbitrary", "arbitrary"),
    )(a, w, cos, sin)


def _merge_kernel(oa_ref, ob_ref, oc_ref, wa_ref, wb_ref, wc_ref, ga_ref, gb_ref, gc_ref, o_ref,
                  wab_ref, wbb_ref, wcb_ref):
    _cast_weight_once(wa_ref, wab_ref)
    _cast_weight_once(wb_ref, wbb_ref)
    _cast_weight_once(wc_ref, wcb_ref)
    t = ga_ref[...].astype(F32) * jnp.dot(oa_ref[...], wab_ref[...], preferred_element_type=F32)
    t += gb_ref[...].astype(F32) * jnp.dot(ob_ref[...], wbb_ref[...], preferred_element_type=F32)
    t += gc_ref[...].astype(F32) * jnp.dot(oc_ref[...], wcb_ref[...], preferred_element_type=F32)
    o_ref[...] = t.astype(o_ref.dtype)


def _merge(o_a, o_b, o_c, w_pa, w_pb, w_pc, layer, gates, n_rows):
    n = o_a.shape[0]
    d = w_pa.shape[2]
    tn = _col_tile(d)
    nj = d // tn

    def act(o):
        return pl.BlockSpec((TM, o.shape[1]), lambda j, i: (i, 0))

    def wgt(w):
        return pl.BlockSpec((None, w.shape[1], tn), lambda j, i: (layer, 0, j))

    def gate(k):
        return pl.BlockSpec((TM, tn), lambda j, i: (i, j + k * nj))

    return pl.pallas_call(
        _merge_kernel,
        name="merge",
        out_shape=jax.ShapeDtypeStruct((n, d), BF16),
        grid=(nj, n_rows // TM),
        in_specs=[act(o_a), act(o_b), act(o_c), wgt(w_pa), wgt(w_pb), wgt(w_pc), gate(0), gate(1), gate(2)],
        out_specs=pl.BlockSpec((TM, tn), lambda j, i: (i, j)),
        scratch_shapes=[pltpu.VMEM((w.shape[1], tn), BF16) for w in (w_pa, w_pb, w_pc)],
        compiler_params=_params("arbitrary", "arbitrary"),
    )(o_a, o_b, o_c, w_pa, w_pb, w_pc, gates, gates, gates)


def _attn_block_geometry(kind):
    half = WIN_H // 2
    if kind == 0:
        return 0, [max(i - half, 0) for i in range(ATT_ROWS)]
    if kind == 1:
        return -half, list(range(ATT_ROWS))
    return -(ATT_KROWS - ATT_ROWS), [min(half + i, ATT_KROWS - WIN_H) for i in range(ATT_ROWS)]


def _attn_sub_key_offset(kind, sub):
    _, first_key_row = _attn_block_geometry(kind)
    return min(min(first_key_row[sub * ATT_SUB:(sub + 1) * ATT_SUB]), ATT_KROWS - ATT_SUB_KROWS)


def _rpb_table_kernel(rpb_ref, o_ref):
    shape = (GRID_W, 2 * GRID_W)
    cq = lax.broadcasted_iota(jnp.int32, shape, 0)
    lane = lax.broadcasted_iota(jnp.int32, shape, 1)
    ck = lane % GRID_W
    second = lane >= GRID_W
    start = jnp.clip(cq - WIN_W // 2, 0, GRID_W - WIN_W)
    in_win = (ck >= start) & (ck < start + WIN_W)
    neg = jnp.full(shape, NEG_INF, F32)
    n_bias_rows = 2 * WIN_H - 1

    def half_tile(a, lane_off):
        row = jnp.broadcast_to(rpb_ref[a:a + 1, :], shape)
        return pltpu.roll(row, (lane_off - (WIN_W - 1)) % LANES, 1, stride=1, stride_axis=0)

    lo_half = [half_tile(a, 0) for a in range(n_bias_rows)]
    hi_half = [half_tile(a, GRID_W) for a in range(n_bias_rows)]
    for kind in range(3):
        diff, first_key_row = _attn_block_geometry(kind)
        for i in range(ATT_ROWS):
            lo = first_key_row[i]
            sub, isub = divmod(i, ATT_SUB)
            koff = _attn_sub_key_offset(kind, sub)
            for pair in range(ATT_SUB_KROWS // 2):
                j0, j1 = koff + 2 * pair, koff + 2 * pair + 1
                t0 = lo_half[j0 - i + diff + WIN_H - 1] if lo <= j0 < lo + WIN_H else neg
                t1 = hi_half[j1 - i + diff + WIN_H - 1] if lo <= j1 < lo + WIN_H else neg
                tile = jnp.where(in_win, jnp.where(second, t1, t0), neg)
                o_ref[kind, sub, isub * GRID_W:(isub + 1) * GRID_W,
                      pair * 2 * GRID_W:(pair + 1) * 2 * GRID_W] = tile


def _rpb_table(rpb):
    heads, nr, nc = rpb.shape
    rpb_pad = jnp.zeros((heads, 2 * WIN_H, LANES), F32).at[:, :nr, :nc].set(rpb)
    shape = (3, ATT_ROWS // ATT_SUB, ATT_SUB * GRID_W, ATT_SUB_KROWS * GRID_W)
    return pl.pallas_call(
        _rpb_table_kernel,
        name="rpb_table",
        out_shape=jax.ShapeDtypeStruct((heads,) + shape, F32),
        grid=(heads,),
        in_specs=[pl.BlockSpec((None, 2 * WIN_H, LANES), lambda h: (h, 0, 0))],
        out_specs=pl.BlockSpec((None,) + shape, lambda h: (h, 0, 0, 0, 0)),
        compiler_params=_params("parallel"),
    )(rpb_pad)


def _attn_kernel(q_ref, qr_ref, kr_ref, v_ref, kc_ref, vc_ref, bias_ref, o_ref, *, n_grid_rows, n_blocks, scale):
    rb = pl.program_id(2)
    kr0 = jnp.clip(rb * ATT_ROWS - WIN_H // 2, 0, n_grid_rows - ATT_KROWS)
    kc = kc_ref[...]
    vc = vc_ref[...]
    for sub in range(ATT_ROWS // ATT_SUB):
        offs = [_attn_sub_key_offset(kind, sub) for kind in range(3)]
        koff = jnp.where(rb == 0, offs[0], jnp.where(rb == n_blocks - 1, offs[2], offs[1]))
        keys = pl.ds(pl.multiple_of((kr0 + koff) * GRID_W, GRID_W), ATT_SUB_KROWS * GRID_W)
        rows = pl.ds(sub * ATT_SUB * GRID_W, ATT_SUB * GRID_W)
        s = lax.dot_general(qr_ref[rows, :], kr_ref[keys, :], NT_DIMS, preferred_element_type=F32) * scale
        s += bias_ref[sub]
        sc = lax.dot_general(q_ref[rows, :], kc, NT_DIMS, preferred_element_type=F32) * scale
        m = jnp.maximum(jnp.max(s, axis=-1, keepdims=True), jnp.max(sc, axis=-1, keepdims=True))
        p = jnp.exp(s - m)
        pc = jnp.exp(sc - m)
        denom = jnp.sum(p, axis=-1, keepdims=True) + jnp.sum(pc, axis=-1, keepdims=True)
        o = jnp.dot(p.astype(BF16), v_ref[keys, :], preferred_element_type=F32)
        o += jnp.dot(pc.astype(BF16), vc, preferred_element_type=F32)
        o_ref[rows, :] = (o / denom).astype(o_ref.dtype)


def _attention(q, q_rot, k, k_rot, v, bias, n_batch, seq, ctx_len):
    n, width = q.shape
    heads = width // HEAD_DIM
    n_grid_rows = seq // GRID_W
    tq = ATT_ROWS * GRID_W
    blocks = seq // tq
    ctx0 = n_batch * seq // ctx_len
    qspec = pl.BlockSpec((tq, HEAD_DIM), lambda b, h, r: (b * blocks + r, h))
    kspec = pl.BlockSpec((seq, HEAD_DIM), lambda b, h, r: (b, h))
    cspec = pl.BlockSpec((ctx_len, HEAD_DIM), lambda b, h, r: (ctx0 + b, h))

    def kind(r):
        return jnp.where(r == 0, 0, jnp.where(r == blocks - 1, 2, 1))

    return pl.pallas_call(
        functools.partial(_attn_kernel, n_grid_rows=n_grid_rows, n_blocks=blocks, scale=HEAD_DIM ** -0.5),
        name="attn",
        out_shape=jax.ShapeDtypeStruct((n, width), BF16),
        grid=(n_batch, heads, blocks),
        in_specs=[qspec, qspec, kspec, kspec, cspec, cspec,
                  pl.BlockSpec((None, None) + bias.shape[2:], lambda b, h, r: (h, kind(r), 0, 0, 0))],
        out_specs=qspec,
        compiler_params=_params("parallel", "parallel", "arbitrary"),
    )(q, q_rot, k_rot, v, k, v, bias)


def _ctx_attn_kernel(q_ref, k_ref, v_ref, prev_ref, o_ref, *, scale):
    del prev_ref
    s = lax.dot_general(q_ref[...], k_ref[...], NT_DIMS, preferred_element_type=F32) * scale
    m = jnp.max(s, axis=-1, keepdims=True)
    p = jnp.exp(s - m)
    denom = jnp.sum(p, axis=-1, keepdims=True)
    o = jnp.dot(p.astype(BF16), v_ref[...], preferred_element_type=F32)
    o_ref[...] = (o / denom).astype(o_ref.dtype)


def _ctx_attention(q, k, v, o_att, n_batch, seq, ctx_len):
    n, width = q.shape
    heads = width // HEAD_DIM
    ctx0 = n_batch * seq // ctx_len
    spec = pl.BlockSpec((ctx_len, HEAD_DIM), lambda b, h: (ctx0 + b, h))
    return pl.pallas_call(
        functools.partial(_ctx_attn_kernel, scale=HEAD_DIM ** -0.5),
        name="ctx_attn",
        out_shape=jax.ShapeDtypeStruct((n, width), BF16),
        grid=(n_batch, heads),
        in_specs=[spec, spec, spec, pl.BlockSpec(memory_space=pl.ANY)],
        out_specs=spec,
        input_output_aliases={3: 0},
        compiler_params=_params("parallel", "parallel"),
    )(q, k, v, o_att)


def _gmlp_kernel(z_ref, g_ref, b_ref, ws_ref, bs_ref, o_ref, *, width, chunks):
    z = jax.nn.gelu(z_ref[...])
    u = z[:, :width]
    v = (_ln(z[:, width:]) * g_ref[...] + b_ref[...]).astype(BF16)
    for c in range(chunks):
        rows = slice(c * CHUNK, (c + 1) * CHUNK)
        for g in range(width // GMLP_GROUP_DIM):
            cols = slice(g * GMLP_GROUP_DIM, (g + 1) * GMLP_GROUP_DIM)
            mixed = jnp.dot(ws_ref[g], v[rows, cols], preferred_element_type=F32) + bs_ref[:, cols]
            o_ref[rows, cols] = (u[rows, cols] * mixed).astype(o_ref.dtype)


def _gmlp(z, ln_g, ln_b, w_s, b_s, n_rows):
    n, two_w = z.shape
    width = two_w // 2
    groups = width // GMLP_GROUP_DIM
    chunks = TM // CHUNK
    bs = jnp.repeat(b_s.T, GMLP_GROUP_DIM, axis=1)
    vec = pl.BlockSpec((1, width), lambda i: (0, 0))
    return pl.pallas_call(
        functools.partial(_gmlp_kernel, width=width, chunks=chunks),
        name="gmlp",
        out_shape=jax.ShapeDtypeStruct((n, width), BF16),
        grid=(n_rows // TM,),
        in_specs=[pl.BlockSpec((TM, two_w), lambda i: (i, 0)), vec, vec,
                  pl.BlockSpec((groups, CHUNK, CHUNK), lambda i: (0, 0, 0)),
                  pl.BlockSpec((CHUNK, width), lambda i: (0, 0))],
        out_specs=pl.BlockSpec((TM, width), lambda i: (i, 0)),
        compiler_params=_params("parallel"),
    )(z, ln_g.reshape(1, width), ln_b.reshape(1, width), w_s.astype(BF16), bs)


def _s5_gen_kernel(lr_ref, li_ref, ldt_ref, bre_ref, bim_ref, cre_ref, cim_ref, m_ref, wst_ref, et_ref, a_ref):
    hi = lax.Precision.HIGHEST
    nb = SSM_BLOCK
    lane = lax.broadcasted_iota(jnp.int32, (SSM_GROUP, SSM_ROW), 1)
    m_rows = [None] * nb
    for d in range(2):
        lr = lr_ref[d:d + 1, :]
        li = li_ref[d:d + 1, :]
        dt = jnp.exp(ldt_ref[d:d + 1, :])
        mag = jnp.exp(lr * dt)
        ar = mag * jnp.cos(li * dt)
        ai = mag * jnp.sin(li * dt)
        den = lr * lr + li * li
        fr = ((ar - 1.0) * lr + ai * li) / den
        fi = (ai * lr - (ar - 1.0) * li) / den
        br = bre_ref[d]
        bi = bim_ref[d]
        bbr = fr * br - fi * bi
        bbi = fr * bi + fi * br
        cr = cre_ref[d]
        ci = cim_ref[d]
        pr = [jnp.ones_like(ar)]
        pi = [jnp.zeros_like(ai)]
        for _ in range(nb):
            pr.append(pr[-1] * ar - pi[-1] * ai)
            pi.append(pr[-2] * ai + pi[-1] * ar)
        a_ref[2 * d] = pr[nb]
        a_ref[2 * d + 1] = pi[nb]
        e_state = [(t + 1) if d == 0 else (nb - t) for t in range(nb)]
        e_in = [(nb - 1 - s) if d == 0 else s for s in range(nb)]
        e_lag = [e if d == 0 else (nb - 1 - e) for e in range(nb)]
        for s in range(nb):
            rows = slice(s * SSM_GROUP, (s + 1) * SSM_GROUP)
            wr, wi = pr[e_in[s]], pi[e_in[s]]
            wst_ref[2 * d, rows, :] = (wr * bbr - wi * bbi).astype(wst_ref.dtype)
            wst_ref[2 * d + 1, rows, :] = (wr * bbi + wi * bbr).astype(wst_ref.dtype)
            er, ei = pr[e_state[s]], pi[e_state[s]]
            et_ref[2 * d, rows, :] = (cr * er - ci * ei).astype(et_ref.dtype)
            et_ref[2 * d + 1, rows, :] = (-(cr * ei + ci * er)).astype(et_ref.dtype)
        lag_r = jnp.concatenate([cr * pr[e] - ci * pi[e] for e in e_lag], axis=0)
        lag_i = jnp.concatenate([-(cr * pi[e] + ci * pr[e]) for e in e_lag], axis=0)
        kbase = (lax.dot_general(bbr, lag_r, NT_DIMS, precision=hi, preferred_element_type=F32)
                 + lax.dot_general(bbi, lag_i, NT_DIMS, precision=hi, preferred_element_type=F32))
        for s in range(nb):
            if d == 0:
                sh = s * SSM_GROUP
                row = kbase if sh == 0 else jnp.where(lane >= sh, pltpu.roll(kbase, sh, 1), 0.0)
                m_rows[s] = row
            else:
                sh = (nb - 1 - s) * SSM_GROUP
                row = kbase if sh == 0 else jnp.where(lane < SSM_ROW - sh, pltpu.roll(kbase, SSM_ROW - sh, 1), 0.0)
                m_rows[s] = m_rows[s] + row
    for s in range(nb):
        m_ref[s * SSM_GROUP:(s + 1) * SSM_GROUP, :] = m_rows[s].astype(m_ref.dtype)


def _s5_gen(lam_re, lam_im, log_dt, b_re, b_im, c_re, c_im):
    _, groups, p = lam_re.shape

    def gp(x):
        return jnp.swapaxes(x, 0, 1)

    ldt = jnp.broadcast_to(log_dt[:, :, None], lam_re.shape)
    vec = pl.BlockSpec((None, 2, p), lambda g: (g, 0, 0))
    mat = pl.BlockSpec((None, 2, SSM_GROUP, p), lambda g: (g, 0, 0, 0))
    return pl.pallas_call(
        _s5_gen_kernel,
        name="s5_gen",
        out_shape=[jax.ShapeDtypeStruct((groups, SSM_ROW, SSM_ROW), BF16),
                   jax.ShapeDtypeStruct((groups, 4, SSM_ROW, p), BF16),
                   jax.ShapeDtypeStruct((groups, 4, SSM_ROW, p), BF16),
                   jax.ShapeDtypeStruct((groups, 4, 1, p), F32)],
        grid=(groups,),
        in_specs=[vec, vec, vec, mat, mat, mat, mat],
        out_specs=[pl.BlockSpec((None, SSM_ROW, SSM_ROW), lambda g: (g, 0, 0)),
                   pl.BlockSpec((None, 4, SSM_ROW, p), lambda g: (g, 0, 0, 0)),
                   pl.BlockSpec((None, 4, SSM_ROW, p), lambda g: (g, 0, 0, 0)),
                   pl.BlockSpec((None, 4, 1, p), lambda g: (g, 0, 0, 0))],
        compiler_params=_params("parallel"),
    )(gp(lam_re), gp(lam_im), gp(ldt), gp(jnp.swapaxes(b_re, 2, 3)), gp(jnp.swapaxes(b_im, 2, 3)),
      gp(c_re), gp(c_im))


def _pair_diag(w):
    g, k, r, p = w.shape
    w = w.reshape(g // 2, 2, k, r, p)
    eye = jnp.eye(2, dtype=w.dtype)
    return jnp.einsum("gjkrp,jJ->gkjrJp", w, eye).reshape(g // 2, k, 2 * r, 2 * p)


def _slot_mask(rows):
    return lax.broadcasted_iota(jnp.int32, (rows, LANES), 1) // SSM_GROUP


def _s5_state_kernel(*refs):
    x_refs = refs[:SSM_BLOCK]
    wst_ref, u_ref, *state_refs = refs[SSM_BLOCK:]
    xs = [r[...] for r in x_refs]
    slot = _slot_mask(xs[0].shape[0])
    per_col = LANES // SSM_GROUP
    pieces = []
    for gl in range(SSM_CHUNK_GROUPS):
        for col in range(SSM_BLOCK // per_col):
            acc = jnp.zeros_like(xs[0])
            for k in range(per_col):
                moved = pltpu.roll(xs[col * per_col + k], ((k - gl) * SSM_GROUP) % LANES, 1)
                acc = jnp.where(slot == k, moved, acc)
            pieces.append(acc)
    cols_per_pair = 2 * SSM_ROW // LANES
    for pair in range(SSM_CHUNK_GROUPS // 2):
        u = jnp.concatenate(pieces[pair * cols_per_pair:(pair + 1) * cols_per_pair], axis=1).astype(BF16)
        u_ref[:, pair * 2 * SSM_ROW:(pair + 1) * 2 * SSM_ROW] = u
        for k, o_ref in enumerate(state_refs):
            o_ref[:, pair * 2 * SSM_STATE:(pair + 1) * 2 * SSM_STATE] = jnp.dot(
                u, wst_ref[pair, k], preferred_element_type=F32)


def _s5_row_tile(n_rows):
    return n_rows // 2 if n_rows % 32 == 0 else n_rows


def _s5_states(us_blocks, wst_pair, groups):
    n_rows = us_blocks.shape[0]
    chunks = groups // SSM_CHUNK_GROUPS
    pairs = SSM_CHUNK_GROUPS // 2
    rt = _s5_row_tile(n_rows)
    state = jax.ShapeDtypeStruct((n_rows, groups * SSM_STATE), F32)
    sspec = pl.BlockSpec((rt, SSM_CHUNK_GROUPS * SSM_STATE), lambda ch, i: (i, ch))
    uspec = pl.BlockSpec((rt, SSM_CHUNK_GROUPS * SSM_ROW), lambda ch, i: (i, ch))
    x_specs = [pl.BlockSpec((rt, LANES), functools.partial(lambda ch, i, s: (i, s * chunks + ch), s=s))
               for s in range(SSM_BLOCK)]
    return pl.pallas_call(
        _s5_state_kernel,
        name="s5_states",
        out_shape=[jax.ShapeDtypeStruct((n_rows, groups * SSM_ROW), BF16)] + [state] * 4,
        grid=(chunks, n_rows // rt),
        in_specs=x_specs + [pl.BlockSpec((pairs,) + wst_pair.shape[1:], lambda ch, i: (ch, 0, 0, 0))],
        out_specs=[uspec] + [sspec] * 4,
        compiler_params=_params("parallel", "parallel"),
    )(*([us_blocks] * SSM_BLOCK), wst_pair)


def _s5_scan_kernel(sfr_ref, sfi_ref, sbr_ref, sbi_ref, a_ref, hfr_ref, hfi_ref, hbr_ref, hbi_ref,
                    *, n_batch, n_lat_blocks, n_ctx_blocks):
    afr, afi, abr, abi = (a_ref[k:k + 1, :] for k in range(4))
    zero = jnp.zeros_like(afr)
    nl, nc = n_lat_blocks, n_ctx_blocks

    def step(t, carry):
        new = []
        in_ctx = t < nc
        for b in range(n_batch):
            lat0, ctx0 = b * nl, n_batch * nl + b * nc
            fr, fi, br, bi = carry[4 * b:4 * b + 4]
            rf = pl.ds(jnp.where(in_ctx, ctx0 + t, lat0 + t - nc), 1)
            rb = pl.ds(jnp.where(in_ctx, ctx0 + nc - 1 - t, lat0 + nl - 1 - (t - nc)), 1)
            hfr_ref[rf, :] = fr
            hfi_ref[rf, :] = fi
            hbr_ref[rb, :] = br
            hbi_ref[rb, :] = bi
            new += [afr * fr - afi * fi + sfr_ref[rf, :], afr * fi + afi * fr + sfi_ref[rf, :],
                    abr * br - abi * bi + sbr_ref[rb, :], abr * bi + abi * br + sbi_ref[rb, :]]
        return tuple(new)

    lax.fori_loop(0, nl + nc, step, (zero,) * (4 * n_batch))


def _s5_scan(s_parts, a_rows, n_batch, n_lat_blocks, n_ctx_blocks):
    rows, lanes = s_parts[0].shape
    tl = min(lanes, 4 * LANES)
    spec = pl.BlockSpec((rows, tl), lambda j: (0, j))
    return pl.pallas_call(
        functools.partial(_s5_scan_kernel, n_batch=n_batch, n_lat_blocks=n_lat_blocks, n_ctx_blocks=n_ctx_blocks),
        name="s5_scan",
        out_shape=[jax.ShapeDtypeStruct((rows, lanes), F32)] * 4,
        grid=(lanes // tl,),
        in_specs=[spec] * 4 + [pl.BlockSpec((4, tl), lambda j: (0, j))],
        out_specs=[spec] * 4,
        compiler_params=_params("parallel"),
    )(*s_parts, a_rows)


def _s5_out_kernel(u_ref, m_ref, et_ref, hfr_ref, hfi_ref, hbr_ref, hbi_ref, *y_refs):
    ys = []
    for pair in range(SSM_CHUNK_GROUPS // 2):
        y = jnp.dot(u_ref[:, pair * 2 * SSM_ROW:(pair + 1) * 2 * SSM_ROW], m_ref[pair], preferred_element_type=F32)
        for k, h_ref in enumerate((hfr_ref, hfi_ref, hbr_ref, hbi_ref)):
            h = h_ref[:, pair * 2 * SSM_STATE:(pair + 1) * 2 * SSM_STATE].astype(BF16)
            y += lax.dot_general(h, et_ref[pair, k], NT_DIMS, preferred_element_type=F32)
        ys.append(y)
    slot = _slot_mask(ys[0].shape[0])
    per_col = LANES // SSM_GROUP
    for s in range(SSM_BLOCK):
        col, k = divmod(s, per_col)
        acc = jnp.zeros((ys[0].shape[0], LANES), F32)
        for gl in range(SSM_CHUNK_GROUPS):
            pair, j = divmod(gl, 2)
            lo = j * SSM_ROW + col * LANES
            moved = pltpu.roll(ys[pair][:, lo:lo + LANES], ((gl - k) * SSM_GROUP) % LANES, 1)
            acc = jnp.where(slot == gl, moved, acc)
        y_refs[s][...] = acc


def _s5_outputs(u, m_pair, et_pair, h_parts, groups):
    n_rows = u.shape[0]
    chunks = groups // SSM_CHUNK_GROUPS
    pairs = SSM_CHUNK_GROUPS // 2
    rt = _s5_row_tile(n_rows)
    hspec = pl.BlockSpec((rt, SSM_CHUNK_GROUPS * SSM_STATE), lambda ch, i: (i, ch))
    yspec = pl.BlockSpec((rt, LANES), lambda ch, i: (i, ch))
    return pl.pallas_call(
        _s5_out_kernel,
        name="s5_out",
        out_shape=[jax.ShapeDtypeStruct((n_rows, groups * SSM_GROUP), F32)] * SSM_BLOCK,
        grid=(chunks, n_rows // rt),
        in_specs=[pl.BlockSpec((rt, SSM_CHUNK_GROUPS * SSM_ROW), lambda ch, i: (i, ch)),
                  pl.BlockSpec((pairs,) + m_pair.shape[1:], lambda ch, i: (ch, 0, 0)),
                  pl.BlockSpec((pairs,) + et_pair.shape[1:], lambda ch, i: (ch, 0, 0, 0)),
                  hspec, hspec, hspec, hspec],
        out_specs=[yspec] * SSM_BLOCK,
        compiler_params=_params("parallel", "parallel"),
    )(u, m_pair, et_pair, *h_parts)


def _s5_glu_kernel(ys_ref, us_ref, d_ref, w_ref, b_ref, o_ref, wb_ref):
    @pl.when(pl.program_id(0) == 0)
    def _():
        wb_ref[...] = w_ref[...].astype(BF16)

    y = jax.nn.gelu(ys_ref[...] + d_ref[...] * us_ref[...])
    gate = _sigmoid(jnp.dot(y.astype(BF16), wb_ref[...], preferred_element_type=F32) + b_ref[...])
    o_ref[...] = (y * gate).astype(o_ref.dtype)


def _s5_glu(y_s, us, d_skip, w_glu, layer, b_glu, n_rows):
    n, w = us.shape
    row = pl.BlockSpec((TM, w), lambda i: (i, 0))
    vec = pl.BlockSpec((1, w), lambda i: (0, 0))
    return pl.pallas_call(
        _s5_glu_kernel,
        name="s5_glu",
        out_shape=jax.ShapeDtypeStruct((n, w), BF16),
        grid=(n_rows // TM,),
        in_specs=[row, row, vec, pl.BlockSpec((None, w, w), lambda i: (layer, 0, 0)), vec],
        out_specs=row,
        scratch_shapes=[pltpu.VMEM((w, w), BF16)],
        compiler_params=_params("arbitrary"),
    )(y_s, us, d_skip.reshape(1, w), w_glu, b_glu.reshape(1, w))


def _s5(us, n_batch, seq, ctx_len, lam_re, lam_im, log_dt, b_re, b_im, c_re, c_im):
    n_all, width = us.shape
    groups = width // SSM_GROUP
    n_rows = n_all // SSM_BLOCK
    m, wst, et, a16 = _s5_gen(lam_re, lam_im, log_dt, b_re, b_im, c_re, c_im)
    m_pair = _pair_diag(m[:, None])[:, 0]
    wst_pair = _pair_diag(wst)
    et_pair = _pair_diag(et)
    a_rows = jnp.transpose(a16[:, :, 0, :], (1, 0, 2)).reshape(4, groups * SSM_STATE)
    u, *s_parts = _s5_states(us.reshape(n_rows, SSM_BLOCK * width), wst_pair, groups)
    h_parts = _s5_scan(s_parts, a_rows, n_batch, seq // SSM_BLOCK, ctx_len // SSM_BLOCK)
    y_steps = _s5_outputs(u, m_pair, et_pair, h_parts, groups)
    return jnp.stack(y_steps, axis=1).reshape(n_all, width)


def _rope_tables(seq):
    t = jnp.arange(seq)
    quarter = HEAD_DIM // 4
    inv = ROPE_BASE ** (-jnp.arange(quarter, dtype=F32) / quarter)
    ang_r = (t // GRID_W).astype(F32)[:, None] * inv[None, :]
    ang_c = (t % GRID_W).astype(F32)[:, None] * inv[None, :]
    cos = jnp.concatenate([jnp.cos(ang_r)] * 2 + [jnp.cos(ang_c)] * 2, axis=1)
    sin = jnp.concatenate([-jnp.sin(ang_r), jnp.sin(ang_r), -jnp.sin(ang_c), jnp.sin(ang_c)], axis=1)
    return cos, sin


def kernel(x, c, ctx, c_ctx, w_mod, b_mod, ln_g, ln_b, ffn1_w_in, ffn1_w_out, ffn2_w_in, ffn2_w_out, w_in, rpb, gm_ln_g, gm_ln_b, gm_ws, gm_bs, ssm_lam_re, ssm_lam_im, ssm_log_dt, ssm_b_re, ssm_b_im, ssm_c_re, ssm_c_im, ssm_d, ssm_w_glu, ssm_b_glu, w_pa, w_pb, w_pc, w_o):
    n_batch, seq, d = x.shape
    ctx_len = ctx.shape[1]
    depth = w_mod.shape[0]
    att_w = w_pb.shape[1]
    gm_w = w_pa.shape[1]
    ssm_w = w_pc.shape[1]
    assert ssm_w % LANES == 0 and seq % SSM_BLOCK == 0 and ctx_len % SSM_BLOCK == 0
    q_off, k_off, v_off = 0, att_w, 2 * att_w
    gm_off = 3 * att_w
    ssm_off = gm_off + 2 * gm_w
    gate_off = ssm_off + ssm_w
    alpha = (2.0 * depth) ** 0.25
    n_lat = n_batch * seq
    n_all = n_lat + n_batch * ctx_len
    assert seq % TM == 0 and (n_batch * ctx_len) % TM == 0 and seq % (ATT_ROWS * GRID_W) == 0
    assert seq // GRID_W >= ATT_KROWS and ctx_len % CHUNK == 0 and n_lat % ctx_len == 0
    assert n_batch + 1 <= MOD_ROWS and rpb.shape[2:] == (2 * WIN_H - 1, 2 * WIN_W - 1)

    seg = _seg_map(seq // TM_ROW, n_batch)
    tiles_per_batch = seq // TM

    x_lat, x_ctx = x.reshape(n_lat, d), ctx.reshape(n_batch * ctx_len, d)
    c_all = jnp.zeros((MOD_ROWS, d), F32).at[:n_batch].set(c).at[n_batch].set(c_ctx)
    mods = _mods(c_all, w_mod, b_mod)[:, :n_batch + 1].reshape(depth, n_batch + 1, N_MOD, d)
    cos, sin = _rope_tables(seq)

    xs = (x_lat, x_ctx)
    h = _modulate(x_lat, x_ctx, mods[0], 0, 1, seg)
    for l in range(depth):
        last = l == depth - 1
        n_mix = n_lat if last else n_all

        y = _mm(_ffn_in(h, ffn1_w_in, l, n_all), ffn1_w_out, l, 0, d, BF16, n_all, "ffn_out")
        xs, h = _norm(xs, y, mods[l], 2, 0.5, ln_g[l, 0], ln_b[l, 0], alpha, seg, n_all,
                      mods_next=mods[l], i_shift=3, i_scale=4)

        q, q_rot = _mm_rope(h, w_in, l, q_off, att_w, cos, sin, tiles_per_batch, n_mix)
        k, k_rot = _mm_rope(h, w_in, l, k_off, att_w, cos, sin, tiles_per_batch, n_all)
        v = _mm(h, w_in, l, v_off, att_w, BF16, n_all, "proj_v")
        z = _mm(h, w_in, l, gm_off, 2 * gm_w, F32, n_mix, "proj_z")
        us = _mm(h, w_in, l, ssm_off, ssm_w, F32, n_all, "proj_us")
        gates = _mm(h, w_in, l, gate_off, N_BRANCH * d, BF16, n_mix, "proj_gates", act="sigmoid")

        o_att = _attention(q, q_rot, k, k_rot, v, _rpb_table(rpb[l]), n_batch, seq, ctx_len)
        if not last:
            o_att = _ctx_attention(q, k, v, o_att, n_batch, seq, ctx_len)
        o_gm = _gmlp(z, gm_ln_g[l], gm_ln_b[l], gm_ws[l], gm_bs[l], n_mix)
        y_s = _s5(us, n_batch, seq, ctx_len, ssm_lam_re[l], ssm_lam_im[l], ssm_log_dt[l],
                  ssm_b_re[l], ssm_b_im[l], ssm_c_re[l], ssm_c_im[l])
        o_ssm = _s5_glu(y_s, us, ssm_d[l], ssm_w_glu, l, ssm_b_glu[l], n_mix)
        merged = _merge(o_gm, o_att, o_ssm, w_pa, w_pb, w_pc, l, gates, n_mix)
        y = _mm(merged, w_o, l, 0, d, BF16, n_mix, "out_proj")
        xs, h = _norm(xs, y, mods[l], 5, 1.0, ln_g[l, 1], ln_b[l, 1], alpha, seg, n_mix,
                      mods_next=mods[l], i_shift=6, i_scale=7)

        y = _mm(_ffn_in(h, ffn2_w_in, l, n_mix), ffn2_w_out, l, 0, d, BF16, n_mix, "ffn_out")
        if last:
            xs, _ = _norm(xs, y, mods[l], 8, 0.5, ln_g[l, 2], ln_b[l, 2], alpha, seg, n_mix, n_out=n_mix)
        else:
            xs, h = _norm(xs, y, mods[l], 8, 0.5, ln_g[l, 2], ln_b[l, 2], alpha, seg, n_mix,
                          mods_next=mods[l + 1], i_shift=0, i_scale=1)
    return xs.reshape(n_batch, seq, d)
```

```python
import functools

import jax
import jax.numpy as jnp
from jax import lax
from jax.experimental import pallas as pl
from jax.experimental.pallas import tpu as pltpu

F32 = jnp.float32
BF16 = jnp.bfloat16

GRID_W = 64
N_SUB = 3
N_MOD = 3 * N_SUB
HEAD_DIM = 128
WIN_H = 8
WIN_W = 16
ROPE_BASE = 10000.0
CHUNK = 128
GMLP_GROUP_DIM = 128
SSM_GROUP = 16
SSM_STATE = 64
N_BRANCH = 3
LN_EPS = 1e-6
NEG_INF = -1e30

LANES = 128
SSM_BLOCK = 16
SSM_ROW = SSM_BLOCK * SSM_GROUP
SSM_CHUNK_GROUPS = LANES // SSM_GROUP
MOD_ROWS = 8
VMEM_LIMIT_BYTES = 56 * 1024 * 1024

TM = 512
TM_ROW = 256
TN = 512
ATT_ROWS = 16
ATT_KROWS = ATT_ROWS + WIN_H
ATT_SUB = 4
ATT_SUB_KROWS = ATT_SUB + WIN_H
NT_DIMS = (((1,), (1,)), ((), ()))


def _params(*sem):
    return pltpu.CompilerParams(dimension_semantics=sem, vmem_limit_bytes=VMEM_LIMIT_BYTES)


def _ln(x):
    mu = jnp.mean(x, axis=-1, keepdims=True)
    xc = x - mu
    var = jnp.mean(xc * xc, axis=-1, keepdims=True)
    return xc * lax.rsqrt(var + LN_EPS)


def _sigmoid(x):
    return 0.5 * jnp.tanh(0.5 * x) + 0.5


def _col_tile(n_cols, col_off=0):
    for tn in range(TN, 0, -LANES):
        if n_cols % tn == 0 and col_off % tn == 0:
            return tn
    raise ValueError(f"column group ({col_off}, {n_cols}) is not lane aligned")


def _seg_map(tiles_per_batch, n_batch):
    return lambda i: jnp.minimum(i // tiles_per_batch, n_batch)


def _cast_weight_once(w_ref, wb_ref):
    @pl.when(pl.program_id(1) == 0)
    def _():
        wb_ref[...] = w_ref[...].astype(BF16)


def _mods_kernel(c_ref, w_ref, b_ref, o_ref):
    s = jax.nn.silu(c_ref[...]).astype(BF16)
    o_ref[...] = jnp.dot(s, w_ref[...].astype(BF16), preferred_element_type=F32) + b_ref[...]


def _mods(c_all, w_mod, b_mod):
    depth, d, nm = w_mod.shape
    tn = _col_tile(nm)
    return pl.pallas_call(
        _mods_kernel,
        name="mods",
        out_shape=jax.ShapeDtypeStruct((depth, MOD_ROWS, nm), F32),
        grid=(depth, nm // tn),
        in_specs=[pl.BlockSpec((MOD_ROWS, d), lambda l, j: (0, 0)),
                  pl.BlockSpec((None, d, tn), lambda l, j: (l, 0, j)),
                  pl.BlockSpec((None, 1, tn), lambda l, j: (l, 0, j))],
        out_specs=pl.BlockSpec((None, MOD_ROWS, tn), lambda l, j: (l, 0, j)),
        compiler_params=_params("parallel", "parallel"),
    )(c_all, w_mod, b_mod.reshape(depth, 1, nm))


def _split_specs(n_lat_tiles, d):
    return [pl.BlockSpec((TM_ROW, d), lambda i: (jnp.minimum(i, n_lat_tiles - 1), 0)),
            pl.BlockSpec((TM_ROW, d), lambda i: (jnp.maximum(i - n_lat_tiles, 0), 0))]


def _modulate_kernel(xl_ref, xc_ref, mod_ref, h_ref, *, n_lat_tiles, i_shift, i_scale):
    x = jnp.where(pl.program_id(0) < n_lat_tiles, xl_ref[...], xc_ref[...])
    h = _ln(x) * (1.0 + mod_ref[i_scale:i_scale + 1, :]) + mod_ref[i_shift:i_shift + 1, :]
    h_ref[...] = h.astype(BF16)


def _modulate(x_lat, x_ctx, mods, i_shift, i_scale, seg):
    n_lat, d = x_lat.shape
    n = n_lat + x_ctx.shape[0]
    n_lat_tiles = n_lat // TM_ROW
    return pl.pallas_call(
        functools.partial(_modulate_kernel, n_lat_tiles=n_lat_tiles, i_shift=i_shift, i_scale=i_scale),
        name="modulate",
        out_shape=jax.ShapeDtypeStruct((n, d), BF16),
        grid=(n // TM_ROW,),
        in_specs=_split_specs(n_lat_tiles, d) + [pl.BlockSpec((None, N_MOD, d), lambda i: (seg(i), 0, 0))],
        out_specs=pl.BlockSpec((TM_ROW, d), lambda i: (i, 0)),
        compiler_params=_params("parallel"),
    )(x_lat, x_ctx, mods)


def _norm_kernel(*refs, n_lat_tiles, alpha, coef, i_gate, i_shift, i_scale):
    if n_lat_tiles is None:
        x_ref, y_ref, modg_ref, g_ref, b_ref, *rest = refs
        x = x_ref[...]
    else:
        xl_ref, xc_ref, y_ref, modg_ref, g_ref, b_ref, *rest = refs
        x = jnp.where(pl.program_id(0) < n_lat_tiles, xl_ref[...], xc_ref[...])
    t = alpha * x + (coef * modg_ref[i_gate:i_gate + 1, :]) * y_ref[...].astype(F32)
    xn = _ln(t) * g_ref[...] + b_ref[...]
    if i_shift is None:
        (xo_ref,) = rest
        xo_ref[...] = xn
    else:
        modn_ref, xo_ref, h_ref = rest
        xo_ref[...] = xn
        h = _ln(xn) * (1.0 + modn_ref[i_scale:i_scale + 1, :]) + modn_ref[i_shift:i_shift + 1, :]
        h_ref[...] = h.astype(BF16)


def _norm(x, y, mods_gate, i_gate, coef, g, b, alpha, seg, n_rows, mods_next=None, i_shift=None, i_scale=None,
          n_out=None):
    split = isinstance(x, tuple)
    d = y.shape[1]
    n_out = y.shape[0] if n_out is None else n_out
    row = pl.BlockSpec((TM_ROW, d), lambda i: (i, 0))
    vec = pl.BlockSpec((1, d), lambda i: (0, 0))
    mod = pl.BlockSpec((None, N_MOD, d), lambda i: (seg(i), 0, 0))
    n_lat_tiles = x[0].shape[0] // TM_ROW if split else None
    in_specs = (_split_specs(n_lat_tiles, d) if split else [row]) + [row, mod, vec, vec]
    args = (list(x) if split else [x]) + [y, mods_gate, g.reshape(1, d), b.reshape(1, d)]
    out_shape = [jax.ShapeDtypeStruct((n_out, d), F32)]
    out_specs = [row]
    if i_shift is not None:
        in_specs.append(mod)
        args.append(mods_next)
        out_shape.append(jax.ShapeDtypeStruct((n_out, d), BF16))
        out_specs.append(row)
    out = pl.pallas_call(
        functools.partial(_norm_kernel, n_lat_tiles=n_lat_tiles, alpha=alpha, coef=coef, i_gate=i_gate,
                          i_shift=i_shift, i_scale=i_scale),
        name="norm",
        out_shape=out_shape,
        grid=(n_rows // TM_ROW,),
        in_specs=in_specs,
        out_specs=out_specs,
        compiler_params=_params("parallel"),
    )(*args)
    return out if i_shift is not None else (out[0], None)


def _ffn_in_kernel(h_ref, wg_ref, wu_ref, o_ref, wgb_ref, wub_ref):
    _cast_weight_once(wg_ref, wgb_ref)
    _cast_weight_once(wu_ref, wub_ref)
    h = h_ref[...]
    g = jnp.dot(h, wgb_ref[...], preferred_element_type=F32)
    u = jnp.dot(h, wub_ref[...], preferred_element_type=F32)
    o_ref[...] = (g * _sigmoid(g) * u).astype(o_ref.dtype)


def _ffn_in(h, w, layer, n_rows):
    n, d = h.shape
    dff = w.shape[2] // 2
    tn = _col_tile(dff)
    return pl.pallas_call(
        _ffn_in_kernel,
        name="ffn_in",
        out_shape=jax.ShapeDtypeStruct((n, dff), BF16),
        grid=(dff // tn, n_rows // TM),
        in_specs=[pl.BlockSpec((TM, d), lambda j, i: (i, 0)),
                  pl.BlockSpec((None, d, tn), lambda j, i: (layer, 0, j)),
                  pl.BlockSpec((None, d, tn), lambda j, i: (layer, 0, j + dff // tn))],
        out_specs=pl.BlockSpec((TM, tn), lambda j, i: (i, j)),
        scratch_shapes=[pltpu.VMEM((d, tn), BF16), pltpu.VMEM((d, tn), BF16)],
        compiler_params=_params("arbitrary", "arbitrary"),
    )(h, w, w)


def _mm_kernel(a_ref, w_ref, o_ref, wb_ref, *, act):
    _cast_weight_once(w_ref, wb_ref)
    acc = jnp.dot(a_ref[...], wb_ref[...], preferred_element_type=F32)
    if act == "sigmoid":
        acc = _sigmoid(acc)
    o_ref[...] = acc.astype(o_ref.dtype)


def _mm(a, w, layer, col_off, n_cols, out_dtype, n_rows, name, act=None):
    n, k = a.shape
    tn = _col_tile(n_cols, col_off)
    joff = col_off // tn
    return pl.pallas_call(
        functools.partial(_mm_kernel, act=act),
        name=name,
        out_shape=jax.ShapeDtypeStruct((n, n_cols), out_dtype),
        grid=(n_cols // tn, n_rows // TM),
        in_specs=[pl.BlockSpec((TM, k), lambda j, i: (i, 0)),
                  pl.BlockSpec((None, k, tn), lambda j, i: (layer, 0, j + joff))],
        out_specs=pl.BlockSpec((TM, tn), lambda j, i: (i, j)),
        scratch_shapes=[pltpu.VMEM((k, tn), BF16)],
        compiler_params=_params("arbitrary", "arbitrary"),
    )(a, w)


def _mm_rope_kernel(a_ref, w_ref, cos_ref, sin_ref, o_ref, orot_ref, wb_ref):
    _cast_weight_once(w_ref, wb_ref)
    acc = jnp.dot(a_ref[...], wb_ref[...], preferred_element_type=F32)
    o_ref[...] = acc.astype(o_ref.dtype)
    tn = acc.shape[1]
    reps = tn // HEAD_DIM
    cos = jnp.concatenate([cos_ref[...]] * reps, axis=1)
    sin = jnp.concatenate([sin_ref[...]] * reps, axis=1)
    lane = lax.broadcasted_iota(jnp.int32, acc.shape, 1)
    quarter = HEAD_DIM // 4
    first = (lane % (2 * quarter)) < quarter
    partner = jnp.where(first, pltpu.roll(acc, tn - quarter, 1), pltpu.roll(acc, quarter, 1))
    orot_ref[...] = (acc * cos + partner * sin).astype(orot_ref.dtype)


def _mm_rope(a, w, layer, col_off, n_cols, cos, sin, tiles_per_batch, n_rows):
    n, k = a.shape
    tn = _col_tile(n_cols, col_off)
    joff = col_off // tn
    out = jax.ShapeDtypeStruct((n, n_cols), BF16)
    tab = pl.BlockSpec((TM, HEAD_DIM), lambda j, i: (i % tiles_per_batch, 0))
    blk = pl.BlockSpec((TM, tn), lambda j, i: (i, j))
    return pl.pallas_call(
        _mm_rope_kernel,
        name="proj_rope",
        out_shape=[out, out],
        grid=(n_cols // tn, n_rows // TM),
        in_specs=[pl.BlockSpec((TM, k), lambda j, i: (i, 0)),
                  pl.BlockSpec((None, k, tn), lambda j, i: (layer, 0, j + joff)),
                  tab, tab],
        out_specs=[blk, blk],
        scratch_shapes=[pltpu.VMEM((k, tn), BF16)],
        compiler_params=_params("arbitrary", "arbitrary"),
    )(a, w, cos, sin)


def _merge_kernel(oa_ref, ob_ref, oc_ref, wa_ref, wb_ref, wc_ref, ga_ref, gb_ref, gc_ref, o_ref,
                  wab_ref, wbb_ref, wcb_ref):
    _cast_weight_once(wa_ref, wab_ref)
    _cast_weight_once(wb_ref, wbb_ref)
    _cast_weight_once(wc_ref, wcb_ref)
    t = ga_ref[...].astype(F32) * jnp.dot(oa_ref[...], wab_ref[...], preferred_element_type=F32)
    t += gb_ref[...].astype(F32) * jnp.dot(ob_ref[...], wbb_ref[...], preferred_element_type=F32)
    t += gc_ref[...].astype(F32) * jnp.dot(oc_ref[...], wcb_ref[...], preferred_element_type=F32)
    o_ref[...] = t.astype(o_ref.dtype)


def _merge(o_a, o_b, o_c, w_pa, w_pb, w_pc, layer, gates, n_rows):
    n = o_a.shape[0]
    d = w_pa.shape[2]
    tn = _col_tile(d)
    nj = d // tn

    def act(o):
        return pl.BlockSpec((TM, o.shape[1]), lambda j, i: (i, 0))

    def wgt(w):
        return pl.BlockSpec((None, w.shape[1], tn), lambda j, i: (layer, 0, j))

    def gate(k):
        return pl.BlockSpec((TM, tn), lambda j, i: (i, j + k * nj))

    return pl.pallas_call(
        _merge_kernel,
        name="merge",
        out_shape=jax.ShapeDtypeStruct((n, d), BF16),
        grid=(nj, n_rows // TM),
        in_specs=[act(o_a), act(o_b), act(o_c), wgt(w_pa), wgt(w_pb), wgt(w_pc), gate(0), gate(1), gate(2)],
        out_specs=pl.BlockSpec((TM, tn), lambda j, i: (i, j)),
        scratch_shapes=[pltpu.VMEM((w.shape[1], tn), BF16) for w in (w_pa, w_pb, w_pc)],
        compiler_params=_params("arbitrary", "arbitrary"),
    )(o_a, o_b, o_c, w_pa, w_pb, w_pc, gates, gates, gates)


def _attn_block_geometry(kind):
    half = WIN_H // 2
    if kind == 0:
        return 0, [max(i - half, 0) for i in range(ATT_ROWS)]
    if kind == 1:
        return -half, list(range(ATT_ROWS))
    return -(ATT_KROWS - ATT_ROWS), [min(half + i, ATT_KROWS - WIN_H) for i in range(ATT_ROWS)]


def _attn_sub_key_offset(kind, sub):
    _, first_key_row = _attn_block_geometry(kind)
    return min(min(first_key_row[sub * ATT_SUB:(sub + 1) * ATT_SUB]), ATT_KROWS - ATT_SUB_KROWS)


def _rpb_table_kernel(rpb_ref, o_ref):
    shape = (GRID_W, 2 * GRID_W)
    cq = lax.broadcasted_iota(jnp.int32, shape, 0)
    lane = lax.broadcasted_iota(jnp.int32, shape, 1)
    ck = lane % GRID_W
    second = lane >= GRID_W
    start = jnp.clip(cq - WIN_W // 2, 0, GRID_W - WIN_W)
    in_win = (ck >= start) & (ck < start + WIN_W)
    neg = jnp.full(shape, NEG_INF, F32)
    n_bias_rows = 2 * WIN_H - 1

    def half_tile(a, lane_off):
        row = jnp.broadcast_to(rpb_ref[a:a + 1, :], shape)
        return pltpu.roll(row, (lane_off - (WIN_W - 1)) % LANES, 1, stride=1, stride_axis=0)

    lo_half = [half_tile(a, 0) for a in range(n_bias_rows)]
    hi_half = [half_tile(a, GRID_W) for a in range(n_bias_rows)]
    for kind in range(3):
        diff, first_key_row = _attn_block_geometry(kind)
        for i in range(ATT_ROWS):
            lo = first_key_row[i]
            sub, isub = divmod(i, ATT_SUB)
            koff = _attn_sub_key_offset(kind, sub)
            for pair in range(ATT_SUB_KROWS // 2):
                j0, j1 = koff + 2 * pair, koff + 2 * pair + 1
                t0 = lo_half[j0 - i + diff + WIN_H - 1] if lo <= j0 < lo + WIN_H else neg
                t1 = hi_half[j1 - i + diff + WIN_H - 1] if lo <= j1 < lo + WIN_H else neg
                tile = jnp.where(in_win, jnp.where(second, t1, t0), neg)
                o_ref[kind, sub, isub * GRID_W:(isub + 1) * GRID_W,
                      pair * 2 * GRID_W:(pair + 1) * 2 * GRID_W] = tile


def _rpb_table(rpb):
    heads, nr, nc = rpb.shape
    rpb_pad = jnp.zeros((heads, 2 * WIN_H, LANES), F32).at[:, :nr, :nc].set(rpb)
    shape = (3, ATT_ROWS // ATT_SUB, ATT_SUB * GRID_W, ATT_SUB_KROWS * GRID_W)
    return pl.pallas_call(
        _rpb_table_kernel,
        name="rpb_table",
        out_shape=jax.ShapeDtypeStruct((heads,) + shape, F32),
        grid=(heads,),
        in_specs=[pl.BlockSpec((None, 2 * WIN_H, LANES), lambda h: (h, 0, 0))],
        out_specs=pl.BlockSpec((None,) + shape, lambda h: (h, 0, 0, 0, 0)),
        compiler_params=_params("parallel"),
    )(rpb_pad)


def _attn_kernel(q_ref, qr_ref, kr_ref, v_ref, kc_ref, vc_ref, bias_ref, o_ref, *, n_grid_rows, n_blocks, scale):
    rb = pl.program_id(2)
    kr0 = jnp.clip(rb * ATT_ROWS - WIN_H // 2, 0, n_grid_rows - ATT_KROWS)
    kc = kc_ref[...]
    vc = vc_ref[...]
    for sub in range(ATT_ROWS // ATT_SUB):
        offs = [_attn_sub_key_offset(kind, sub) for kind in range(3)]
        koff = jnp.where(rb == 0, offs[0], jnp.where(rb == n_blocks - 1, offs[2], offs[1]))
        keys = pl.ds(pl.multiple_of((kr0 + koff) * GRID_W, GRID_W), ATT_SUB_KROWS * GRID_W)
        rows = pl.ds(sub * ATT_SUB * GRID_W, ATT_SUB * GRID_W)
        s = lax.dot_general(qr_ref[rows, :], kr_ref[keys, :], NT_DIMS, preferred_element_type=F32) * scale
        s += bias_ref[sub]
        sc = lax.dot_general(q_ref[rows, :], kc, NT_DIMS, preferred_element_type=F32) * scale
        m = jnp.maximum(jnp.max(s, axis=-1, keepdims=True), jnp.max(sc, axis=-1, keepdims=True))
        p = jnp.exp(s - m)
        pc = jnp.exp(sc - m)
        denom = jnp.sum(p, axis=-1, keepdims=True) + jnp.sum(pc, axis=-1, keepdims=True)
        o = jnp.dot(p.astype(BF16), v_ref[keys, :], preferred_element_type=F32)
        o += jnp.dot(pc.astype(BF16), vc, preferred_element_type=F32)
        o_ref[rows, :] = (o / denom).astype(o_ref.dtype)


def _attention(q, q_rot, k, k_rot, v, bias, n_batch, seq, ctx_len):
    n, width = q.shape
    heads = width // HEAD_DIM
    n_grid_rows = seq // GRID_W
    tq = ATT_ROWS * GRID_W
    blocks = seq // tq
    ctx0 = n_batch * seq // ctx_len
    qspec = pl.BlockSpec((tq, HEAD_DIM), lambda b, h, r: (b * blocks + r, h))
    kspec = pl.BlockSpec((seq, HEAD_DIM), lambda b, h, r: (b, h))
    cspec = pl.BlockSpec((ctx_len, HEAD_DIM), lambda b, h, r: (ctx0 + b, h))

    def kind(r):
        return jnp.where(r == 0, 0, jnp.where(r == blocks - 1, 2, 1))

    return pl.pallas_call(
        functools.partial(_attn_kernel, n_grid_rows=n_grid_rows, n_blocks=blocks, scale=HEAD_DIM ** -0.5),
        name="attn",
        out_shape=jax.ShapeDtypeStruct((n, width), BF16),
        grid=(n_batch, heads, blocks),
        in_specs=[qspec, qspec, kspec, kspec, cspec, cspec,
                  pl.BlockSpec((None, None) + bias.shape[2:], lambda b, h, r: (h, kind(r), 0, 0, 0))],
        out_specs=qspec,
        compiler_params=_params("parallel", "parallel", "arbitrary"),
    )(q, q_rot, k_rot, v, k, v, bias)


def _ctx_attn_kernel(q_ref, k_ref, v_ref, prev_ref, o_ref, *, scale):
    del prev_ref
    s = lax.dot_general(q_ref[...], k_ref[...], NT_DIMS, preferred_element_type=F32) * scale
    m = jnp.max(s, axis=-1, keepdims=True)
    p = jnp.exp(s - m)
    denom = jnp.sum(p, axis=-1, keepdims=True)
    o = jnp.dot(p.astype(BF16), v_ref[...], preferred_element_type=F32)
    o_ref[...] = (o / denom).astype(o_ref.dtype)


def _ctx_attention(q, k, v, o_att, n_batch, seq, ctx_len):
    n, width = q.shape
    heads = width // HEAD_DIM
    ctx0 = n_batch * seq // ctx_len
    spec = pl.BlockSpec((ctx_len, HEAD_DIM), lambda b, h: (ctx0 + b, h))
    return pl.pallas_call(
        functools.partial(_ctx_attn_kernel, scale=HEAD_DIM ** -0.5),
        name="ctx_attn",
        out_shape=jax.ShapeDtypeStruct((n, width), BF16),
        grid=(n_batch, heads),
        in_specs=[spec, spec, spec, pl.BlockSpec(memory_space=pl.ANY)],
        out_specs=spec,
        input_output_aliases={3: 0},
        compiler_params=_params("parallel", "parallel"),
    )(q, k, v, o_att)


def _gmlp_kernel(z_ref, g_ref, b_ref, ws_ref, bs_ref, o_ref, *, width, chunks):
    z = jax.nn.gelu(z_ref[...])
    u = z[:, :width]
    v = (_ln(z[:, width:]) * g_ref[...] + b_ref[...]).astype(BF16)
    for c in range(chunks):
        rows = slice(c * CHUNK, (c + 1) * CHUNK)
        for g in range(width // GMLP_GROUP_DIM):
            cols = slice(g * GMLP_GROUP_DIM, (g + 1) * GMLP_GROUP_DIM)
            mixed = jnp.dot(ws_ref[g], v[rows, cols], preferred_element_type=F32) + bs_ref[:, cols]
            o_ref[rows, cols] = (u[rows, cols] * mixed).astype(o_ref.dtype)


def _gmlp(z, ln_g, ln_b, w_s, b_s, n_rows):
    n, two_w = z.shape
    width = two_w // 2
    groups = width // GMLP_GROUP_DIM
    chunks = TM // CHUNK
    bs = jnp.repeat(b_s.T, GMLP_GROUP_DIM, axis=1)
    vec = pl.BlockSpec((1, width), lambda i: (0, 0))
    return pl.pallas_call(
        functools.partial(_gmlp_kernel, width=width, chunks=chunks),
        name="gmlp",
        out_shape=jax.ShapeDtypeStruct((n, width), BF16),
        grid=(n_rows // TM,),
        in_specs=[pl.BlockSpec((TM, two_w), lambda i: (i, 0)), vec, vec,
                  pl.BlockSpec((groups, CHUNK, CHUNK), lambda i: (0, 0, 0)),
                  pl.BlockSpec((CHUNK, width), lambda i: (0, 0))],
        out_specs=pl.BlockSpec((TM, width), lambda i: (i, 0)),
        compiler_params=_params("parallel"),
    )(z, ln_g.reshape(1, width), ln_b.reshape(1, width), w_s.astype(BF16), bs)


def _s5_gen_kernel(lr_ref, li_ref, ldt_ref, bre_ref, bim_ref, cre_ref, cim_ref, m_ref, wst_ref, et_ref, a_ref):
    hi = lax.Precision.HIGHEST
    nb = SSM_BLOCK
    lane = lax.broadcasted_iota(jnp.int32, (SSM_GROUP, SSM_ROW), 1)
    state_lane = lax.broadcasted_iota(jnp.int32, (SSM_GROUP, 2 * SSM_STATE), 1)
    in_group = [state_lane < SSM_STATE, state_lane >= SSM_STATE]
    m_rows = [[None] * nb for _ in range(2)]
    for d in range(2):
        lr = lr_ref[d:d + 1, :]
        li = li_ref[d:d + 1, :]
        dt = jnp.exp(ldt_ref[d:d + 1, :])
        mag = jnp.exp(lr * dt)
        ar = mag * jnp.cos(li * dt)
        ai = mag * jnp.sin(li * dt)
        den = lr * lr + li * li
        fr = ((ar - 1.0) * lr + ai * li) / den
        fi = (ai * lr - (ar - 1.0) * li) / den
        br = bre_ref[d]
        bi = bim_ref[d]
        bbr = fr * br - fi * bi
        bbi = fr * bi + fi * br
        cr = cre_ref[d]
        ci = cim_ref[d]
        pr = [jnp.ones_like(ar)]
        pi = [jnp.zeros_like(ai)]
        for _ in range(nb):
            pr.append(pr[-1] * ar - pi[-1] * ai)
            pi.append(pr[-2] * ai + pi[-1] * ar)
        a_ref[2 * d] = pr[nb]
        a_ref[2 * d + 1] = pi[nb]
        e_state = [(t + 1) if d == 0 else (nb - t) for t in range(nb)]
        e_in = [(nb - 1 - s) if d == 0 else s for s in range(nb)]
        e_lag = [e if d == 0 else (nb - 1 - e) for e in range(nb)]
        for s in range(nb):
            wr, wi = pr[e_in[s]], pi[e_in[s]]
            er, ei = pr[e_state[s]], pi[e_state[s]]
            parts = (wr * bbr - wi * bbi, wr * bbi + wi * bbr)
            outs = (cr * er - ci * ei, -(cr * ei + ci * er))
            for g in range(2):
                rows = slice(g * SSM_ROW + s * SSM_GROUP, g * SSM_ROW + (s + 1) * SSM_GROUP)
                for part in range(2):
                    wst_ref[2 * d + part, rows, :] = jnp.where(in_group[g], parts[part], 0.0).astype(wst_ref.dtype)
                    et_ref[2 * d + part, rows, :] = jnp.where(in_group[g], outs[part], 0.0).astype(et_ref.dtype)
        lag_r = jnp.concatenate([cr * pr[e] - ci * pi[e] for e in e_lag], axis=0)
        lag_i = jnp.concatenate([-(cr * pi[e] + ci * pr[e]) for e in e_lag], axis=0)
        for g in range(2):
            gbr = jnp.where(in_group[g], bbr, 0.0)
            gbi = jnp.where(in_group[g], bbi, 0.0)
            kbase = (lax.dot_general(gbr, lag_r, NT_DIMS, precision=hi, preferred_element_type=F32)
                     + lax.dot_general(gbi, lag_i, NT_DIMS, precision=hi, preferred_element_type=F32))
            for s in range(nb):
                if d == 0:
                    sh = s * SSM_GROUP
                    m_rows[g][s] = kbase if sh == 0 else jnp.where(lane >= sh, pltpu.roll(kbase, sh, 1), 0.0)
                else:
                    sh = (nb - 1 - s) * SSM_GROUP
                    row = kbase if sh == 0 else jnp.where(lane < SSM_ROW - sh, pltpu.roll(kbase, SSM_ROW - sh, 1), 0.0)
                    m_rows[g][s] = m_rows[g][s] + row
    zeros = jnp.zeros((SSM_GROUP, SSM_ROW), m_ref.dtype)
    for g in range(2):
        for s in range(nb):
            rows = slice(g * SSM_ROW + s * SSM_GROUP, g * SSM_ROW + (s + 1) * SSM_GROUP)
            m_ref[rows, g * SSM_ROW:(g + 1) * SSM_ROW] = m_rows[g][s].astype(m_ref.dtype)
            m_ref[rows, (1 - g) * SSM_ROW:(2 - g) * SSM_ROW] = zeros


def _s5_gen(lam_re, lam_im, log_dt, b_re, b_im, c_re, c_im):
    _, groups, p = lam_re.shape
    pairs = groups // 2

    def vec_pairs(x):
        return jnp.swapaxes(x.reshape(2, pairs, 2 * p), 0, 1)

    def mat_pairs(x):
        x = x.reshape(2, pairs, 2, SSM_GROUP, p)
        return jnp.transpose(x, (1, 0, 3, 2, 4)).reshape(pairs, 2, SSM_GROUP, 2 * p)

    ldt = jnp.broadcast_to(log_dt[:, :, None], lam_re.shape)
    vec = pl.BlockSpec((None, 2, 2 * p), lambda g: (g, 0, 0))
    mat = pl.BlockSpec((None, 2, SSM_GROUP, 2 * p), lambda g: (g, 0, 0, 0))
    return pl.pallas_call(
        _s5_gen_kernel,
        name="s5_gen",
        out_shape=[jax.ShapeDtypeStruct((pairs, 2 * SSM_ROW, 2 * SSM_ROW), BF16),
                   jax.ShapeDtypeStruct((pairs, 4, 2 * SSM_ROW, 2 * p), BF16),
                   jax.ShapeDtypeStruct((pairs, 4, 2 * SSM_ROW, 2 * p), BF16),
                   jax.ShapeDtypeStruct((pairs, 4, 1, 2 * p), F32)],
        grid=(pairs,),
        in_specs=[vec, vec, vec, mat, mat, mat, mat],
        out_specs=[pl.BlockSpec((None, 2 * SSM_ROW, 2 * SSM_ROW), lambda g: (g, 0, 0)),
                   pl.BlockSpec((None, 4, 2 * SSM_ROW, 2 * p), lambda g: (g, 0, 0, 0)),
                   pl.BlockSpec((None, 4, 2 * SSM_ROW, 2 * p), lambda g: (g, 0, 0, 0)),
                   pl.BlockSpec((None, 4, 1, 2 * p), lambda g: (g, 0, 0, 0))],
        compiler_params=_params("parallel"),
    )(vec_pairs(lam_re), vec_pairs(lam_im), vec_pairs(ldt),
      mat_pairs(jnp.swapaxes(b_re, 2, 3)), mat_pairs(jnp.swapaxes(b_im, 2, 3)), mat_pairs(c_re), mat_pairs(c_im))


def _slot_mask(rows):
    return lax.broadcasted_iota(jnp.int32, (rows, LANES), 1) // SSM_GROUP


def _s5_state_kernel(x_ref, wst_ref, u_ref, *state_refs):
    n_blk = u_ref.shape[0]
    xs = [x_ref[pl.ds(s, n_blk, stride=SSM_BLOCK), :] for s in range(SSM_BLOCK)]
    slot = _slot_mask(n_blk)
    per_col = LANES // SSM_GROUP
    pieces = []
    for gl in range(SSM_CHUNK_GROUPS):
        for col in range(SSM_BLOCK // per_col):
            acc = jnp.zeros_like(xs[0])
            for k in range(per_col):
                moved = pltpu.roll(xs[col * per_col + k], ((k - gl) * SSM_GROUP) % LANES, 1)
                acc = jnp.where(slot == k, moved, acc)
            pieces.append(acc)
    cols_per_pair = 2 * SSM_ROW // LANES
    for pair in range(SSM_CHUNK_GROUPS // 2):
        u = jnp.concatenate(pieces[pair * cols_per_pair:(pair + 1) * cols_per_pair], axis=1).astype(BF16)
        u_ref[:, pair * 2 * SSM_ROW:(pair + 1) * 2 * SSM_ROW] = u
        for k, o_ref in enumerate(state_refs):
            o_ref[:, pair * 2 * SSM_STATE:(pair + 1) * 2 * SSM_STATE] = jnp.dot(
                u, wst_ref[pair, k], preferred_element_type=F32)


def _s5_row_tile(n_rows):
    return n_rows // 2 if n_rows % 32 == 0 else n_rows


def _s5_states(us, wst_pair, groups):
    n_rows = us.shape[0] // SSM_BLOCK
    chunks = groups // SSM_CHUNK_GROUPS
    pairs = SSM_CHUNK_GROUPS // 2
    rt = _s5_row_tile(n_rows)
    state = jax.ShapeDtypeStruct((n_rows, groups * SSM_STATE), F32)
    sspec = pl.BlockSpec((rt, SSM_CHUNK_GROUPS * SSM_STATE), lambda ch, i: (i, ch))
    uspec = pl.BlockSpec((rt, SSM_CHUNK_GROUPS * SSM_ROW), lambda ch, i: (i, ch))
    return pl.pallas_call(
        _s5_state_kernel,
        name="s5_states",
        out_shape=[jax.ShapeDtypeStruct((n_rows, groups * SSM_ROW), BF16)] + [state] * 4,
        grid=(chunks, n_rows // rt),
        in_specs=[pl.BlockSpec((rt * SSM_BLOCK, LANES), lambda ch, i: (i, ch)),
                  pl.BlockSpec((pairs,) + wst_pair.shape[1:], lambda ch, i: (ch, 0, 0, 0))],
        out_specs=[uspec] + [sspec] * 4,
        compiler_params=_params("parallel", "parallel"),
    )(us, wst_pair)


def _s5_scan_kernel(sfr_ref, sfi_ref, sbr_ref, sbi_ref, a_ref, hfr_ref, hfi_ref, hbr_ref, hbi_ref,
                    *, n_batch, n_lat_blocks, n_ctx_blocks):
    afr, afi, abr, abi = (a_ref[k:k + 1, :] for k in range(4))
    zero = jnp.zeros_like(afr)
    nl, nc = n_lat_blocks, n_ctx_blocks

    def step(t, carry):
        new = []
        in_ctx = t < nc
        for b in range(n_batch):
            lat0, ctx0 = b * nl, n_batch * nl + b * nc
            fr, fi, br, bi = carry[4 * b:4 * b + 4]
            rf = pl.ds(jnp.where(in_ctx, ctx0 + t, lat0 + t - nc), 1)
            rb = pl.ds(jnp.where(in_ctx, ctx0 + nc - 1 - t, lat0 + nl - 1 - (t - nc)), 1)
            hfr_ref[rf, :] = fr
            hfi_ref[rf, :] = fi
            hbr_ref[rb, :] = br
            hbi_ref[rb, :] = bi
            new += [afr * fr - afi * fi + sfr_ref[rf, :], afr * fi + afi * fr + sfi_ref[rf, :],
                    abr * br - abi * bi + sbr_ref[rb, :], abr * bi + abi * br + sbi_ref[rb, :]]
        return tuple(new)

    lax.fori_loop(0, nl + nc, step, (zero,) * (4 * n_batch))


def _s5_scan(s_parts, a_rows, n_batch, n_lat_blocks, n_ctx_blocks):
    rows, lanes = s_parts[0].shape
    tl = min(lanes, 4 * LANES)
    spec = pl.BlockSpec((rows, tl), lambda j: (0, j))
    return pl.pallas_call(
        functools.partial(_s5_scan_kernel, n_batch=n_batch, n_lat_blocks=n_lat_blocks, n_ctx_blocks=n_ctx_blocks),
        name="s5_scan",
        out_shape=[jax.ShapeDtypeStruct((rows, lanes), F32)] * 4,
        grid=(lanes // tl,),
        in_specs=[spec] * 4 + [pl.BlockSpec((4, tl), lambda j: (0, j))],
        out_specs=[spec] * 4,
        compiler_params=_params("parallel"),
    )(*s_parts, a_rows)


def _s5_out_kernel(u_ref, m_ref, et_ref, hfr_ref, hfi_ref, hbr_ref, hbi_ref, y_ref):
    ys = []
    for pair in range(SSM_CHUNK_GROUPS // 2):
        y = jnp.dot(u_ref[:, pair * 2 * SSM_ROW:(pair + 1) * 2 * SSM_ROW], m_ref[pair], preferred_element_type=F32)
        for k, h_ref in enumerate((hfr_ref, hfi_ref, hbr_ref, hbi_ref)):
            h = h_ref[:, pair * 2 * SSM_STATE:(pair + 1) * 2 * SSM_STATE].astype(BF16)
            y += lax.dot_general(h, et_ref[pair, k], NT_DIMS, preferred_element_type=F32)
        ys.append(y)
    n_blk = ys[0].shape[0]
    slot = _slot_mask(n_blk)
    per_col = LANES // SSM_GROUP
    for s in range(SSM_BLOCK):
        col, k = divmod(s, per_col)
        acc = jnp.zeros((n_blk, LANES), F32)
        for gl in range(SSM_CHUNK_GROUPS):
            pair, j = divmod(gl, 2)
            lo = j * SSM_ROW + col * LANES
            moved = pltpu.roll(ys[pair][:, lo:lo + LANES], ((gl - k) * SSM_GROUP) % LANES, 1)
            acc = jnp.where(slot == gl, moved, acc)
        y_ref[pl.ds(s, n_blk, stride=SSM_BLOCK), :] = acc


def _s5_outputs(u, m_pair, et_pair, h_parts, groups):
    n_rows = u.shape[0]
    chunks = groups // SSM_CHUNK_GROUPS
    pairs = SSM_CHUNK_GROUPS // 2
    rt = _s5_row_tile(n_rows)
    hspec = pl.BlockSpec((rt, SSM_CHUNK_GROUPS * SSM_STATE), lambda ch, i: (i, ch))
    return pl.pallas_call(
        _s5_out_kernel,
        name="s5_out",
        out_shape=jax.ShapeDtypeStruct((n_rows * SSM_BLOCK, groups * SSM_GROUP), F32),
        grid=(chunks, n_rows // rt),
        in_specs=[pl.BlockSpec((rt, SSM_CHUNK_GROUPS * SSM_ROW), lambda ch, i: (i, ch)),
                  pl.BlockSpec((pairs,) + m_pair.shape[1:], lambda ch, i: (ch, 0, 0)),
                  pl.BlockSpec((pairs,) + et_pair.shape[1:], lambda ch, i: (ch, 0, 0, 0)),
                  hspec, hspec, hspec, hspec],
        out_specs=pl.BlockSpec((rt * SSM_BLOCK, LANES), lambda ch, i: (i, ch)),
        compiler_params=_params("parallel", "parallel"),
    )(u, m_pair, et_pair, *h_parts)


def _s5_glu_kernel(ys_ref, us_ref, d_ref, w_ref, b_ref, o_ref, wb_ref):
    @pl.when(pl.program_id(0) == 0)
    def _():
        wb_ref[...] = w_ref[...].astype(BF16)

    y = jax.nn.gelu(ys_ref[...] + d_ref[...] * us_ref[...])
    gate = _sigmoid(jnp.dot(y.astype(BF16), wb_ref[...], preferred_element_type=F32) + b_ref[...])
    o_ref[...] = (y * gate).astype(o_ref.dtype)


def _s5_glu(y_s, us, d_skip, w_glu, layer, b_glu, n_rows):
    n, w = us.shape
    row = pl.BlockSpec((TM, w), lambda i: (i, 0))
    vec = pl.BlockSpec((1, w), lambda i: (0, 0))
    return pl.pallas_call(
        _s5_glu_kernel,
        name="s5_glu",
        out_shape=jax.ShapeDtypeStruct((n, w), BF16),
        grid=(n_rows // TM,),
        in_specs=[row, row, vec, pl.BlockSpec((None, w, w), lambda i: (layer, 0, 0)), vec],
        out_specs=row,
        scratch_shapes=[pltpu.VMEM((w, w), BF16)],
        compiler_params=_params("arbitrary"),
    )(y_s, us, d_skip.reshape(1, w), w_glu, b_glu.reshape(1, w))


def _s5(us, n_batch, seq, ctx_len, lam_re, lam_im, log_dt, b_re, b_im, c_re, c_im):
    groups = us.shape[1] // SSM_GROUP
    m_pair, wst_pair, et_pair, a_pair = _s5_gen(lam_re, lam_im, log_dt, b_re, b_im, c_re, c_im)
    a_rows = jnp.transpose(a_pair[:, :, 0, :], (1, 0, 2)).reshape(4, groups * SSM_STATE)
    u, *s_parts = _s5_states(us, wst_pair, groups)
    h_parts = _s5_scan(s_parts, a_rows, n_batch, seq // SSM_BLOCK, ctx_len // SSM_BLOCK)
    return _s5_outputs(u, m_pair, et_pair, h_parts, groups)


def _rope_tables(seq):
    t = jnp.arange(seq)
    quarter = HEAD_DIM // 4
    inv = ROPE_BASE ** (-jnp.arange(quarter, dtype=F32) / quarter)
    ang_r = (t // GRID_W).astype(F32)[:, None] * inv[None, :]
    ang_c = (t % GRID_W).astype(F32)[:, None] * inv[None, :]
    cos = jnp.concatenate([jnp.cos(ang_r)] * 2 + [jnp.cos(ang_c)] * 2, axis=1)
    sin = jnp.concatenate([-jnp.sin(ang_r), jnp.sin(ang_r), -jnp.sin(ang_c), jnp.sin(ang_c)], axis=1)
    return cos, sin


def kernel(x, c, ctx, c_ctx, w_mod, b_mod, ln_g, ln_b, ffn1_w_in, ffn1_w_out, ffn2_w_in, ffn2_w_out, w_in, rpb, gm_ln_g, gm_ln_b, gm_ws, gm_bs, ssm_lam_re, ssm_lam_im, ssm_log_dt, ssm_b_re, ssm_b_im, ssm_c_re, ssm_c_im, ssm_d, ssm_w_glu, ssm_b_glu, w_pa, w_pb, w_pc, w_o):
    n_batch, seq, d = x.shape
    ctx_len = ctx.shape[1]
    depth = w_mod.shape[0]
    att_w = w_pb.shape[1]
    gm_w = w_pa.shape[1]
    ssm_w = w_pc.shape[1]
    assert ssm_w % LANES == 0 and seq % SSM_BLOCK == 0 and ctx_len % SSM_BLOCK == 0
    q_off, k_off, v_off = 0, att_w, 2 * att_w
    gm_off = 3 * att_w
    ssm_off = gm_off + 2 * gm_w
    gate_off = ssm_off + ssm_w
    alpha = (2.0 * depth) ** 0.25
    n_lat = n_batch * seq
    n_all = n_lat + n_batch * ctx_len
    assert seq % TM == 0 and (n_batch * ctx_len) % TM == 0 and seq % (ATT_ROWS * GRID_W) == 0
    assert seq // GRID_W >= ATT_KROWS and ctx_len % CHUNK == 0 and n_lat % ctx_len == 0
    assert n_batch + 1 <= MOD_ROWS and rpb.shape[2:] == (2 * WIN_H - 1, 2 * WIN_W - 1)

    seg = _seg_map(seq // TM_ROW, n_batch)
    tiles_per_batch = seq // TM

    x_lat, x_ctx = x.reshape(n_lat, d), ctx.reshape(n_batch * ctx_len, d)
    c_all = jnp.zeros((MOD_ROWS, d), F32).at[:n_batch].set(c).at[n_batch].set(c_ctx)
    mods = _mods(c_all, w_mod, b_mod)[:, :n_batch + 1].reshape(depth, n_batch + 1, N_MOD, d)
    cos, sin = _rope_tables(seq)

    xs = (x_lat, x_ctx)
    h = _modulate(x_lat, x_ctx, mods[0], 0, 1, seg)
    for l in range(depth):
        last = l == depth - 1
        n_mix = n_lat if last else n_all

        y = _mm(_ffn_in(h, ffn1_w_in, l, n_all), ffn1_w_out, l, 0, d, BF16, n_all, "ffn_out")
        xs, h = _norm(xs, y, mods[l], 2, 0.5, ln_g[l, 0], ln_b[l, 0], alpha, seg, n_all,
                      mods_next=mods[l], i_shift=3, i_scale=4)

        q, q_rot = _mm_rope(h, w_in, l, q_off, att_w, cos, sin, tiles_per_batch, n_mix)
        k, k_rot = _mm_rope(h, w_in, l, k_off, att_w, cos, sin, tiles_per_batch, n_all)
        v = _mm(h, w_in, l, v_off, att_w, BF16, n_all, "proj_v")
        z = _mm(h, w_in, l, gm_off, 2 * gm_w, F32, n_mix, "proj_z")
        us = _mm(h, w_in, l, ssm_off, ssm_w, F32, n_all, "proj_us")
        gates = _mm(h, w_in, l, gate_off, N_BRANCH * d, BF16, n_mix, "proj_gates", act="sigmoid")

        o_att = _attention(q, q_rot, k, k_rot, v, _rpb_table(rpb[l]), n_batch, seq, ctx_len)
        if not last:
            o_att = _ctx_attention(q, k, v, o_att, n_batch, seq, ctx_len)
        o_gm = _gmlp(z, gm_ln_g[l], gm_ln_b[l], gm_ws[l], gm_bs[l], n_mix)
        y_s = _s5(us, n_batch, seq, ctx_len, ssm_lam_re[l], ssm_lam_im[l], ssm_log_dt[l],
                  ssm_b_re[l], ssm_b_im[l], ssm_c_re[l], ssm_c_im[l])
        o_ssm = _s5_glu(y_s, us, ssm_d[l], ssm_w_glu, l, ssm_b_glu[l], n_mix)
        merged = _merge(o_gm, o_att, o_ssm, w_pa, w_pb, w_pc, l, gates, n_mix)
        y = _mm(merged, w_o, l, 0, d, BF16, n_mix, "out_proj")
        xs, h = _norm(xs, y, mods[l], 5, 1.0, ln_g[l, 1], ln_b[l, 1], alpha, seg, n_mix,
                      mods_next=mods[l], i_shift=6, i_scale=7)

        y = _mm(_ffn_in(h, ffn2_w_in, l, n_mix), ffn2_w_out, l, 0, d, BF16, n_mix, "ffn_out")
        if last:
            xs, _ = _norm(xs, y, mods[l], 8, 0.5, ln_g[l, 2], ln_b[l, 2], alpha, seg, n_mix, n_out=n_mix)
        else:
            xs, h = _norm(xs, y, mods[l], 8, 0.5, ln_g[l, 2], ln_b[l, 2], alpha, seg, n_mix,
                          mods_next=mods[l + 1], i_shift=0, i_scale=1)
    return xs.reshape(n_batch, seq, d)
```

```python
import functools

import jax
import jax.numpy as jnp
from jax import lax
from jax.experimental import pallas as pl
from jax.experimental.pallas import tpu as pltpu

F32 = jnp.float32
BF16 = jnp.bfloat16

GRID_W = 64
N_SUB = 3
N_MOD = 3 * N_SUB
HEAD_DIM = 128
WIN_H = 8
WIN_W = 16
ROPE_BASE = 10000.0
CHUNK = 128
GMLP_GROUP_DIM = 128
SSM_GROUP = 16
SSM_STATE = 64
N_BRANCH = 3
LN_EPS = 1e-6
NEG_INF = -1e30

LANES = 128
SSM_BLOCK = 16
SSM_ROW = SSM_BLOCK * SSM_GROUP
SSM_CHUNK_GROUPS = LANES // SSM_GROUP
MOD_ROWS = 8
VMEM_LIMIT_BYTES = 60 * 1024 * 1024
VMEM_MARGIN_BYTES = 2 * 1024 * 1024

TM = 512
TM_ROW = 256
TN = 512
MERGE_TN = 256
ATT_ROWS = 16
ATT_KROWS = ATT_ROWS + WIN_H
ATT_SUB = 4
ATT_SUB_KROWS = ATT_SUB + WIN_H
NT_DIMS = (((1,), (1,)), ((), ()))


def _params(*sem):
    return pltpu.CompilerParams(dimension_semantics=sem, vmem_limit_bytes=VMEM_LIMIT_BYTES)


def _ln(x):
    mu = jnp.mean(x, axis=-1, keepdims=True)
    xc = x - mu
    var = jnp.mean(xc * xc, axis=-1, keepdims=True)
    return xc * lax.rsqrt(var + LN_EPS)


def _sigmoid(x):
    return 0.5 * jnp.tanh(0.5 * x) + 0.5


def _col_tile(n_cols, col_off=0, max_tn=TN):
    for tn in range(max_tn, 0, -LANES):
        if n_cols % tn == 0 and col_off % tn == 0:
            return tn
    raise ValueError(f"column group ({col_off}, {n_cols}) is not lane aligned")


def _wide_tile(k, out_bytes_per_col):
    tn = 2 * TN
    need = k * tn * (2 * 4 + 2) + 2 * TM * k * 2 + 2 * TM * tn * out_bytes_per_col
    return tn if need <= VMEM_LIMIT_BYTES - VMEM_MARGIN_BYTES else TN


def _seg_map(tiles_per_batch, n_batch):
    return lambda i: jnp.minimum(i // tiles_per_batch, n_batch)


def _cast_weight_once(w_ref, wb_ref):
    @pl.when(pl.program_id(1) == 0)
    def _():
        wb_ref[...] = w_ref[...].astype(BF16)


def _mods_kernel(c_ref, w_ref, b_ref, o_ref):
    s = jax.nn.silu(c_ref[...]).astype(BF16)
    o_ref[...] = jnp.dot(s, w_ref[...].astype(BF16), preferred_element_type=F32) + b_ref[...]


def _mods(c_all, w_mod, b_mod):
    depth, d, nm = w_mod.shape
    tn = _col_tile(nm)
    return pl.pallas_call(
        _mods_kernel,
        name="mods",
        out_shape=jax.ShapeDtypeStruct((depth, MOD_ROWS, nm), F32),
        grid=(depth, nm // tn),
        in_specs=[pl.BlockSpec((MOD_ROWS, d), lambda l, j: (0, 0)),
                  pl.BlockSpec((None, d, tn), lambda l, j: (l, 0, j)),
                  pl.BlockSpec((None, 1, tn), lambda l, j: (l, 0, j))],
        out_specs=pl.BlockSpec((None, MOD_ROWS, tn), lambda l, j: (l, 0, j)),
        compiler_params=_params("parallel", "parallel"),
    )(c_all, w_mod, b_mod.reshape(depth, 1, nm))


def _split_specs(n_lat_tiles, d):
    return [pl.BlockSpec((TM_ROW, d), lambda i: (jnp.minimum(i, n_lat_tiles - 1), 0)),
            pl.BlockSpec((TM_ROW, d), lambda i: (jnp.maximum(i - n_lat_tiles, 0), 0))]


def _modulate_kernel(xl_ref, xc_ref, mod_ref, h_ref, *, n_lat_tiles, i_shift, i_scale):
    x = jnp.where(pl.program_id(0) < n_lat_tiles, xl_ref[...], xc_ref[...])
    h = _ln(x) * (1.0 + mod_ref[i_scale:i_scale + 1, :]) + mod_ref[i_shift:i_shift + 1, :]
    h_ref[...] = h.astype(BF16)


def _modulate(x_lat, x_ctx, mods, i_shift, i_scale, seg):
    n_lat, d = x_lat.shape
    n = n_lat + x_ctx.shape[0]
    n_lat_tiles = n_lat // TM_ROW
    return pl.pallas_call(
        functools.partial(_modulate_kernel, n_lat_tiles=n_lat_tiles, i_shift=i_shift, i_scale=i_scale),
        name="modulate",
        out_shape=jax.ShapeDtypeStruct((n, d), BF16),
        grid=(n // TM_ROW,),
        in_specs=_split_specs(n_lat_tiles, d) + [pl.BlockSpec((None, N_MOD, d), lambda i: (seg(i), 0, 0))],
        out_specs=pl.BlockSpec((TM_ROW, d), lambda i: (i, 0)),
        compiler_params=_params("parallel"),
    )(x_lat, x_ctx, mods)


def _norm_kernel(*refs, n_lat_tiles, alpha, coef, i_gate, i_shift, i_scale):
    if n_lat_tiles is None:
        x_ref, y_ref, modg_ref, g_ref, b_ref, *rest = refs
        x = x_ref[...]
    else:
        xl_ref, xc_ref, y_ref, modg_ref, g_ref, b_ref, *rest = refs
        x = jnp.where(pl.program_id(0) < n_lat_tiles, xl_ref[...], xc_ref[...])
    t = alpha * x + (coef * modg_ref[i_gate:i_gate + 1, :]) * y_ref[...].astype(F32)
    xn = _ln(t) * g_ref[...] + b_ref[...]
    if i_shift is None:
        (xo_ref,) = rest
        xo_ref[...] = xn
    else:
        modn_ref, xo_ref, h_ref = rest
        xo_ref[...] = xn
        h = _ln(xn) * (1.0 + modn_ref[i_scale:i_scale + 1, :]) + modn_ref[i_shift:i_shift + 1, :]
        h_ref[...] = h.astype(BF16)


def _norm(x, y, mods_gate, i_gate, coef, g, b, alpha, seg, n_rows, mods_next=None, i_shift=None, i_scale=None,
          n_out=None):
    split = isinstance(x, tuple)
    d = y.shape[1]
    n_out = y.shape[0] if n_out is None else n_out
    row = pl.BlockSpec((TM_ROW, d), lambda i: (i, 0))
    vec = pl.BlockSpec((1, d), lambda i: (0, 0))
    mod = pl.BlockSpec((None, N_MOD, d), lambda i: (seg(i), 0, 0))
    n_lat_tiles = x[0].shape[0] // TM_ROW if split else None
    in_specs = (_split_specs(n_lat_tiles, d) if split else [row]) + [row, mod, vec, vec]
    args = (list(x) if split else [x]) + [y, mods_gate, g.reshape(1, d), b.reshape(1, d)]
    out_shape = [jax.ShapeDtypeStruct((n_out, d), F32)]
    out_specs = [row]
    if i_shift is not None:
        in_specs.append(mod)
        args.append(mods_next)
        out_shape.append(jax.ShapeDtypeStruct((n_out, d), BF16))
        out_specs.append(row)
    out = pl.pallas_call(
        functools.partial(_norm_kernel, n_lat_tiles=n_lat_tiles, alpha=alpha, coef=coef, i_gate=i_gate,
                          i_shift=i_shift, i_scale=i_scale),
        name="norm",
        out_shape=out_shape,
        grid=(n_rows // TM_ROW,),
        in_specs=in_specs,
        out_specs=out_specs,
        compiler_params=_params("parallel"),
    )(*args)
    return out if i_shift is not None else (out[0], None)


def _ffn_in_kernel(h_ref, wg_ref, wu_ref, o_ref, wgb_ref, wub_ref):
    _cast_weight_once(wg_ref, wgb_ref)
    _cast_weight_once(wu_ref, wub_ref)
    h = h_ref[...]
    g = jnp.dot(h, wgb_ref[...], preferred_element_type=F32)
    u = jnp.dot(h, wub_ref[...], preferred_element_type=F32)
    o_ref[...] = (g * _sigmoid(g) * u).astype(o_ref.dtype)


def _ffn_in(h, w, layer, n_rows):
    n, d = h.shape
    dff = w.shape[2] // 2
    tn = _col_tile(dff)
    return pl.pallas_call(
        _ffn_in_kernel,
        name="ffn_in",
        out_shape=jax.ShapeDtypeStruct((n, dff), BF16),
        grid=(dff // tn, n_rows // TM),
        in_specs=[pl.BlockSpec((TM, d), lambda j, i: (i, 0)),
                  pl.BlockSpec((None, d, tn), lambda j, i: (layer, 0, j)),
                  pl.BlockSpec((None, d, tn), lambda j, i: (layer, 0, j + dff // tn))],
        out_specs=pl.BlockSpec((TM, tn), lambda j, i: (i, j)),
        scratch_shapes=[pltpu.VMEM((d, tn), BF16), pltpu.VMEM((d, tn), BF16)],
        compiler_params=_params("arbitrary", "arbitrary"),
    )(h, w, w)


def _mm_kernel(a_ref, w_ref, o_ref, wb_ref):
    _cast_weight_once(w_ref, wb_ref)
    o_ref[...] = jnp.dot(a_ref[...], wb_ref[...], preferred_element_type=F32).astype(o_ref.dtype)


def _mm(a, w, layer, col_off, n_cols, out_dtype, n_rows, name):
    n, k = a.shape
    tn = _col_tile(n_cols, col_off, _wide_tile(k, jnp.dtype(out_dtype).itemsize))
    joff = col_off // tn
    return pl.pallas_call(
        _mm_kernel,
        name=name,
        out_shape=jax.ShapeDtypeStruct((n, n_cols), out_dtype),
        grid=(n_cols // tn, n_rows // TM),
        in_specs=[pl.BlockSpec((TM, k), lambda j, i: (i, 0)),
                  pl.BlockSpec((None, k, tn), lambda j, i: (layer, 0, j + joff))],
        out_specs=pl.BlockSpec((TM, tn), lambda j, i: (i, j)),
        scratch_shapes=[pltpu.VMEM((k, tn), BF16)],
        compiler_params=_params("arbitrary", "arbitrary"),
    )(a, w)


def _mm_rope_kernel(a_ref, w_ref, cos_ref, sin_ref, o_ref, orot_ref, wb_ref):
    _cast_weight_once(w_ref, wb_ref)
    acc = jnp.dot(a_ref[...], wb_ref[...], preferred_element_type=F32)
    o_ref[...] = acc.astype(o_ref.dtype)
    tn = acc.shape[1]
    reps = tn // HEAD_DIM
    cos = jnp.concatenate([cos_ref[...]] * reps, axis=1)
    sin = jnp.concatenate([sin_ref[...]] * reps, axis=1)
    lane = lax.broadcasted_iota(jnp.int32, acc.shape, 1)
    quarter = HEAD_DIM // 4
    first = (lane % (2 * quarter)) < quarter
    partner = jnp.where(first, pltpu.roll(acc, tn - quarter, 1), pltpu.roll(acc, quarter, 1))
    orot_ref[...] = (acc * cos + partner * sin).astype(orot_ref.dtype)


def _mm_rope(a, w, layer, col_off, n_cols, cos, sin, tiles_per_batch, n_rows):
    n, k = a.shape
    tn = _col_tile(n_cols, col_off, _wide_tile(k, 2 * jnp.dtype(BF16).itemsize))
    joff = col_off // tn
    out = jax.ShapeDtypeStruct((n, n_cols), BF16)
    tab = pl.BlockSpec((TM, HEAD_DIM), lambda j, i: (i % tiles_per_batch, 0))
    blk = pl.BlockSpec((TM, tn), lambda j, i: (i, j))
    return pl.pallas_call(
        _mm_rope_kernel,
        name="proj_rope",
        out_shape=[out, out],
        grid=(n_cols // tn, n_rows // TM),
        in_specs=[pl.BlockSpec((TM, k), lambda j, i: (i, 0)),
                  pl.BlockSpec((None, k, tn), lambda j, i: (layer, 0, j + joff)),
                  tab, tab],
        out_specs=[blk, blk],
        scratch_shapes=[pltpu.VMEM((k, tn), BF16)],
        compiler_params=_params("arbitrary", "arbitrary"),
    )(a, w, cos, sin)


def _merge_kernel(h_ref, oa_ref, ob_ref, oc_ref, wga_ref, wgb_ref, wgc_ref, wa_ref, wb_ref, wc_ref, o_ref,
                  *scratch):
    weights = (wga_ref, wgb_ref, wgc_ref, wa_ref, wb_ref, wc_ref)
    for w_ref, wb16_ref in zip(weights, scratch):
        _cast_weight_once(w_ref, wb16_ref)
    gate_w, proj_w = scratch[:N_BRANCH], scratch[N_BRANCH:]
    h = h_ref[...]
    t = None
    for branch_ref, wg, wp in zip((oa_ref, ob_ref, oc_ref), gate_w, proj_w):
        gate = _sigmoid(jnp.dot(h, wg[...], preferred_element_type=F32))
        term = gate * jnp.dot(branch_ref[...], wp[...], preferred_element_type=F32)
        t = term if t is None else t + term
    o_ref[...] = t.astype(o_ref.dtype)


def _merge(h, o_a, o_b, o_c, w_in, gate_off, w_pa, w_pb, w_pc, layer, n_rows):
    n, d = h.shape
    tn = _col_tile(d, gate_off, MERGE_TN)

    def act(o):
        return pl.BlockSpec((TM, o.shape[1]), lambda j, i: (i, 0))

    def proj(w):
        return pl.BlockSpec((None, w.shape[1], tn), lambda j, i: (layer, 0, j))

    def gate(branch):
        joff = (gate_off + branch * d) // tn
        return pl.BlockSpec((None, d, tn), lambda j, i: (layer, 0, j + joff), pipeline_mode=pl.Buffered(1))

    return pl.pallas_call(
        _merge_kernel,
        name="merge",
        out_shape=jax.ShapeDtypeStruct((n, d), BF16),
        grid=(d // tn, n_rows // TM),
        in_specs=[act(h), act(o_a), act(o_b), act(o_c), gate(0), gate(1), gate(2), proj(w_pa), proj(w_pb), proj(w_pc)],
        out_specs=pl.BlockSpec((TM, tn), lambda j, i: (i, j)),
        scratch_shapes=([pltpu.VMEM((d, tn), BF16)] * N_BRANCH
                        + [pltpu.VMEM((w.shape[1], tn), BF16) for w in (w_pa, w_pb, w_pc)]),
        compiler_params=_params("arbitrary", "arbitrary"),
    )(h, o_a, o_b, o_c, w_in, w_in, w_in, w_pa, w_pb, w_pc)


def _attn_block_geometry(kind):
    half = WIN_H // 2
    if kind == 0:
        return 0, [max(i - half, 0) for i in range(ATT_ROWS)]
    if kind == 1:
        return -half, list(range(ATT_ROWS))
    return -(ATT_KROWS - ATT_ROWS), [min(half + i, ATT_KROWS - WIN_H) for i in range(ATT_ROWS)]


def _attn_sub_key_offset(kind, sub):
    _, first_key_row = _attn_block_geometry(kind)
    return min(min(first_key_row[sub * ATT_SUB:(sub + 1) * ATT_SUB]), ATT_KROWS - ATT_SUB_KROWS)


def _rpb_table_kernel(rpb_ref, o_ref):
    shape = (GRID_W, 2 * GRID_W)
    cq = lax.broadcasted_iota(jnp.int32, shape, 0)
    lane = lax.broadcasted_iota(jnp.int32, shape, 1)
    ck = lane % GRID_W
    second = lane >= GRID_W
    start = jnp.clip(cq - WIN_W // 2, 0, GRID_W - WIN_W)
    in_win = (ck >= start) & (ck < start + WIN_W)
    neg = jnp.full(shape, NEG_INF, F32)
    n_bias_rows = 2 * WIN_H - 1

    def half_tile(a, lane_off):
        row = jnp.broadcast_to(rpb_ref[a:a + 1, :], shape)
        return pltpu.roll(row, (lane_off - (WIN_W - 1)) % LANES, 1, stride=1, stride_axis=0)

    lo_half = [half_tile(a, 0) for a in range(n_bias_rows)]
    hi_half = [half_tile(a, GRID_W) for a in range(n_bias_rows)]
    for kind in range(3):
        diff, first_key_row = _attn_block_geometry(kind)
        for i in range(ATT_ROWS):
            lo = first_key_row[i]
            sub, isub = divmod(i, ATT_SUB)
            koff = _attn_sub_key_offset(kind, sub)
            for pair in range(ATT_SUB_KROWS // 2):
                j0, j1 = koff + 2 * pair, koff + 2 * pair + 1
                t0 = lo_half[j0 - i + diff + WIN_H - 1] if lo <= j0 < lo + WIN_H else neg
                t1 = hi_half[j1 - i + diff + WIN_H - 1] if lo <= j1 < lo + WIN_H else neg
                tile = jnp.where(in_win, jnp.where(second, t1, t0), neg)
                o_ref[kind, sub, isub * GRID_W:(isub + 1) * GRID_W,
                      pair * 2 * GRID_W:(pair + 1) * 2 * GRID_W] = tile


def _rpb_table(rpb):
    heads, nr, nc = rpb.shape
    rpb_pad = jnp.zeros((heads, 2 * WIN_H, LANES), F32).at[:, :nr, :nc].set(rpb)
    shape = (3, ATT_ROWS // ATT_SUB, ATT_SUB * GRID_W, ATT_SUB_KROWS * GRID_W)
    return pl.pallas_call(
        _rpb_table_kernel,
        name="rpb_table",
        out_shape=jax.ShapeDtypeStruct((heads,) + shape, F32),
        grid=(heads,),
        in_specs=[pl.BlockSpec((None, 2 * WIN_H, LANES), lambda h: (h, 0, 0))],
        out_specs=pl.BlockSpec((None,) + shape, lambda h: (h, 0, 0, 0, 0)),
        compiler_params=_params("parallel"),
    )(rpb_pad)


def _attn_kernel(q_ref, qr_ref, kr_ref, v_ref, kc_ref, vc_ref, bias_ref, o_ref, *, n_grid_rows, n_blocks, scale):
    rb = pl.program_id(2)
    kr0 = jnp.clip(rb * ATT_ROWS - WIN_H // 2, 0, n_grid_rows - ATT_KROWS)
    kc = kc_ref[...]
    vc = vc_ref[...]
    for sub in range(ATT_ROWS // ATT_SUB):
        offs = [_attn_sub_key_offset(kind, sub) for kind in range(3)]
        koff = jnp.where(rb == 0, offs[0], jnp.where(rb == n_blocks - 1, offs[2], offs[1]))
        keys = pl.ds(pl.multiple_of((kr0 + koff) * GRID_W, GRID_W), ATT_SUB_KROWS * GRID_W)
        rows = pl.ds(sub * ATT_SUB * GRID_W, ATT_SUB * GRID_W)
        s = lax.dot_general(qr_ref[rows, :], kr_ref[keys, :], NT_DIMS, preferred_element_type=F32) * scale
        s += bias_ref[sub]
        sc = lax.dot_general(q_ref[rows, :], kc, NT_DIMS, preferred_element_type=F32) * scale
        m = jnp.maximum(jnp.max(s, axis=-1, keepdims=True), jnp.max(sc, axis=-1, keepdims=True))
        p = jnp.exp(s - m)
        pc = jnp.exp(sc - m)
        denom = jnp.sum(p, axis=-1, keepdims=True) + jnp.sum(pc, axis=-1, keepdims=True)
        o = jnp.dot(p.astype(BF16), v_ref[keys, :], preferred_element_type=F32)
        o += jnp.dot(pc.astype(BF16), vc, preferred_element_type=F32)
        o_ref[rows, :] = (o / denom).astype(o_ref.dtype)


def _attention(q, q_rot, k, k_rot, v, bias, n_batch, seq, ctx_len):
    n, width = q.shape
    heads = width // HEAD_DIM
    n_grid_rows = seq // GRID_W
    tq = ATT_ROWS * GRID_W
    blocks = seq // tq
    ctx0 = n_batch * seq // ctx_len
    qspec = pl.BlockSpec((tq, HEAD_DIM), lambda b, h, r: (b * blocks + r, h))
    kspec = pl.BlockSpec((seq, HEAD_DIM), lambda b, h, r: (b, h))
    cspec = pl.BlockSpec((ctx_len, HEAD_DIM), lambda b, h, r: (ctx0 + b, h))

    def kind(r):
        return jnp.where(r == 0, 0, jnp.where(r == blocks - 1, 2, 1))

    return pl.pallas_call(
        functools.partial(_attn_kernel, n_grid_rows=n_grid_rows, n_blocks=blocks, scale=HEAD_DIM ** -0.5),
        name="attn",
        out_shape=jax.ShapeDtypeStruct((n, width), BF16),
        grid=(n_batch, heads, blocks),
        in_specs=[qspec, qspec, kspec, kspec, cspec, cspec,
                  pl.BlockSpec((None, None) + bias.shape[2:], lambda b, h, r: (h, kind(r), 0, 0, 0))],
        out_specs=qspec,
        compiler_params=_params("parallel", "parallel", "arbitrary"),
    )(q, q_rot, k_rot, v, k, v, bias)


def _ctx_attn_kernel(q_ref, k_ref, v_ref, prev_ref, o_ref, *, scale):
    del prev_ref
    s = lax.dot_general(q_ref[...], k_ref[...], NT_DIMS, preferred_element_type=F32) * scale
    m = jnp.max(s, axis=-1, keepdims=True)
    p = jnp.exp(s - m)
    denom = jnp.sum(p, axis=-1, keepdims=True)
    o = jnp.dot(p.astype(BF16), v_ref[...], preferred_element_type=F32)
    o_ref[...] = (o / denom).astype(o_ref.dtype)


def _ctx_attention(q, k, v, o_att, n_batch, seq, ctx_len):
    n, width = q.shape
    heads = width // HEAD_DIM
    ctx0 = n_batch * seq // ctx_len
    spec = pl.BlockSpec((ctx_len, HEAD_DIM), lambda b, h: (ctx0 + b, h))
    return pl.pallas_call(
        functools.partial(_ctx_attn_kernel, scale=HEAD_DIM ** -0.5),
        name="ctx_attn",
        out_shape=jax.ShapeDtypeStruct((n, width), BF16),
        grid=(n_batch, heads),
        in_specs=[spec, spec, spec, pl.BlockSpec(memory_space=pl.ANY)],
        out_specs=spec,
        input_output_aliases={3: 0},
        compiler_params=_params("parallel", "parallel"),
    )(q, k, v, o_att)


def _gmlp_kernel(z_ref, g_ref, b_ref, ws_ref, bs_ref, o_ref, *, width, chunks):
    z = jax.nn.gelu(z_ref[...])
    u = z[:, :width]
    v = (_ln(z[:, width:]) * g_ref[...] + b_ref[...]).astype(BF16)
    for c in range(chunks):
        rows = slice(c * CHUNK, (c + 1) * CHUNK)
        for g in range(width // GMLP_GROUP_DIM):
            cols = slice(g * GMLP_GROUP_DIM, (g + 1) * GMLP_GROUP_DIM)
            mixed = jnp.dot(ws_ref[g], v[rows, cols], preferred_element_type=F32) + bs_ref[:, cols]
            o_ref[rows, cols] = (u[rows, cols] * mixed).astype(o_ref.dtype)


def _gmlp(z, ln_g, ln_b, w_s, b_s, n_rows):
    n, two_w = z.shape
    width = two_w // 2
    groups = width // GMLP_GROUP_DIM
    chunks = TM // CHUNK
    bs = jnp.repeat(b_s.T, GMLP_GROUP_DIM, axis=1)
    vec = pl.BlockSpec((1, width), lambda i: (0, 0))
    return pl.pallas_call(
        functools.partial(_gmlp_kernel, width=width, chunks=chunks),
        name="gmlp",
        out_shape=jax.ShapeDtypeStruct((n, width), BF16),
        grid=(n_rows // TM,),
        in_specs=[pl.BlockSpec((TM, two_w), lambda i: (i, 0)), vec, vec,
                  pl.BlockSpec((groups, CHUNK, CHUNK), lambda i: (0, 0, 0)),
                  pl.BlockSpec((CHUNK, width), lambda i: (0, 0))],
        out_specs=pl.BlockSpec((TM, width), lambda i: (i, 0)),
        compiler_params=_params("parallel"),
    )(z, ln_g.reshape(1, width), ln_b.reshape(1, width), w_s.astype(BF16), bs)


def _s5_gen_kernel(lr_ref, li_ref, ldt_ref, bre_ref, bim_ref, cre_ref, cim_ref, m_ref, wst_ref, et_ref, a_ref):
    hi = lax.Precision.HIGHEST
    nb = SSM_BLOCK
    lane = lax.broadcasted_iota(jnp.int32, (SSM_GROUP, SSM_ROW), 1)
    state_lane = lax.broadcasted_iota(jnp.int32, (SSM_GROUP, 2 * SSM_STATE), 1)
    in_group = [state_lane < SSM_STATE, state_lane >= SSM_STATE]
    m_rows = [[None] * nb for _ in range(2)]
    for d in range(2):
        lr = lr_ref[d:d + 1, :]
        li = li_ref[d:d + 1, :]
        dt = jnp.exp(ldt_ref[d:d + 1, :])
        mag = jnp.exp(lr * dt)
        ar = mag * jnp.cos(li * dt)
        ai = mag * jnp.sin(li * dt)
        den = lr * lr + li * li
        fr = ((ar - 1.0) * lr + ai * li) / den
        fi = (ai * lr - (ar - 1.0) * li) / den
        br = bre_ref[d]
        bi = bim_ref[d]
        bbr = fr * br - fi * bi
        bbi = fr * bi + fi * br
        cr = cre_ref[d]
        ci = cim_ref[d]
        pr = [jnp.ones_like(ar)]
        pi = [jnp.zeros_like(ai)]
        for _ in range(nb):
            pr.append(pr[-1] * ar - pi[-1] * ai)
            pi.append(pr[-2] * ai + pi[-1] * ar)
        a_ref[2 * d] = pr[nb]
        a_ref[2 * d + 1] = pi[nb]
        e_state = [(t + 1) if d == 0 else (nb - t) for t in range(nb)]
        e_in = [(nb - 1 - s) if d == 0 else s for s in range(nb)]
        e_lag = [e if d == 0 else (nb - 1 - e) for e in range(nb)]
        for s in range(nb):
            wr, wi = pr[e_in[s]], pi[e_in[s]]
            er, ei = pr[e_state[s]], pi[e_state[s]]
            parts = (wr * bbr - wi * bbi, wr * bbi + wi * bbr)
            outs = (cr * er - ci * ei, -(cr * ei + ci * er))
            for g in range(2):
                rows = slice(g * SSM_ROW + s * SSM_GROUP, g * SSM_ROW + (s + 1) * SSM_GROUP)
                for part in range(2):
                    wst_ref[2 * d + part, rows, :] = jnp.where(in_group[g], parts[part], 0.0).astype(wst_ref.dtype)
                    et_ref[2 * d + part, rows, :] = jnp.where(in_group[g], outs[part], 0.0).astype(et_ref.dtype)
        lag_r = jnp.concatenate([cr * pr[e] - ci * pi[e] for e in e_lag], axis=0)
        lag_i = jnp.concatenate([-(cr * pi[e] + ci * pr[e]) for e in e_lag], axis=0)
        for g in range(2):
            gbr = jnp.where(in_group[g], bbr, 0.0)
            gbi = jnp.where(in_group[g], bbi, 0.0)
            kbase = (lax.dot_general(gbr, lag_r, NT_DIMS, precision=hi, preferred_element_type=F32)
                     + lax.dot_general(gbi, lag_i, NT_DIMS, precision=hi, preferred_element_type=F32))
            for s in range(nb):
                if d == 0:
                    sh = s * SSM_GROUP
                    m_rows[g][s] = kbase if sh == 0 else jnp.where(lane >= sh, pltpu.roll(kbase, sh, 1), 0.0)
                else:
                    sh = (nb - 1 - s) * SSM_GROUP
                    row = kbase if sh == 0 else jnp.where(lane < SSM_ROW - sh, pltpu.roll(kbase, SSM_ROW - sh, 1), 0.0)
                    m_rows[g][s] = m_rows[g][s] + row
    zeros = jnp.zeros((SSM_GROUP, SSM_ROW), m_ref.dtype)
    for g in range(2):
        for s in range(nb):
            rows = slice(g * SSM_ROW + s * SSM_GROUP, g * SSM_ROW + (s + 1) * SSM_GROUP)
            m_ref[rows, g * SSM_ROW:(g + 1) * SSM_ROW] = m_rows[g][s].astype(m_ref.dtype)
            m_ref[rows, (1 - g) * SSM_ROW:(2 - g) * SSM_ROW] = zeros


def _s5_gen(lam_re, lam_im, log_dt, b_re, b_im, c_re, c_im):
    _, groups, p = lam_re.shape
    pairs = groups // 2

    def vec_pairs(x):
        return jnp.swapaxes(x.reshape(2, pairs, 2 * p), 0, 1)

    def mat_pairs(x):
        x = x.reshape(2, pairs, 2, SSM_GROUP, p)
        return jnp.transpose(x, (1, 0, 3, 2, 4)).reshape(pairs, 2, SSM_GROUP, 2 * p)

    ldt = jnp.broadcast_to(log_dt[:, :, None], lam_re.shape)
    vec = pl.BlockSpec((None, 2, 2 * p), lambda g: (g, 0, 0))
    mat = pl.BlockSpec((None, 2, SSM_GROUP, 2 * p), lambda g: (g, 0, 0, 0))
    return pl.pallas_call(
        _s5_gen_kernel,
        name="s5_gen",
        out_shape=[jax.ShapeDtypeStruct((pairs, 2 * SSM_ROW, 2 * SSM_ROW), BF16),
                   jax.ShapeDtypeStruct((pairs, 4, 2 * SSM_ROW, 2 * p), BF16),
                   jax.ShapeDtypeStruct((pairs, 4, 2 * SSM_ROW, 2 * p), BF16),
                   jax.ShapeDtypeStruct((pairs, 4, 1, 2 * p), F32)],
        grid=(pairs,),
        in_specs=[vec, vec, vec, mat, mat, mat, mat],
        out_specs=[pl.BlockSpec((None, 2 * SSM_ROW, 2 * SSM_ROW), lambda g: (g, 0, 0)),
                   pl.BlockSpec((None, 4, 2 * SSM_ROW, 2 * p), lambda g: (g, 0, 0, 0)),
                   pl.BlockSpec((None, 4, 2 * SSM_ROW, 2 * p), lambda g: (g, 0, 0, 0)),
                   pl.BlockSpec((None, 4, 1, 2 * p), lambda g: (g, 0, 0, 0))],
        compiler_params=_params("parallel"),
    )(vec_pairs(lam_re), vec_pairs(lam_im), vec_pairs(ldt),
      mat_pairs(jnp.swapaxes(b_re, 2, 3)), mat_pairs(jnp.swapaxes(b_im, 2, 3)), mat_pairs(c_re), mat_pairs(c_im))


def _slot_mask(rows):
    return lax.broadcasted_iota(jnp.int32, (rows, LANES), 1) // SSM_GROUP


def _s5_state_kernel(x_ref, wst_ref, u_ref, *state_refs):
    n_blk = u_ref.shape[0]
    xs = [x_ref[pl.ds(s, n_blk, stride=SSM_BLOCK), :] for s in range(SSM_BLOCK)]
    slot = _slot_mask(n_blk)
    per_col = LANES // SSM_GROUP
    pieces = []
    for gl in range(SSM_CHUNK_GROUPS):
        for col in range(SSM_BLOCK // per_col):
            acc = jnp.zeros_like(xs[0])
            for k in range(per_col):
                moved = pltpu.roll(xs[col * per_col + k], ((k - gl) * SSM_GROUP) % LANES, 1)
                acc = jnp.where(slot == k, moved, acc)
            pieces.append(acc)
    cols_per_pair = 2 * SSM_ROW // LANES
    for pair in range(SSM_CHUNK_GROUPS // 2):
        u = jnp.concatenate(pieces[pair * cols_per_pair:(pair + 1) * cols_per_pair], axis=1).astype(BF16)
        u_ref[:, pair * 2 * SSM_ROW:(pair + 1) * 2 * SSM_ROW] = u
        for k, o_ref in enumerate(state_refs):
            o_ref[:, pair * 2 * SSM_STATE:(pair + 1) * 2 * SSM_STATE] = jnp.dot(
                u, wst_ref[pair, k], preferred_element_type=F32)


def _s5_row_tile(n_rows):
    return n_rows // 2 if n_rows % 32 == 0 else n_rows


def _s5_states(us, wst_pair, groups):
    n_rows = us.shape[0] // SSM_BLOCK
    chunks = groups // SSM_CHUNK_GROUPS
    pairs = SSM_CHUNK_GROUPS // 2
    rt = _s5_row_tile(n_rows)
    state = jax.ShapeDtypeStruct((n_rows, groups * SSM_STATE), F32)
    sspec = pl.BlockSpec((rt, SSM_CHUNK_GROUPS * SSM_STATE), lambda ch, i: (i, ch))
    uspec = pl.BlockSpec((rt, SSM_CHUNK_GROUPS * SSM_ROW), lambda ch, i: (i, ch))
    return pl.pallas_call(
        _s5_state_kernel,
        name="s5_states",
        out_shape=[jax.ShapeDtypeStruct((n_rows, groups * SSM_ROW), BF16)] + [state] * 4,
        grid=(chunks, n_rows // rt),
        in_specs=[pl.BlockSpec((rt * SSM_BLOCK, LANES), lambda ch, i: (i, ch)),
                  pl.BlockSpec((pairs,) + wst_pair.shape[1:], lambda ch, i: (ch, 0, 0, 0))],
        out_specs=[uspec] + [sspec] * 4,
        compiler_params=_params("parallel", "parallel"),
    )(us, wst_pair)


def _s5_scan_kernel(sfr_ref, sfi_ref, sbr_ref, sbi_ref, a_ref, hfr_ref, hfi_ref, hbr_ref, hbi_ref,
                    *, n_batch, n_lat_blocks, n_ctx_blocks):
    afr, afi, abr, abi = (a_ref[k:k + 1, :] for k in range(4))
    zero = jnp.zeros_like(afr)
    nl, nc = n_lat_blocks, n_ctx_blocks

    def step(t, carry):
        new = []
        in_ctx = t < nc
        for b in range(n_batch):
            lat0, ctx0 = b * nl, n_batch * nl + b * nc
            fr, fi, br, bi = carry[4 * b:4 * b + 4]
            rf = pl.ds(jnp.where(in_ctx, ctx0 + t, lat0 + t - nc), 1)
            rb = pl.ds(jnp.where(in_ctx, ctx0 + nc - 1 - t, lat0 + nl - 1 - (t - nc)), 1)
            hfr_ref[rf, :] = fr
            hfi_ref[rf, :] = fi
            hbr_ref[rb, :] = br
            hbi_ref[rb, :] = bi
            new += [afr * fr - afi * fi + sfr_ref[rf, :], afr * fi + afi * fr + sfi_ref[rf, :],
                    abr * br - abi * bi + sbr_ref[rb, :], abr * bi + abi * br + sbi_ref[rb, :]]
        return tuple(new)

    lax.fori_loop(0, nl + nc, step, (zero,) * (4 * n_batch))


def _s5_scan(s_parts, a_rows, n_batch, n_lat_blocks, n_ctx_blocks):
    rows, lanes = s_parts[0].shape
    tl = min(lanes, 4 * LANES)
    spec = pl.BlockSpec((rows, tl), lambda j: (0, j))
    return pl.pallas_call(
        functools.partial(_s5_scan_kernel, n_batch=n_batch, n_lat_blocks=n_lat_blocks, n_ctx_blocks=n_ctx_blocks),
        name="s5_scan",
        out_shape=[jax.ShapeDtypeStruct((rows, lanes), F32)] * 4,
        grid=(lanes // tl,),
        in_specs=[spec] * 4 + [pl.BlockSpec((4, tl), lambda j: (0, j))],
        out_specs=[spec] * 4,
        compiler_params=_params("parallel"),
    )(*s_parts, a_rows)


def _s5_out_kernel(u_ref, m_ref, et_ref, hfr_ref, hfi_ref, hbr_ref, hbi_ref, y_ref):
    ys = []
    for pair in range(SSM_CHUNK_GROUPS // 2):
        y = jnp.dot(u_ref[:, pair * 2 * SSM_ROW:(pair + 1) * 2 * SSM_ROW], m_ref[pair], preferred_element_type=F32)
        for k, h_ref in enumerate((hfr_ref, hfi_ref, hbr_ref, hbi_ref)):
            h = h_ref[:, pair * 2 * SSM_STATE:(pair + 1) * 2 * SSM_STATE].astype(BF16)
            y += lax.dot_general(h, et_ref[pair, k], NT_DIMS, preferred_element_type=F32)
        ys.append(y)
    n_blk = ys[0].shape[0]
    slot = _slot_mask(n_blk)
    per_col = LANES // SSM_GROUP
    for s in range(SSM_BLOCK):
        col, k = divmod(s, per_col)
        acc = jnp.zeros((n_blk, LANES), F32)
        for gl in range(SSM_CHUNK_GROUPS):
            pair, j = divmod(gl, 2)
            lo = j * SSM_ROW + col * LANES
            moved = pltpu.roll(ys[pair][:, lo:lo + LANES], ((gl - k) * SSM_GROUP) % LANES, 1)
            acc = jnp.where(slot == gl, moved, acc)
        y_ref[pl.ds(s, n_blk, stride=SSM_BLOCK), :] = acc


def _s5_outputs(u, m_pair, et_pair, h_parts, groups):
    n_rows = u.shape[0]
    chunks = groups // SSM_CHUNK_GROUPS
    pairs = SSM_CHUNK_GROUPS // 2
    rt = _s5_row_tile(n_rows)
    hspec = pl.BlockSpec((rt, SSM_CHUNK_GROUPS * SSM_STATE), lambda ch, i: (i, ch))
    return pl.pallas_call(
        _s5_out_kernel,
        name="s5_out",
        out_shape=jax.ShapeDtypeStruct((n_rows * SSM_BLOCK, groups * SSM_GROUP), F32),
        grid=(chunks, n_rows // rt),
        in_specs=[pl.BlockSpec((rt, SSM_CHUNK_GROUPS * SSM_ROW), lambda ch, i: (i, ch)),
                  pl.BlockSpec((pairs,) + m_pair.shape[1:], lambda ch, i: (ch, 0, 0)),
                  pl.BlockSpec((pairs,) + et_pair.shape[1:], lambda ch, i: (ch, 0, 0, 0)),
                  hspec, hspec, hspec, hspec],
        out_specs=pl.BlockSpec((rt * SSM_BLOCK, LANES), lambda ch, i: (i, ch)),
        compiler_params=_params("parallel", "parallel"),
    )(u, m_pair, et_pair, *h_parts)


def _s5_glu_kernel(ys_ref, us_ref, d_ref, w_ref, b_ref, o_ref, wb_ref):
    @pl.when(pl.program_id(0) == 0)
    def _():
        wb_ref[...] = w_ref[...].astype(BF16)

    y = jax.nn.gelu(ys_ref[...] + d_ref[...] * us_ref[...])
    gate = _sigmoid(jnp.dot(y.astype(BF16), wb_ref[...], preferred_element_type=F32) + b_ref[...])
    o_ref[...] = (y * gate).astype(o_ref.dtype)


def _s5_glu(y_s, us, d_skip, w_glu, layer, b_glu, n_rows):
    n, w = us.shape
    row = pl.BlockSpec((TM, w), lambda i: (i, 0))
    vec = pl.BlockSpec((1, w), lambda i: (0, 0))
    return pl.pallas_call(
        _s5_glu_kernel,
        name="s5_glu",
        out_shape=jax.ShapeDtypeStruct((n, w), BF16),
        grid=(n_rows // TM,),
        in_specs=[row, row, vec, pl.BlockSpec((None, w, w), lambda i: (layer, 0, 0)), vec],
        out_specs=row,
        scratch_shapes=[pltpu.VMEM((w, w), BF16)],
        compiler_params=_params("arbitrary"),
    )(y_s, us, d_skip.reshape(1, w), w_glu, b_glu.reshape(1, w))


def _s5(us, n_batch, seq, ctx_len, lam_re, lam_im, log_dt, b_re, b_im, c_re, c_im):
    groups = us.shape[1] // SSM_GROUP
    m_pair, wst_pair, et_pair, a_pair = _s5_gen(lam_re, lam_im, log_dt, b_re, b_im, c_re, c_im)
    a_rows = jnp.transpose(a_pair[:, :, 0, :], (1, 0, 2)).reshape(4, groups * SSM_STATE)
    u, *s_parts = _s5_states(us, wst_pair, groups)
    h_parts = _s5_scan(s_parts, a_rows, n_batch, seq // SSM_BLOCK, ctx_len // SSM_BLOCK)
    return _s5_outputs(u, m_pair, et_pair, h_parts, groups)


def _rope_tables(seq):
    t = jnp.arange(seq)
    quarter = HEAD_DIM // 4
    inv = ROPE_BASE ** (-jnp.arange(quarter, dtype=F32) / quarter)
    ang_r = (t // GRID_W).astype(F32)[:, None] * inv[None, :]
    ang_c = (t % GRID_W).astype(F32)[:, None] * inv[None, :]
    cos = jnp.concatenate([jnp.cos(ang_r)] * 2 + [jnp.cos(ang_c)] * 2, axis=1)
    sin = jnp.concatenate([-jnp.sin(ang_r), jnp.sin(ang_r), -jnp.sin(ang_c), jnp.sin(ang_c)], axis=1)
    return cos, sin


def kernel(x, c, ctx, c_ctx, w_mod, b_mod, ln_g, ln_b, ffn1_w_in, ffn1_w_out, ffn2_w_in, ffn2_w_out, w_in, rpb, gm_ln_g, gm_ln_b, gm_ws, gm_bs, ssm_lam_re, ssm_lam_im, ssm_log_dt, ssm_b_re, ssm_b_im, ssm_c_re, ssm_c_im, ssm_d, ssm_w_glu, ssm_b_glu, w_pa, w_pb, w_pc, w_o):
    n_batch, seq, d = x.shape
    ctx_len = ctx.shape[1]
    depth = w_mod.shape[0]
    att_w = w_pb.shape[1]
    gm_w = w_pa.shape[1]
    ssm_w = w_pc.shape[1]
    assert ssm_w % LANES == 0 and seq % SSM_BLOCK == 0 and ctx_len % SSM_BLOCK == 0
    q_off, k_off, v_off = 0, att_w, 2 * att_w
    gm_off = 3 * att_w
    ssm_off = gm_off + 2 * gm_w
    gate_off = ssm_off + ssm_w
    alpha = (2.0 * depth) ** 0.25
    n_lat = n_batch * seq
    n_all = n_lat + n_batch * ctx_len
    assert seq % TM == 0 and (n_batch * ctx_len) % TM == 0 and seq % (ATT_ROWS * GRID_W) == 0
    assert seq // GRID_W >= ATT_KROWS and ctx_len % CHUNK == 0 and n_lat % ctx_len == 0
    assert n_batch + 1 <= MOD_ROWS and rpb.shape[2:] == (2 * WIN_H - 1, 2 * WIN_W - 1)

    seg = _seg_map(seq // TM_ROW, n_batch)
    tiles_per_batch = seq // TM

    x_lat, x_ctx = x.reshape(n_lat, d), ctx.reshape(n_batch * ctx_len, d)
    c_all = jnp.zeros((MOD_ROWS, d), F32).at[:n_batch].set(c).at[n_batch].set(c_ctx)
    mods = _mods(c_all, w_mod, b_mod)[:, :n_batch + 1].reshape(depth, n_batch + 1, N_MOD, d)
    cos, sin = _rope_tables(seq)

    xs = (x_lat, x_ctx)
    h = _modulate(x_lat, x_ctx, mods[0], 0, 1, seg)
    for l in range(depth):
        last = l == depth - 1
        n_mix = n_lat if last else n_all

        y = _mm(_ffn_in(h, ffn1_w_in, l, n_all), ffn1_w_out, l, 0, d, BF16, n_all, "ffn_out")
        xs, h = _norm(xs, y, mods[l], 2, 0.5, ln_g[l, 0], ln_b[l, 0], alpha, seg, n_all,
                      mods_next=mods[l], i_shift=3, i_scale=4)

        q, q_rot = _mm_rope(h, w_in, l, q_off, att_w, cos, sin, tiles_per_batch, n_mix)
        k, k_rot = _mm_rope(h, w_in, l, k_off, att_w, cos, sin, tiles_per_batch, n_all)
        v = _mm(h, w_in, l, v_off, att_w, BF16, n_all, "proj_v")
        z = _mm(h, w_in, l, gm_off, 2 * gm_w, F32, n_mix, "proj_z")
        us = _mm(h, w_in, l, ssm_off, ssm_w, F32, n_all, "proj_us")

        o_att = _attention(q, q_rot, k, k_rot, v, _rpb_table(rpb[l]), n_batch, seq, ctx_len)
        if not last:
            o_att = _ctx_attention(q, k, v, o_att, n_batch, seq, ctx_len)
        o_gm = _gmlp(z, gm_ln_g[l], gm_ln_b[l], gm_ws[l], gm_bs[l], n_mix)
        y_s = _s5(us, n_batch, seq, ctx_len, ssm_lam_re[l], ssm_lam_im[l], ssm_log_dt[l],
                  ssm_b_re[l], ssm_b_im[l], ssm_c_re[l], ssm_c_im[l])
        o_ssm = _s5_glu(y_s, us, ssm_d[l], ssm_w_glu, l, ssm_b_glu[l], n_mix)
        merged = _merge(h, o_gm, o_att, o_ssm, w_in, gate_off, w_pa, w_pb, w_pc, l, n_mix)
        y = _mm(merged, w_o, l, 0, d, BF16, n_mix, "out_proj")
        xs, h = _norm(xs, y, mods[l], 5, 1.0, ln_g[l, 1], ln_b[l, 1], alpha, seg, n_mix,
                      mods_next=mods[l], i_shift=6, i_scale=7)

        y = _mm(_ffn_in(h, ffn2_w_in, l, n_mix), ffn2_w_out, l, 0, d, BF16, n_mix, "ffn_out")
        if last:
            xs, _ = _norm(xs, y, mods[l], 8, 0.5, ln_g[l, 2], ln_b[l, 2], alpha, seg, n_mix, n_out=n_mix)
        else:
            xs, h = _norm(xs, y, mods[l], 8, 0.5, ln_g[l, 2], ln_b[l, 2], alpha, seg, n_mix,
                          mods_next=mods[l + 1], i_shift=0, i_scale=1)
    return xs.reshape(n_batch, seq, d)
```

```python
import functools

import jax
import jax.numpy as jnp
from jax import lax
from jax.experimental import pallas as pl
from jax.experimental.pallas import tpu as pltpu

F32 = jnp.float32
BF16 = jnp.bfloat16

GRID_W = 64
N_SUB = 3
N_MOD = 3 * N_SUB
HEAD_DIM = 128
WIN_H = 8
WIN_W = 16
ROPE_BASE = 10000.0
CHUNK = 128
GMLP_GROUP_DIM = 128
SSM_GROUP = 16
SSM_STATE = 64
N_BRANCH = 3
LN_EPS = 1e-6
NEG_INF = -1e30

LANES = 128
SUBLANES = 8
MIB = 1024 * 1024
V7X_VMEM_BYTES = 64 * MIB
VMEM_LIMIT_BYTES = V7X_VMEM_BYTES - 4 * MIB
VMEM_MARGIN_BYTES = 2 * MIB
SSM_BLOCK = 16
SSM_ROW = SSM_BLOCK * SSM_GROUP
SLOTS = LANES // SSM_GROUP
SSM_CHUNK_GROUPS = SLOTS
MOD_ROWS = SUBLANES

TM = 512
TM_ROW = 256
TN = 512
MERGE_TN = 256
ATT_ROWS = 16
ATT_KROWS = ATT_ROWS + WIN_H
ATT_SUB = 4
ATT_SUB_KROWS = ATT_SUB + WIN_H
NT_DIMS = (((1,), (1,)), ((), ()))


def _params(*sem):
    return pltpu.CompilerParams(dimension_semantics=sem, vmem_limit_bytes=VMEM_LIMIT_BYTES)


def _ln(x):
    mu = jnp.mean(x, axis=-1, keepdims=True)
    xc = x - mu
    var = jnp.mean(xc * xc, axis=-1, keepdims=True)
    return xc * lax.rsqrt(var + LN_EPS)


def _sigmoid(x):
    return 0.5 * jnp.tanh(0.5 * x) + 0.5


def _col_tile(n_cols, col_off=0, max_tn=TN):
    for tn in range(max_tn, 0, -LANES):
        if n_cols % tn == 0 and col_off % tn == 0:
            return tn
    raise ValueError(f"column group ({col_off}, {n_cols}) is not lane aligned")


def _wide_tile(k, out_bytes_per_col):
    tn = 2 * TN
    f32, bf16 = jnp.dtype(F32).itemsize, jnp.dtype(BF16).itemsize
    for buffers in (2, 1):
        need = k * tn * (buffers * f32 + bf16) + 2 * TM * k * bf16 + 2 * TM * tn * out_bytes_per_col
        if need <= VMEM_LIMIT_BYTES - VMEM_MARGIN_BYTES:
            return tn, buffers
    return TN, 2


def _seg_map(tiles_per_batch, n_batch):
    return lambda i: jnp.minimum(i // tiles_per_batch, n_batch)


def _cast_weight_once(w_ref, wb_ref):
    @pl.when(pl.program_id(1) == 0)
    def _():
        wb_ref[...] = w_ref[...].astype(BF16)


def _mods_kernel(c_ref, w_ref, b_ref, o_ref):
    s = jax.nn.silu(c_ref[...]).astype(BF16)
    o_ref[...] = jnp.dot(s, w_ref[...].astype(BF16), preferred_element_type=F32) + b_ref[...]


def _mods(c_all, w_mod, b_mod):
    depth, d, nm = w_mod.shape
    tn = _col_tile(nm)
    return pl.pallas_call(
        _mods_kernel,
        name="mods",
        out_shape=jax.ShapeDtypeStruct((depth, MOD_ROWS, nm), F32),
        grid=(depth, nm // tn),
        in_specs=[pl.BlockSpec((MOD_ROWS, d), lambda l, j: (0, 0)),
                  pl.BlockSpec((None, d, tn), lambda l, j: (l, 0, j)),
                  pl.BlockSpec((None, 1, tn), lambda l, j: (l, 0, j))],
        out_specs=pl.BlockSpec((None, MOD_ROWS, tn), lambda l, j: (l, 0, j)),
        compiler_params=_params("parallel", "parallel"),
    )(c_all, w_mod, b_mod.reshape(depth, 1, nm))


def _split_specs(n_lat_tiles, d):
    return [pl.BlockSpec((TM_ROW, d), lambda i: (jnp.minimum(i, n_lat_tiles - 1), 0)),
            pl.BlockSpec((TM_ROW, d), lambda i: (jnp.maximum(i - n_lat_tiles, 0), 0))]


def _modulate_kernel(xl_ref, xc_ref, mod_ref, h_ref, *, n_lat_tiles, i_shift, i_scale):
    x = jnp.where(pl.program_id(0) < n_lat_tiles, xl_ref[...], xc_ref[...])
    h = _ln(x) * (1.0 + mod_ref[i_scale:i_scale + 1, :]) + mod_ref[i_shift:i_shift + 1, :]
    h_ref[...] = h.astype(BF16)


def _modulate(x_lat, x_ctx, mods, i_shift, i_scale, seg):
    n_lat, d = x_lat.shape
    n = n_lat + x_ctx.shape[0]
    n_lat_tiles = n_lat // TM_ROW
    return pl.pallas_call(
        functools.partial(_modulate_kernel, n_lat_tiles=n_lat_tiles, i_shift=i_shift, i_scale=i_scale),
        name="modulate",
        out_shape=jax.ShapeDtypeStruct((n, d), BF16),
        grid=(n // TM_ROW,),
        in_specs=_split_specs(n_lat_tiles, d) + [pl.BlockSpec((None, N_MOD, d), lambda i: (seg(i), 0, 0))],
        out_specs=pl.BlockSpec((TM_ROW, d), lambda i: (i, 0)),
        compiler_params=_params("parallel"),
    )(x_lat, x_ctx, mods)


def _norm_kernel(*refs, n_lat_tiles, alpha, coef, i_gate, i_shift, i_scale):
    if n_lat_tiles is None:
        x_ref, y_ref, modg_ref, g_ref, b_ref, *rest = refs
        x = x_ref[...]
    else:
        xl_ref, xc_ref, y_ref, modg_ref, g_ref, b_ref, *rest = refs
        x = jnp.where(pl.program_id(0) < n_lat_tiles, xl_ref[...], xc_ref[...])
    t = alpha * x + (coef * modg_ref[i_gate:i_gate + 1, :]) * y_ref[...].astype(F32)
    xn = _ln(t) * g_ref[...] + b_ref[...]
    if i_shift is None:
        (xo_ref,) = rest
        xo_ref[...] = xn
    else:
        modn_ref, xo_ref, h_ref = rest
        xo_ref[...] = xn
        h = _ln(xn) * (1.0 + modn_ref[i_scale:i_scale + 1, :]) + modn_ref[i_shift:i_shift + 1, :]
        h_ref[...] = h.astype(BF16)


def _norm(x, y, mods_gate, i_gate, coef, g, b, alpha, seg, n_rows, mods_next=None, i_shift=None, i_scale=None,
          n_out=None):
    split = isinstance(x, tuple)
    d = y.shape[1]
    n_out = y.shape[0] if n_out is None else n_out
    row = pl.BlockSpec((TM_ROW, d), lambda i: (i, 0))
    vec = pl.BlockSpec((1, d), lambda i: (0, 0))
    mod = pl.BlockSpec((None, N_MOD, d), lambda i: (seg(i), 0, 0))
    n_lat_tiles = x[0].shape[0] // TM_ROW if split else None
    in_specs = (_split_specs(n_lat_tiles, d) if split else [row]) + [row, mod, vec, vec]
    args = (list(x) if split else [x]) + [y, mods_gate, g.reshape(1, d), b.reshape(1, d)]
    out_shape = [jax.ShapeDtypeStruct((n_out, d), F32)]
    out_specs = [row]
    if i_shift is not None:
        in_specs.append(mod)
        args.append(mods_next)
        out_shape.append(jax.ShapeDtypeStruct((n_out, d), BF16))
        out_specs.append(row)
    out = pl.pallas_call(
        functools.partial(_norm_kernel, n_lat_tiles=n_lat_tiles, alpha=alpha, coef=coef, i_gate=i_gate,
                          i_shift=i_shift, i_scale=i_scale),
        name="norm",
        out_shape=out_shape,
        grid=(n_rows // TM_ROW,),
        in_specs=in_specs,
        out_specs=out_specs,
        compiler_params=_params("parallel"),
    )(*args)
    return out if i_shift is not None else (out[0], None)


def _ffn_in_kernel(h_ref, wg_ref, wu_ref, o_ref, wgb_ref, wub_ref):
    _cast_weight_once(wg_ref, wgb_ref)
    _cast_weight_once(wu_ref, wub_ref)
    h = h_ref[...]
    g = jnp.dot(h, wgb_ref[...], preferred_element_type=F32)
    u = jnp.dot(h, wub_ref[...], preferred_element_type=F32)
    o_ref[...] = (g * _sigmoid(g) * u).astype(o_ref.dtype)


def _ffn_in(h, w, layer, n_rows):
    n, d = h.shape
    dff = w.shape[2] // 2
    tn = _col_tile(dff)
    return pl.pallas_call(
        _ffn_in_kernel,
        name="ffn_in",
        out_shape=jax.ShapeDtypeStruct((n, dff), BF16),
        grid=(dff // tn, n_rows // TM),
        in_specs=[pl.BlockSpec((TM, d), lambda j, i: (i, 0)),
                  pl.BlockSpec((None, d, tn), lambda j, i: (layer, 0, j)),
                  pl.BlockSpec((None, d, tn), lambda j, i: (layer, 0, j + dff // tn))],
        out_specs=pl.BlockSpec((TM, tn), lambda j, i: (i, j)),
        scratch_shapes=[pltpu.VMEM((d, tn), BF16), pltpu.VMEM((d, tn), BF16)],
        compiler_params=_params("arbitrary", "arbitrary"),
    )(h, w, w)


def _mm_kernel(a_ref, w_ref, o_ref, wb_ref):
    _cast_weight_once(w_ref, wb_ref)
    o_ref[...] = jnp.dot(a_ref[...], wb_ref[...], preferred_element_type=F32).astype(o_ref.dtype)


def _mm(a, w, layer, col_off, n_cols, out_dtype, n_rows, name):
    n, k = a.shape
    max_tn, buffers = _wide_tile(k, jnp.dtype(out_dtype).itemsize)
    tn = _col_tile(n_cols, col_off, max_tn)
    joff = col_off // tn
    return pl.pallas_call(
        _mm_kernel,
        name=name,
        out_shape=jax.ShapeDtypeStruct((n, n_cols), out_dtype),
        grid=(n_cols // tn, n_rows // TM),
        in_specs=[pl.BlockSpec((TM, k), lambda j, i: (i, 0)),
                  pl.BlockSpec((None, k, tn), lambda j, i: (layer, 0, j + joff),
                               pipeline_mode=pl.Buffered(buffers))],
        out_specs=pl.BlockSpec((TM, tn), lambda j, i: (i, j)),
        scratch_shapes=[pltpu.VMEM((k, tn), BF16)],
        compiler_params=_params("arbitrary", "arbitrary"),
    )(a, w)


def _mm_rope_kernel(a_ref, w_ref, cos_ref, sin_ref, o_ref, orot_ref, wb_ref):
    _cast_weight_once(w_ref, wb_ref)
    acc = jnp.dot(a_ref[...], wb_ref[...], preferred_element_type=F32)
    o_ref[...] = acc.astype(o_ref.dtype)
    tn = acc.shape[1]
    reps = tn // HEAD_DIM
    cos = jnp.concatenate([cos_ref[...]] * reps, axis=1)
    sin = jnp.concatenate([sin_ref[...]] * reps, axis=1)
    lane = lax.broadcasted_iota(jnp.int32, acc.shape, 1)
    quarter = HEAD_DIM // 4
    first = (lane % (2 * quarter)) < quarter
    partner = jnp.where(first, pltpu.roll(acc, tn - quarter, 1), pltpu.roll(acc, quarter, 1))
    orot_ref[...] = (acc * cos + partner * sin).astype(orot_ref.dtype)


def _mm_rope(a, w, layer, col_off, n_cols, cos, sin, tiles_per_batch, n_rows):
    n, k = a.shape
    max_tn, buffers = _wide_tile(k, 2 * jnp.dtype(BF16).itemsize)
    tn = _col_tile(n_cols, col_off, max_tn)
    joff = col_off // tn
    out = jax.ShapeDtypeStruct((n, n_cols), BF16)
    tab = pl.BlockSpec((TM, HEAD_DIM), lambda j, i: (i % tiles_per_batch, 0))
    blk = pl.BlockSpec((TM, tn), lambda j, i: (i, j))
    return pl.pallas_call(
        _mm_rope_kernel,
        name="proj_rope",
        out_shape=[out, out],
        grid=(n_cols // tn, n_rows // TM),
        in_specs=[pl.BlockSpec((TM, k), lambda j, i: (i, 0)),
                  pl.BlockSpec((None, k, tn), lambda j, i: (layer, 0, j + joff),
                               pipeline_mode=pl.Buffered(buffers)),
                  tab, tab],
        out_specs=[blk, blk],
        scratch_shapes=[pltpu.VMEM((k, tn), BF16)],
        compiler_params=_params("arbitrary", "arbitrary"),
    )(a, w, cos, sin)


def _merge_kernel(h_ref, oa_ref, ob_ref, oc_ref, wga_ref, wgb_ref, wgc_ref, wa_ref, wb_ref, wc_ref, o_ref,
                  *scratch):
    weights = (wga_ref, wgb_ref, wgc_ref, wa_ref, wb_ref, wc_ref)
    for w_ref, wb16_ref in zip(weights, scratch):
        _cast_weight_once(w_ref, wb16_ref)
    gate_w, proj_w = scratch[:N_BRANCH], scratch[N_BRANCH:]
    h = h_ref[...]
    t = None
    for branch_ref, wg, wp in zip((oa_ref, ob_ref, oc_ref), gate_w, proj_w):
        gate = _sigmoid(jnp.dot(h, wg[...], preferred_element_type=F32))
        term = gate * jnp.dot(branch_ref[...], wp[...], preferred_element_type=F32)
        t = term if t is None else t + term
    o_ref[...] = t.astype(o_ref.dtype)


def _merge(h, o_a, o_b, o_c, w_in, gate_off, w_pa, w_pb, w_pc, layer, n_rows):
    n, d = h.shape
    tn = _col_tile(d, gate_off, MERGE_TN)

    def act(o):
        return pl.BlockSpec((TM, o.shape[1]), lambda j, i: (i, 0))

    def proj(w):
        return pl.BlockSpec((None, w.shape[1], tn), lambda j, i: (layer, 0, j))

    def gate(branch):
        joff = (gate_off + branch * d) // tn
        return pl.BlockSpec((None, d, tn), lambda j, i: (layer, 0, j + joff), pipeline_mode=pl.Buffered(1))

    return pl.pallas_call(
        _merge_kernel,
        name="merge",
        out_shape=jax.ShapeDtypeStruct((n, d), BF16),
        grid=(d // tn, n_rows // TM),
        in_specs=[act(h), act(o_a), act(o_b), act(o_c), gate(0), gate(1), gate(2), proj(w_pa), proj(w_pb), proj(w_pc)],
        out_specs=pl.BlockSpec((TM, tn), lambda j, i: (i, j)),
        scratch_shapes=([pltpu.VMEM((d, tn), BF16)] * N_BRANCH
                        + [pltpu.VMEM((w.shape[1], tn), BF16) for w in (w_pa, w_pb, w_pc)]),
        compiler_params=_params("arbitrary", "arbitrary"),
    )(h, o_a, o_b, o_c, w_in, w_in, w_in, w_pa, w_pb, w_pc)


def _attn_block_geometry(kind):
    half = WIN_H // 2
    if kind == 0:
        return 0, [max(i - half, 0) for i in range(ATT_ROWS)]
    if kind == 1:
        return -half, list(range(ATT_ROWS))
    return -(ATT_KROWS - ATT_ROWS), [min(half + i, ATT_KROWS - WIN_H) for i in range(ATT_ROWS)]


def _attn_sub_key_offset(kind, sub):
    _, first_key_row = _attn_block_geometry(kind)
    return min(min(first_key_row[sub * ATT_SUB:(sub + 1) * ATT_SUB]), ATT_KROWS - ATT_SUB_KROWS)


def _rpb_table_kernel(rpb_ref, o_ref):
    shape = (GRID_W, 2 * GRID_W)
    cq = lax.broadcasted_iota(jnp.int32, shape, 0)
    lane = lax.broadcasted_iota(jnp.int32, shape, 1)
    ck = lane % GRID_W
    second = lane >= GRID_W
    start = jnp.clip(cq - WIN_W // 2, 0, GRID_W - WIN_W)
    in_win = (ck >= start) & (ck < start + WIN_W)
    neg = jnp.full(shape, NEG_INF, F32)
    n_bias_rows = 2 * WIN_H - 1

    def half_tile(a, lane_off):
        row = jnp.broadcast_to(rpb_ref[a:a + 1, :], shape)
        return pltpu.roll(row, (lane_off - (WIN_W - 1)) % LANES, 1, stride=1, stride_axis=0)

    lo_half = [half_tile(a, 0) for a in range(n_bias_rows)]
    hi_half = [half_tile(a, GRID_W) for a in range(n_bias_rows)]
    for kind in range(3):
        diff, first_key_row = _attn_block_geometry(kind)
        for i in range(ATT_ROWS):
            lo = first_key_row[i]
            sub, isub = divmod(i, ATT_SUB)
            koff = _attn_sub_key_offset(kind, sub)
            for pair in range(ATT_SUB_KROWS // 2):
                j0, j1 = koff + 2 * pair, koff + 2 * pair + 1
                t0 = lo_half[j0 - i + diff + WIN_H - 1] if lo <= j0 < lo + WIN_H else neg
                t1 = hi_half[j1 - i + diff + WIN_H - 1] if lo <= j1 < lo + WIN_H else neg
                tile = jnp.where(in_win, jnp.where(second, t1, t0), neg)
                o_ref[kind, sub, isub * GRID_W:(isub + 1) * GRID_W,
                      pair * 2 * GRID_W:(pair + 1) * 2 * GRID_W] = tile


def _rpb_table(rpb):
    heads, nr, nc = rpb.shape
    rpb_pad = jnp.zeros((heads, 2 * WIN_H, LANES), F32).at[:, :nr, :nc].set(rpb)
    shape = (3, ATT_ROWS // ATT_SUB, ATT_SUB * GRID_W, ATT_SUB_KROWS * GRID_W)
    return pl.pallas_call(
        _rpb_table_kernel,
        name="rpb_table",
        out_shape=jax.ShapeDtypeStruct((heads,) + shape, F32),
        grid=(heads,),
        in_specs=[pl.BlockSpec((None, 2 * WIN_H, LANES), lambda h: (h, 0, 0))],
        out_specs=pl.BlockSpec((None,) + shape, lambda h: (h, 0, 0, 0, 0)),
        compiler_params=_params("parallel"),
    )(rpb_pad)


def _attn_kernel(q_ref, qr_ref, kr_ref, v_ref, kc_ref, vc_ref, bias_ref, o_ref, *, n_grid_rows, n_blocks, scale):
    rb = pl.program_id(2)
    kr0 = jnp.clip(rb * ATT_ROWS - WIN_H // 2, 0, n_grid_rows - ATT_KROWS)
    kc = kc_ref[...]
    vc = vc_ref[...]
    for sub in range(ATT_ROWS // ATT_SUB):
        offs = [_attn_sub_key_offset(kind, sub) for kind in range(3)]
        koff = jnp.where(rb == 0, offs[0], jnp.where(rb == n_blocks - 1, offs[2], offs[1]))
        keys = pl.ds(pl.multiple_of((kr0 + koff) * GRID_W, GRID_W), ATT_SUB_KROWS * GRID_W)
        rows = pl.ds(sub * ATT_SUB * GRID_W, ATT_SUB * GRID_W)
        s = lax.dot_general(qr_ref[rows, :], kr_ref[keys, :], NT_DIMS, preferred_element_type=F32) * scale
        s += bias_ref[sub]
        sc = lax.dot_general(q_ref[rows, :], kc, NT_DIMS, preferred_element_type=F32) * scale
        m = jnp.maximum(jnp.max(s, axis=-1, keepdims=True), jnp.max(sc, axis=-1, keepdims=True))
        p = jnp.exp(s - m)
        pc = jnp.exp(sc - m)
        denom = jnp.sum(p, axis=-1, keepdims=True) + jnp.sum(pc, axis=-1, keepdims=True)
        o = jnp.dot(p.astype(BF16), v_ref[keys, :], preferred_element_type=F32)
        o += jnp.dot(pc.astype(BF16), vc, preferred_element_type=F32)
        o_ref[rows, :] = (o / denom).astype(o_ref.dtype)


def _attention(q, q_rot, k, k_rot, v, bias, n_batch, seq, ctx_len):
    n, width = q.shape
    heads = width // HEAD_DIM
    n_grid_rows = seq // GRID_W
    tq = ATT_ROWS * GRID_W
    blocks = seq // tq
    ctx0 = n_batch * seq // ctx_len
    qspec = pl.BlockSpec((tq, HEAD_DIM), lambda b, h, r: (b * blocks + r, h))
    kspec = pl.BlockSpec((seq, HEAD_DIM), lambda b, h, r: (b, h))
    cspec = pl.BlockSpec((ctx_len, HEAD_DIM), lambda b, h, r: (ctx0 + b, h))

    def kind(r):
        return jnp.where(r == 0, 0, jnp.where(r == blocks - 1, 2, 1))

    return pl.pallas_call(
        functools.partial(_attn_kernel, n_grid_rows=n_grid_rows, n_blocks=blocks, scale=HEAD_DIM ** -0.5),
        name="attn",
        out_shape=jax.ShapeDtypeStruct((n, width), BF16),
        grid=(n_batch, heads, blocks),
        in_specs=[qspec, qspec, kspec, kspec, cspec, cspec,
                  pl.BlockSpec((None, None) + bias.shape[2:], lambda b, h, r: (h, kind(r), 0, 0, 0))],
        out_specs=qspec,
        compiler_params=_params("parallel", "parallel", "arbitrary"),
    )(q, q_rot, k_rot, v, k, v, bias)


def _ctx_attn_kernel(q_ref, k_ref, v_ref, prev_ref, o_ref, *, scale):
    del prev_ref
    s = lax.dot_general(q_ref[...], k_ref[...], NT_DIMS, preferred_element_type=F32) * scale
    m = jnp.max(s, axis=-1, keepdims=True)
    p = jnp.exp(s - m)
    denom = jnp.sum(p, axis=-1, keepdims=True)
    o = jnp.dot(p.astype(BF16), v_ref[...], preferred_element_type=F32)
    o_ref[...] = (o / denom).astype(o_ref.dtype)


def _ctx_attention(q, k, v, o_att, n_batch, seq, ctx_len):
    n, width = q.shape
    heads = width // HEAD_DIM
    ctx0 = n_batch * seq // ctx_len
    spec = pl.BlockSpec((ctx_len, HEAD_DIM), lambda b, h: (ctx0 + b, h))
    return pl.pallas_call(
        functools.partial(_ctx_attn_kernel, scale=HEAD_DIM ** -0.5),
        name="ctx_attn",
        out_shape=jax.ShapeDtypeStruct((n, width), BF16),
        grid=(n_batch, heads),
        in_specs=[spec, spec, spec, pl.BlockSpec(memory_space=pl.ANY)],
        out_specs=spec,
        input_output_aliases={3: 0},
        compiler_params=_params("parallel", "parallel"),
    )(q, k, v, o_att)


def _gmlp_kernel(z_ref, g_ref, b_ref, ws_ref, bs_ref, o_ref, *, width, chunks):
    z = jax.nn.gelu(z_ref[...])
    u = z[:, :width]
    v = (_ln(z[:, width:]) * g_ref[...] + b_ref[...]).astype(BF16)
    for c in range(chunks):
        rows = slice(c * CHUNK, (c + 1) * CHUNK)
        for g in range(width // GMLP_GROUP_DIM):
            cols = slice(g * GMLP_GROUP_DIM, (g + 1) * GMLP_GROUP_DIM)
            mixed = jnp.dot(ws_ref[g], v[rows, cols], preferred_element_type=F32) + bs_ref[:, cols]
            o_ref[rows, cols] = (u[rows, cols] * mixed).astype(o_ref.dtype)


def _gmlp(z, ln_g, ln_b, w_s, b_s, n_rows):
    n, two_w = z.shape
    width = two_w // 2
    groups = width // GMLP_GROUP_DIM
    chunks = TM // CHUNK
    bs = jnp.repeat(b_s.T, GMLP_GROUP_DIM, axis=1)
    vec = pl.BlockSpec((1, width), lambda i: (0, 0))
    return pl.pallas_call(
        functools.partial(_gmlp_kernel, width=width, chunks=chunks),
        name="gmlp",
        out_shape=jax.ShapeDtypeStruct((n, width), BF16),
        grid=(n_rows // TM,),
        in_specs=[pl.BlockSpec((TM, two_w), lambda i: (i, 0)), vec, vec,
                  pl.BlockSpec((groups, CHUNK, CHUNK), lambda i: (0, 0, 0)),
                  pl.BlockSpec((CHUNK, width), lambda i: (0, 0))],
        out_specs=pl.BlockSpec((TM, width), lambda i: (i, 0)),
        compiler_params=_params("parallel"),
    )(z, ln_g.reshape(1, width), ln_b.reshape(1, width), w_s.astype(BF16), bs)


def _s5_gen_kernel(lr_ref, li_ref, ldt_ref, bre_ref, bim_ref, cre_ref, cim_ref, m_ref, wst_ref, et_ref, a_ref):
    hi = lax.Precision.HIGHEST
    nb = SSM_BLOCK
    lane = lax.broadcasted_iota(jnp.int32, (SSM_GROUP, SSM_ROW), 1)
    state_lane = lax.broadcasted_iota(jnp.int32, (SSM_GROUP, 2 * SSM_STATE), 1)
    in_group = [state_lane < SSM_STATE, state_lane >= SSM_STATE]
    m_rows = [[None] * nb for _ in range(2)]
    for d in range(2):
        lr = lr_ref[d:d + 1, :]
        li = li_ref[d:d + 1, :]
        dt = jnp.exp(ldt_ref[d:d + 1, :])
        mag = jnp.exp(lr * dt)
        ar = mag * jnp.cos(li * dt)
        ai = mag * jnp.sin(li * dt)
        den = lr * lr + li * li
        fr = ((ar - 1.0) * lr + ai * li) / den
        fi = (ai * lr - (ar - 1.0) * li) / den
        br = bre_ref[d]
        bi = bim_ref[d]
        bbr = fr * br - fi * bi
        bbi = fr * bi + fi * br
        cr = cre_ref[d]
        ci = cim_ref[d]
        pr = [jnp.ones_like(ar)]
        pi = [jnp.zeros_like(ai)]
        for _ in range(nb):
            pr.append(pr[-1] * ar - pi[-1] * ai)
            pi.append(pr[-2] * ai + pi[-1] * ar)
        a_ref[2 * d] = pr[nb]
        a_ref[2 * d + 1] = pi[nb]
        e_state = [(t + 1) if d == 0 else (nb - t) for t in range(nb)]
        e_in = [(nb - 1 - s) if d == 0 else s for s in range(nb)]
        e_lag = [e if d == 0 else (nb - 1 - e) for e in range(nb)]
        for s in range(nb):
            wr, wi = pr[e_in[s]], pi[e_in[s]]
            er, ei = pr[e_state[s]], pi[e_state[s]]
            parts = (wr * bbr - wi * bbi, wr * bbi + wi * bbr)
            outs = (cr * er - ci * ei, -(cr * ei + ci * er))
            for g in range(2):
                rows = slice(g * SSM_ROW + s * SSM_GROUP, g * SSM_ROW + (s + 1) * SSM_GROUP)
                for part in range(2):
                    cols = slice((2 * d + part) * 2 * SSM_STATE, (2 * d + part + 1) * 2 * SSM_STATE)
                    wst_ref[rows, cols] = jnp.where(in_group[g], parts[part], 0.0).astype(wst_ref.dtype)
                    et_ref[rows, cols] = jnp.where(in_group[g], outs[part], 0.0).astype(et_ref.dtype)
        lag_r = jnp.concatenate([cr * pr[e] - ci * pi[e] for e in e_lag], axis=0)
        lag_i = jnp.concatenate([-(cr * pi[e] + ci * pr[e]) for e in e_lag], axis=0)
        for g in range(2):
            gbr = jnp.where(in_group[g], bbr, 0.0)
            gbi = jnp.where(in_group[g], bbi, 0.0)
            kbase = (lax.dot_general(gbr, lag_r, NT_DIMS, precision=hi, preferred_element_type=F32)
                     + lax.dot_general(gbi, lag_i, NT_DIMS, precision=hi, preferred_element_type=F32))
            for s in range(nb):
                if d == 0:
                    sh = s * SSM_GROUP
                    m_rows[g][s] = kbase if sh == 0 else jnp.where(lane >= sh, pltpu.roll(kbase, sh, 1), 0.0)
                else:
                    sh = (nb - 1 - s) * SSM_GROUP
                    row = kbase if sh == 0 else jnp.where(lane < SSM_ROW - sh, pltpu.roll(kbase, SSM_ROW - sh, 1), 0.0)
                    m_rows[g][s] = m_rows[g][s] + row
    zeros = jnp.zeros((SSM_GROUP, SSM_ROW), m_ref.dtype)
    for g in range(2):
        for s in range(nb):
            rows = slice(g * SSM_ROW + s * SSM_GROUP, g * SSM_ROW + (s + 1) * SSM_GROUP)
            m_ref[rows, g * SSM_ROW:(g + 1) * SSM_ROW] = m_rows[g][s].astype(m_ref.dtype)
            m_ref[rows, (1 - g) * SSM_ROW:(2 - g) * SSM_ROW] = zeros


def _s5_gen(lam_re, lam_im, log_dt, b_re, b_im, c_re, c_im):
    _, groups, p = lam_re.shape
    pairs = groups // 2

    def vec_pairs(x):
        return jnp.swapaxes(x.reshape(2, pairs, 2 * p), 0, 1)

    def mat_pairs(x):
        x = x.reshape(2, pairs, 2, SSM_GROUP, p)
        return jnp.transpose(x, (1, 0, 3, 2, 4)).reshape(pairs, 2, SSM_GROUP, 2 * p)

    ldt = jnp.broadcast_to(log_dt[:, :, None], lam_re.shape)
    vec = pl.BlockSpec((None, 2, 2 * p), lambda g: (g, 0, 0))
    mat = pl.BlockSpec((None, 2, SSM_GROUP, 2 * p), lambda g: (g, 0, 0, 0))
    return pl.pallas_call(
        _s5_gen_kernel,
        name="s5_gen",
        out_shape=[jax.ShapeDtypeStruct((pairs, 2 * SSM_ROW, 2 * SSM_ROW), BF16),
                   jax.ShapeDtypeStruct((pairs, 2 * SSM_ROW, 4 * 2 * p), BF16),
                   jax.ShapeDtypeStruct((pairs, 2 * SSM_ROW, 4 * 2 * p), BF16),
                   jax.ShapeDtypeStruct((pairs, 4, 1, 2 * p), F32)],
        grid=(pairs,),
        in_specs=[vec, vec, vec, mat, mat, mat, mat],
        out_specs=[pl.BlockSpec((None, 2 * SSM_ROW, 2 * SSM_ROW), lambda g: (g, 0, 0)),
                   pl.BlockSpec((None, 2 * SSM_ROW, 4 * 2 * p), lambda g: (g, 0, 0)),
                   pl.BlockSpec((None, 2 * SSM_ROW, 4 * 2 * p), lambda g: (g, 0, 0)),
                   pl.BlockSpec((None, 4, 1, 2 * p), lambda g: (g, 0, 0, 0))],
        compiler_params=_params("parallel"),
    )(vec_pairs(lam_re), vec_pairs(lam_im), vec_pairs(ldt),
      mat_pairs(jnp.swapaxes(b_re, 2, 3)), mat_pairs(jnp.swapaxes(b_im, 2, 3)), mat_pairs(c_re), mat_pairs(c_im))


def _slot_transpose(xs):
    n = SLOTS
    slot = lax.broadcasted_iota(jnp.int32, xs[0].shape, 1) // SSM_GROUP
    diag = []
    for r in range(n):
        w = xs[r % n]
        for a in range(1, n):
            w = jnp.where(slot == a, xs[(a + r) % n], w)
        diag.append(w if r == 0 else pltpu.roll(w, r * SSM_GROUP, 1))
    ys = []
    for a in range(n):
        y = diag[0]
        for r in range(1, n):
            y = jnp.where(slot == (a + r) % n, diag[r], y)
        ys.append(y)
    return ys


def _s5_state_kernel(x_ref, wst_ref, u_ref, *state_refs):
    n_blk = u_ref.shape[0]
    xs = [x_ref[pl.ds(s, n_blk, stride=SSM_BLOCK), :] for s in range(SSM_BLOCK)]
    by_col = [_slot_transpose(xs[col * SLOTS:(col + 1) * SLOTS]) for col in range(SSM_BLOCK // SLOTS)]
    pieces = [by_col[col][gl] for gl in range(SSM_CHUNK_GROUPS) for col in range(SSM_BLOCK // SLOTS)]
    cols_per_pair = 2 * SSM_ROW // LANES
    for pair in range(SSM_CHUNK_GROUPS // 2):
        u = jnp.concatenate(pieces[pair * cols_per_pair:(pair + 1) * cols_per_pair], axis=1).astype(BF16)
        u_ref[:, pair * 2 * SSM_ROW:(pair + 1) * 2 * SSM_ROW] = u
        states = jnp.dot(u, wst_ref[pair], preferred_element_type=F32)
        for k, o_ref in enumerate(state_refs):
            o_ref[:, pair * 2 * SSM_STATE:(pair + 1) * 2 * SSM_STATE] = states[:, k * 2 * SSM_STATE:
                                                                                (k + 1) * 2 * SSM_STATE]


def _s5_row_tile(n_rows):
    return n_rows // 2 if n_rows % 32 == 0 else n_rows


def _s5_states(us, wst_pair, groups):
    n_rows = us.shape[0] // SSM_BLOCK
    chunks = groups // SSM_CHUNK_GROUPS
    pairs = SSM_CHUNK_GROUPS // 2
    rt = _s5_row_tile(n_rows)
    state = jax.ShapeDtypeStruct((n_rows, groups * SSM_STATE), F32)
    sspec = pl.BlockSpec((rt, SSM_CHUNK_GROUPS * SSM_STATE), lambda ch, i: (i, ch))
    uspec = pl.BlockSpec((rt, SSM_CHUNK_GROUPS * SSM_ROW), lambda ch, i: (i, ch))
    return pl.pallas_call(
        _s5_state_kernel,
        name="s5_states",
        out_shape=[jax.ShapeDtypeStruct((n_rows, groups * SSM_ROW), BF16)] + [state] * 4,
        grid=(chunks, n_rows // rt),
        in_specs=[pl.BlockSpec((rt * SSM_BLOCK, LANES), lambda ch, i: (i, ch)),
                  pl.BlockSpec((pairs,) + wst_pair.shape[1:], lambda ch, i: (ch, 0, 0))],
        out_specs=[uspec] + [sspec] * 4,
        compiler_params=_params("parallel", "parallel"),
    )(us, wst_pair)


def _s5_scan_kernel(sfr_ref, sfi_ref, sbr_ref, sbi_ref, a_ref, hfr_ref, hfi_ref, hbr_ref, hbi_ref,
                    *, n_batch, n_lat_blocks, n_ctx_blocks):
    afr, afi, abr, abi = (a_ref[k:k + 1, :] for k in range(4))
    zero = jnp.zeros_like(afr)
    nl, nc = n_lat_blocks, n_ctx_blocks

    def step(t, carry):
        new = []
        in_ctx = t < nc
        for b in range(n_batch):
            lat0, ctx0 = b * nl, n_batch * nl + b * nc
            fr, fi, br, bi = carry[4 * b:4 * b + 4]
            rf = pl.ds(jnp.where(in_ctx, ctx0 + t, lat0 + t - nc), 1)
            rb = pl.ds(jnp.where(in_ctx, ctx0 + nc - 1 - t, lat0 + nl - 1 - (t - nc)), 1)
            hfr_ref[rf, :] = fr
            hfi_ref[rf, :] = fi
            hbr_ref[rb, :] = br
            hbi_ref[rb, :] = bi
            new += [afr * fr - afi * fi + sfr_ref[rf, :], afr * fi + afi * fr + sfi_ref[rf, :],
                    abr * br - abi * bi + sbr_ref[rb, :], abr * bi + abi * br + sbi_ref[rb, :]]
        return tuple(new)

    lax.fori_loop(0, nl + nc, step, (zero,) * (4 * n_batch))


def _s5_scan(s_parts, a_rows, n_batch, n_lat_blocks, n_ctx_blocks):
    rows, lanes = s_parts[0].shape
    tl = min(lanes, 4 * LANES)
    spec = pl.BlockSpec((rows, tl), lambda j: (0, j))
    return pl.pallas_call(
        functools.partial(_s5_scan_kernel, n_batch=n_batch, n_lat_blocks=n_lat_blocks, n_ctx_blocks=n_ctx_blocks),
        name="s5_scan",
        out_shape=[jax.ShapeDtypeStruct((rows, lanes), F32)] * 4,
        grid=(lanes // tl,),
        in_specs=[spec] * 4 + [pl.BlockSpec((4, tl), lambda j: (0, j))],
        out_specs=[spec] * 4,
        compiler_params=_params("parallel"),
    )(*s_parts, a_rows)


def _s5_out_kernel(u_ref, m_ref, et_ref, hfr_ref, hfi_ref, hbr_ref, hbi_ref, y_ref):
    ys = []
    for pair in range(SSM_CHUNK_GROUPS // 2):
        y = jnp.dot(u_ref[:, pair * 2 * SSM_ROW:(pair + 1) * 2 * SSM_ROW], m_ref[pair], preferred_element_type=F32)
        h = jnp.concatenate([h_ref[:, pair * 2 * SSM_STATE:(pair + 1) * 2 * SSM_STATE]
                             for h_ref in (hfr_ref, hfi_ref, hbr_ref, hbi_ref)], axis=1).astype(BF16)
        ys.append(y + lax.dot_general(h, et_ref[pair], NT_DIMS, preferred_element_type=F32))
    n_blk = ys[0].shape[0]
    for col in range(SSM_BLOCK // SLOTS):
        by_group = [ys[gl // 2][:, (gl % 2) * SSM_ROW + col * LANES:(gl % 2) * SSM_ROW + (col + 1) * LANES]
                    for gl in range(SSM_CHUNK_GROUPS)]
        for k, y in enumerate(_slot_transpose(by_group)):
            y_ref[pl.ds(col * SLOTS + k, n_blk, stride=SSM_BLOCK), :] = y


def _s5_outputs(u, m_pair, et_pair, h_parts, groups):
    n_rows = u.shape[0]
    chunks = groups // SSM_CHUNK_GROUPS
    pairs = SSM_CHUNK_GROUPS // 2
    rt = _s5_row_tile(n_rows)
    hspec = pl.BlockSpec((rt, SSM_CHUNK_GROUPS * SSM_STATE), lambda ch, i: (i, ch))
    return pl.pallas_call(
        _s5_out_kernel,
        name="s5_out",
        out_shape=jax.ShapeDtypeStruct((n_rows * SSM_BLOCK, groups * SSM_GROUP), F32),
        grid=(chunks, n_rows // rt),
        in_specs=[pl.BlockSpec((rt, SSM_CHUNK_GROUPS * SSM_ROW), lambda ch, i: (i, ch)),
                  pl.BlockSpec((pairs,) + m_pair.shape[1:], lambda ch, i: (ch, 0, 0)),
                  pl.BlockSpec((pairs,) + et_pair.shape[1:], lambda ch, i: (ch, 0, 0)),
                  hspec, hspec, hspec, hspec],
        out_specs=pl.BlockSpec((rt * SSM_BLOCK, LANES), lambda ch, i: (i, ch)),
        compiler_params=_params("parallel", "parallel"),
    )(u, m_pair, et_pair, *h_parts)


def _s5_glu_kernel(ys_ref, us_ref, d_ref, w_ref, b_ref, o_ref, wb_ref):
    @pl.when(pl.program_id(0) == 0)
    def _():
        wb_ref[...] = w_ref[...].astype(BF16)

    y = jax.nn.gelu(ys_ref[...] + d_ref[...] * us_ref[...])
    gate = _sigmoid(jnp.dot(y.astype(BF16), wb_ref[...], preferred_element_type=F32) + b_ref[...])
    o_ref[...] = (y * gate).astype(o_ref.dtype)


def _s5_glu(y_s, us, d_skip, w_glu, layer, b_glu, n_rows):
    n, w = us.shape
    row = pl.BlockSpec((TM, w), lambda i: (i, 0))
    vec = pl.BlockSpec((1, w), lambda i: (0, 0))
    return pl.pallas_call(
        _s5_glu_kernel,
        name="s5_glu",
        out_shape=jax.ShapeDtypeStruct((n, w), BF16),
        grid=(n_rows // TM,),
        in_specs=[row, row, vec, pl.BlockSpec((None, w, w), lambda i: (layer, 0, 0)), vec],
        out_specs=row,
        scratch_shapes=[pltpu.VMEM((w, w), BF16)],
        compiler_params=_params("arbitrary"),
    )(y_s, us, d_skip.reshape(1, w), w_glu, b_glu.reshape(1, w))


def _s5(us, n_batch, seq, ctx_len, lam_re, lam_im, log_dt, b_re, b_im, c_re, c_im):
    groups = us.shape[1] // SSM_GROUP
    m_pair, wst_pair, et_pair, a_pair = _s5_gen(lam_re, lam_im, log_dt, b_re, b_im, c_re, c_im)
    a_rows = jnp.transpose(a_pair[:, :, 0, :], (1, 0, 2)).reshape(4, groups * SSM_STATE)
    u, *s_parts = _s5_states(us, wst_pair, groups)
    h_parts = _s5_scan(s_parts, a_rows, n_batch, seq // SSM_BLOCK, ctx_len // SSM_BLOCK)
    return _s5_outputs(u, m_pair, et_pair, h_parts, groups)


def _rope_tables(seq):
    t = jnp.arange(seq)
    quarter = HEAD_DIM // 4
    inv = ROPE_BASE ** (-jnp.arange(quarter, dtype=F32) / quarter)
    ang_r = (t // GRID_W).astype(F32)[:, None] * inv[None, :]
    ang_c = (t % GRID_W).astype(F32)[:, None] * inv[None, :]
    cos = jnp.concatenate([jnp.cos(ang_r)] * 2 + [jnp.cos(ang_c)] * 2, axis=1)
    sin = jnp.concatenate([-jnp.sin(ang_r), jnp.sin(ang_r), -jnp.sin(ang_c), jnp.sin(ang_c)], axis=1)
    return cos, sin


def kernel(x, c, ctx, c_ctx, w_mod, b_mod, ln_g, ln_b, ffn1_w_in, ffn1_w_out, ffn2_w_in, ffn2_w_out, w_in, rpb, gm_ln_g, gm_ln_b, gm_ws, gm_bs, ssm_lam_re, ssm_lam_im, ssm_log_dt, ssm_b_re, ssm_b_im, ssm_c_re, ssm_c_im, ssm_d, ssm_w_glu, ssm_b_glu, w_pa, w_pb, w_pc, w_o):
    n_batch, seq, d = x.shape
    ctx_len = ctx.shape[1]
    depth = w_mod.shape[0]
    att_w = w_pb.shape[1]
    gm_w = w_pa.shape[1]
    ssm_w = w_pc.shape[1]
    assert ssm_w % LANES == 0 and seq % SSM_BLOCK == 0 and ctx_len % SSM_BLOCK == 0
    q_off, k_off, v_off = 0, att_w, 2 * att_w
    gm_off = 3 * att_w
    ssm_off = gm_off + 2 * gm_w
    gate_off = ssm_off + ssm_w
    alpha = (2.0 * depth) ** 0.25
    n_lat = n_batch * seq
    n_all = n_lat + n_batch * ctx_len
    assert seq % TM == 0 and (n_batch * ctx_len) % TM == 0 and seq % (ATT_ROWS * GRID_W) == 0
    assert seq // GRID_W >= ATT_KROWS and ctx_len % CHUNK == 0 and n_lat % ctx_len == 0
    assert n_batch + 1 <= MOD_ROWS and rpb.shape[2:] == (2 * WIN_H - 1, 2 * WIN_W - 1)

    seg = _seg_map(seq // TM_ROW, n_batch)
    tiles_per_batch = seq // TM

    x_lat, x_ctx = x.reshape(n_lat, d), ctx.reshape(n_batch * ctx_len, d)
    c_all = jnp.zeros((MOD_ROWS, d), F32).at[:n_batch].set(c).at[n_batch].set(c_ctx)
    mods = _mods(c_all, w_mod, b_mod)[:, :n_batch + 1].reshape(depth, n_batch + 1, N_MOD, d)
    cos, sin = _rope_tables(seq)

    xs = (x_lat, x_ctx)
    h = _modulate(x_lat, x_ctx, mods[0], 0, 1, seg)
    for l in range(depth):
        last = l == depth - 1
        n_mix = n_lat if last else n_all

        y = _mm(_ffn_in(h, ffn1_w_in, l, n_all), ffn1_w_out, l, 0, d, BF16, n_all, "ffn_out")
        xs, h = _norm(xs, y, mods[l], 2, 0.5, ln_g[l, 0], ln_b[l, 0], alpha, seg, n_all,
                      mods_next=mods[l], i_shift=3, i_scale=4)

        q, q_rot = _mm_rope(h, w_in, l, q_off, att_w, cos, sin, tiles_per_batch, n_mix)
        k, k_rot = _mm_rope(h, w_in, l, k_off, att_w, cos, sin, tiles_per_batch, n_all)
        v = _mm(h, w_in, l, v_off, att_w, BF16, n_all, "proj_v")
        z = _mm(h, w_in, l, gm_off, 2 * gm_w, F32, n_mix, "proj_z")
        us = _mm(h, w_in, l, ssm_off, ssm_w, F32, n_all, "proj_us")

        o_att = _attention(q, q_rot, k, k_rot, v, _rpb_table(rpb[l]), n_batch, seq, ctx_len)
        if not last:
            o_att = _ctx_attention(q, k, v, o_att, n_batch, seq, ctx_len)
        o_gm = _gmlp(z, gm_ln_g[l], gm_ln_b[l], gm_ws[l], gm_bs[l], n_mix)
        y_s = _s5(us, n_batch, seq, ctx_len, ssm_lam_re[l], ssm_lam_im[l], ssm_log_dt[l],
                  ssm_b_re[l], ssm_b_im[l], ssm_c_re[l], ssm_c_im[l])
        o_ssm = _s5_glu(y_s, us, ssm_d[l], ssm_w_glu, l, ssm_b_glu[l], n_mix)
        merged = _merge(h, o_gm, o_att, o_ssm, w_in, gate_off, w_pa, w_pb, w_pc, l, n_mix)
        y = _mm(merged, w_o, l, 0, d, BF16, n_mix, "out_proj")
        xs, h = _norm(xs, y, mods[l], 5, 1.0, ln_g[l, 1], ln_b[l, 1], alpha, seg, n_mix,
                      mods_next=mods[l], i_shift=6, i_scale=7)

        y = _mm(_ffn_in(h, ffn2_w_in, l, n_mix), ffn2_w_out, l, 0, d, BF16, n_mix, "ffn_out")
        if last:
            xs, _ = _norm(xs, y, mods[l], 8, 0.5, ln_g[l, 2], ln_b[l, 2], alpha, seg, n_mix, n_out=n_mix)
        else:
            xs, h = _norm(xs, y, mods[l], 8, 0.5, ln_g[l, 2], ln_b[l, 2], alpha, seg, n_mix,
                          mods_next=mods[l + 1], i_shift=0, i_scale=1)
    return xs.reshape(n_batch, seq, d)
```

```python
import functools

import jax
import jax.numpy as jnp
from jax import lax
from jax.experimental import pallas as pl
from jax.experimental.pallas import tpu as pltpu

F32 = jnp.float32
BF16 = jnp.bfloat16

GRID_W = 64
N_SUB = 3
N_MOD = 3 * N_SUB
HEAD_DIM = 128
WIN_H = 8
WIN_W = 16
ROPE_BASE = 10000.0
CHUNK = 128
GMLP_GROUP_DIM = 128
SSM_GROUP = 16
SSM_STATE = 64
N_BRANCH = 3
LN_EPS = 1e-6
NEG_INF = -1e30

LANES = 128
SUBLANES = 8
MIB = 1024 * 1024
V7X_VMEM_BYTES = 64 * MIB
VMEM_LIMIT_BYTES = V7X_VMEM_BYTES - 4 * MIB
VMEM_MARGIN_BYTES = 2 * MIB
SSM_BLOCK = 16
SSM_ROW = SSM_BLOCK * SSM_GROUP
SLOTS = LANES // SSM_GROUP
SSM_CHUNK_GROUPS = SLOTS
MOD_ROWS = SUBLANES

TM = 512
TM_ROW = 256
TN = 512
MERGE_TN = 256
ATT_ROWS = 16
ATT_KROWS = ATT_ROWS + WIN_H
ATT_SUB = 4
ATT_SUB_KROWS = ATT_SUB + WIN_H
NT_DIMS = (((1,), (1,)), ((), ()))


def _params(*sem):
    return pltpu.CompilerParams(dimension_semantics=sem, vmem_limit_bytes=VMEM_LIMIT_BYTES)


def _ln(x):
    mu = jnp.mean(x, axis=-1, keepdims=True)
    xc = x - mu
    var = jnp.mean(xc * xc, axis=-1, keepdims=True)
    return xc * lax.rsqrt(var + LN_EPS)


def _sigmoid(x):
    return 0.5 * jnp.tanh(0.5 * x) + 0.5


def _col_tile(n_cols, col_off=0, max_tn=TN):
    for tn in range(max_tn, 0, -LANES):
        if n_cols % tn == 0 and col_off % tn == 0:
            return tn
    raise ValueError(f"column group ({col_off}, {n_cols}) is not lane aligned")


def _wide_tile(k, out_bytes_per_col):
    tn = 2 * TN
    f32, bf16 = jnp.dtype(F32).itemsize, jnp.dtype(BF16).itemsize
    for buffers in (2, 1):
        need = k * tn * (buffers * f32 + bf16) + 2 * TM * k * bf16 + 2 * TM * tn * out_bytes_per_col
        if need <= VMEM_LIMIT_BYTES - VMEM_MARGIN_BYTES:
            return tn, buffers
    return TN, 2


def _seg_map(tiles_per_batch, n_batch):
    return lambda i: jnp.minimum(i // tiles_per_batch, n_batch)


def _cast_weight_once(w_ref, wb_ref):
    @pl.when(pl.program_id(1) == 0)
    def _():
        wb_ref[...] = w_ref[...].astype(BF16)


def _mods_kernel(c_ref, w_ref, b_ref, o_ref):
    s = jax.nn.silu(c_ref[...]).astype(BF16)
    o_ref[...] = jnp.dot(s, w_ref[...].astype(BF16), preferred_element_type=F32) + b_ref[...]


def _mods(c_all, w_mod, b_mod):
    depth, d, nm = w_mod.shape
    tn = _col_tile(nm, max_tn=2 * TN)
    return pl.pallas_call(
        _mods_kernel,
        name="mods",
        out_shape=jax.ShapeDtypeStruct((depth, MOD_ROWS, nm), F32),
        grid=(depth, nm // tn),
        in_specs=[pl.BlockSpec((MOD_ROWS, d), lambda l, j: (0, 0)),
                  pl.BlockSpec((None, d, tn), lambda l, j: (l, 0, j)),
                  pl.BlockSpec((None, 1, tn), lambda l, j: (l, 0, j))],
        out_specs=pl.BlockSpec((None, MOD_ROWS, tn), lambda l, j: (l, 0, j)),
        compiler_params=_params("parallel", "parallel"),
    )(c_all, w_mod, b_mod.reshape(depth, 1, nm))


def _split_specs(n_lat_tiles, d):
    return [pl.BlockSpec((TM_ROW, d), lambda i: (jnp.minimum(i, n_lat_tiles - 1), 0)),
            pl.BlockSpec((TM_ROW, d), lambda i: (jnp.maximum(i - n_lat_tiles, 0), 0))]


def _modulate_kernel(xl_ref, xc_ref, mod_ref, h_ref, *, n_lat_tiles, i_shift, i_scale):
    x = jnp.where(pl.program_id(0) < n_lat_tiles, xl_ref[...], xc_ref[...])
    h = _ln(x) * (1.0 + mod_ref[i_scale:i_scale + 1, :]) + mod_ref[i_shift:i_shift + 1, :]
    h_ref[...] = h.astype(BF16)


def _modulate(x_lat, x_ctx, mods, i_shift, i_scale, seg):
    n_lat, d = x_lat.shape
    n = n_lat + x_ctx.shape[0]
    n_lat_tiles = n_lat // TM_ROW
    return pl.pallas_call(
        functools.partial(_modulate_kernel, n_lat_tiles=n_lat_tiles, i_shift=i_shift, i_scale=i_scale),
        name="modulate",
        out_shape=jax.ShapeDtypeStruct((n, d), BF16),
        grid=(n // TM_ROW,),
        in_specs=_split_specs(n_lat_tiles, d) + [pl.BlockSpec((None, N_MOD, d), lambda i: (seg(i), 0, 0))],
        out_specs=pl.BlockSpec((TM_ROW, d), lambda i: (i, 0)),
        compiler_params=_params("parallel"),
    )(x_lat, x_ctx, mods)


def _norm_kernel(*refs, n_lat_tiles, alpha, coef, i_gate, i_shift, i_scale):
    if n_lat_tiles is None:
        x_ref, y_ref, modg_ref, g_ref, b_ref, *rest = refs
        x = x_ref[...]
    else:
        xl_ref, xc_ref, y_ref, modg_ref, g_ref, b_ref, *rest = refs
        x = jnp.where(pl.program_id(0) < n_lat_tiles, xl_ref[...], xc_ref[...])
    t = alpha * x + (coef * modg_ref[i_gate:i_gate + 1, :]) * y_ref[...].astype(F32)
    xn = _ln(t) * g_ref[...] + b_ref[...]
    if i_shift is None:
        (xo_ref,) = rest
        xo_ref[...] = xn
    else:
        modn_ref, xo_ref, h_ref = rest
        xo_ref[...] = xn
        h = _ln(xn) * (1.0 + modn_ref[i_scale:i_scale + 1, :]) + modn_ref[i_shift:i_shift + 1, :]
        h_ref[...] = h.astype(BF16)


def _norm(x, y, mods_gate, i_gate, coef, g, b, alpha, seg, n_rows, mods_next=None, i_shift=None, i_scale=None,
          n_out=None):
    split = isinstance(x, tuple)
    d = y.shape[1]
    n_out = y.shape[0] if n_out is None else n_out
    row = pl.BlockSpec((TM_ROW, d), lambda i: (i, 0))
    vec = pl.BlockSpec((1, d), lambda i: (0, 0))
    mod = pl.BlockSpec((None, N_MOD, d), lambda i: (seg(i), 0, 0))
    n_lat_tiles = x[0].shape[0] // TM_ROW if split else None
    in_specs = (_split_specs(n_lat_tiles, d) if split else [row]) + [row, mod, vec, vec]
    args = (list(x) if split else [x]) + [y, mods_gate, g.reshape(1, d), b.reshape(1, d)]
    out_shape = [jax.ShapeDtypeStruct((n_out, d), F32)]
    out_specs = [row]
    if i_shift is not None:
        in_specs.append(mod)
        args.append(mods_next)
        out_shape.append(jax.ShapeDtypeStruct((n_out, d), BF16))
        out_specs.append(row)
    out = pl.pallas_call(
        functools.partial(_norm_kernel, n_lat_tiles=n_lat_tiles, alpha=alpha, coef=coef, i_gate=i_gate,
                          i_shift=i_shift, i_scale=i_scale),
        name="norm",
        out_shape=out_shape,
        grid=(n_rows // TM_ROW,),
        in_specs=in_specs,
        out_specs=out_specs,
        compiler_params=_params("parallel"),
    )(*args)
    return out if i_shift is not None else (out[0], None)


def _ffn_in_kernel(h_ref, wg_ref, wu_ref, o_ref, wgb_ref, wub_ref):
    _cast_weight_once(wg_ref, wgb_ref)
    _cast_weight_once(wu_ref, wub_ref)
    h = h_ref[...]
    g = jnp.dot(h, wgb_ref[...], preferred_element_type=F32)
    u = jnp.dot(h, wub_ref[...], preferred_element_type=F32)
    o_ref[...] = (g * _sigmoid(g) * u).astype(o_ref.dtype)


def _ffn_in(h, w, layer, n_rows):
    n, d = h.shape
    dff = w.shape[2] // 2
    tn = _col_tile(dff)
    return pl.pallas_call(
        _ffn_in_kernel,
        name="ffn_in",
        out_shape=jax.ShapeDtypeStruct((n, dff), BF16),
        grid=(dff // tn, n_rows // TM),
        in_specs=[pl.BlockSpec((TM, d), lambda j, i: (i, 0)),
                  pl.BlockSpec((None, d, tn), lambda j, i: (layer, 0, j)),
                  pl.BlockSpec((None, d, tn), lambda j, i: (layer, 0, j + dff // tn))],
        out_specs=pl.BlockSpec((TM, tn), lambda j, i: (i, j)),
        scratch_shapes=[pltpu.VMEM((d, tn), BF16), pltpu.VMEM((d, tn), BF16)],
        compiler_params=_params("arbitrary", "arbitrary"),
    )(h, w, w)


def _mm_kernel(a_ref, w_ref, o_ref, wb_ref):
    _cast_weight_once(w_ref, wb_ref)
    o_ref[...] = jnp.dot(a_ref[...], wb_ref[...], preferred_element_type=F32).astype(o_ref.dtype)


def _mm(a, w, layer, col_off, n_cols, out_dtype, n_rows, name):
    n, k = a.shape
    max_tn, buffers = _wide_tile(k, jnp.dtype(out_dtype).itemsize)
    tn = _col_tile(n_cols, col_off, max_tn)
    joff = col_off // tn
    return pl.pallas_call(
        _mm_kernel,
        name=name,
        out_shape=jax.ShapeDtypeStruct((n, n_cols), out_dtype),
        grid=(n_cols // tn, n_rows // TM),
        in_specs=[pl.BlockSpec((TM, k), lambda j, i: (i, 0)),
                  pl.BlockSpec((None, k, tn), lambda j, i: (layer, 0, j + joff),
                               pipeline_mode=pl.Buffered(buffers))],
        out_specs=pl.BlockSpec((TM, tn), lambda j, i: (i, j)),
        scratch_shapes=[pltpu.VMEM((k, tn), BF16)],
        compiler_params=_params("arbitrary", "arbitrary"),
    )(a, w)


def _mm_rope_kernel(a_ref, w_ref, cos_ref, sin_ref, o_ref, orot_ref, wb_ref):
    _cast_weight_once(w_ref, wb_ref)
    acc = jnp.dot(a_ref[...], wb_ref[...], preferred_element_type=F32)
    o_ref[...] = acc.astype(o_ref.dtype)
    tn = acc.shape[1]
    reps = tn // HEAD_DIM
    cos = jnp.concatenate([cos_ref[...]] * reps, axis=1)
    sin = jnp.concatenate([sin_ref[...]] * reps, axis=1)
    lane = lax.broadcasted_iota(jnp.int32, acc.shape, 1)
    quarter = HEAD_DIM // 4
    first = (lane % (2 * quarter)) < quarter
    partner = jnp.where(first, pltpu.roll(acc, tn - quarter, 1), pltpu.roll(acc, quarter, 1))
    orot_ref[...] = (acc * cos + partner * sin).astype(orot_ref.dtype)


def _mm_rope(a, w, layer, col_off, n_cols, cos, sin, tiles_per_batch, n_rows):
    n, k = a.shape
    max_tn, buffers = _wide_tile(k, 2 * jnp.dtype(BF16).itemsize)
    tn = _col_tile(n_cols, col_off, max_tn)
    joff = col_off // tn
    out = jax.ShapeDtypeStruct((n, n_cols), BF16)
    tab = pl.BlockSpec((TM, HEAD_DIM), lambda j, i: (i % tiles_per_batch, 0))
    blk = pl.BlockSpec((TM, tn), lambda j, i: (i, j))
    return pl.pallas_call(
        _mm_rope_kernel,
        name="proj_rope",
        out_shape=[out, out],
        grid=(n_cols // tn, n_rows // TM),
        in_specs=[pl.BlockSpec((TM, k), lambda j, i: (i, 0)),
                  pl.BlockSpec((None, k, tn), lambda j, i: (layer, 0, j + joff),
                               pipeline_mode=pl.Buffered(buffers)),
                  tab, tab],
        out_specs=[blk, blk],
        scratch_shapes=[pltpu.VMEM((k, tn), BF16)],
        compiler_params=_params("arbitrary", "arbitrary"),
    )(a, w, cos, sin)


def _merge_kernel(h_ref, oa_ref, ob_ref, oc_ref, wga_ref, wgb_ref, wgc_ref, wa_ref, wb_ref, wc_ref, o_ref,
                  *scratch):
    weights = (wga_ref, wgb_ref, wgc_ref, wa_ref, wb_ref, wc_ref)
    for w_ref, wb16_ref in zip(weights, scratch):
        _cast_weight_once(w_ref, wb16_ref)
    gate_w, proj_w = scratch[:N_BRANCH], scratch[N_BRANCH:]
    h = h_ref[...]
    t = None
    for branch_ref, wg, wp in zip((oa_ref, ob_ref, oc_ref), gate_w, proj_w):
        gate = _sigmoid(jnp.dot(h, wg[...], preferred_element_type=F32))
        term = gate * jnp.dot(branch_ref[...], wp[...], preferred_element_type=F32)
        t = term if t is None else t + term
    o_ref[...] = t.astype(o_ref.dtype)


def _merge(h, o_a, o_b, o_c, w_in, gate_off, w_pa, w_pb, w_pc, layer, n_rows):
    n, d = h.shape
    tn = _col_tile(d, gate_off, MERGE_TN)

    def act(o):
        return pl.BlockSpec((TM, o.shape[1]), lambda j, i: (i, 0))

    def proj(w):
        return pl.BlockSpec((None, w.shape[1], tn), lambda j, i: (layer, 0, j), pipeline_mode=pl.Buffered(1))

    def gate(branch):
        joff = (gate_off + branch * d) // tn
        return pl.BlockSpec((None, d, tn), lambda j, i: (layer, 0, j + joff))

    return pl.pallas_call(
        _merge_kernel,
        name="merge",
        out_shape=jax.ShapeDtypeStruct((n, d), BF16),
        grid=(d // tn, n_rows // TM),
        in_specs=[act(h), act(o_a), act(o_b), act(o_c), gate(0), gate(1), gate(2), proj(w_pa), proj(w_pb), proj(w_pc)],
        out_specs=pl.BlockSpec((TM, tn), lambda j, i: (i, j)),
        scratch_shapes=([pltpu.VMEM((d, tn), BF16)] * N_BRANCH
                        + [pltpu.VMEM((w.shape[1], tn), BF16) for w in (w_pa, w_pb, w_pc)]),
        compiler_params=_params("arbitrary", "arbitrary"),
    )(h, o_a, o_b, o_c, w_in, w_in, w_in, w_pa, w_pb, w_pc)


def _attn_block_geometry(kind):
    half = WIN_H // 2
    if kind == 0:
        return 0, [max(i - half, 0) for i in range(ATT_ROWS)]
    if kind == 1:
        return -half, list(range(ATT_ROWS))
    return -(ATT_KROWS - ATT_ROWS), [min(half + i, ATT_KROWS - WIN_H) for i in range(ATT_ROWS)]


def _attn_sub_key_offset(kind, sub):
    _, first_key_row = _attn_block_geometry(kind)
    return min(min(first_key_row[sub * ATT_SUB:(sub + 1) * ATT_SUB]), ATT_KROWS - ATT_SUB_KROWS)


def _rpb_table_kernel(rpb_ref, o_ref):
    shape = (GRID_W, 2 * GRID_W)
    cq = lax.broadcasted_iota(jnp.int32, shape, 0)
    lane = lax.broadcasted_iota(jnp.int32, shape, 1)
    ck = lane % GRID_W
    second = lane >= GRID_W
    start = jnp.clip(cq - WIN_W // 2, 0, GRID_W - WIN_W)
    in_win = (ck >= start) & (ck < start + WIN_W)
    neg = jnp.full(shape, NEG_INF, F32)
    n_bias_rows = 2 * WIN_H - 1

    def half_tile(a, lane_off):
        row = jnp.broadcast_to(rpb_ref[a:a + 1, :], shape)
        return pltpu.roll(row, (lane_off - (WIN_W - 1)) % LANES, 1, stride=1, stride_axis=0)

    lo_half = [half_tile(a, 0) for a in range(n_bias_rows)]
    hi_half = [half_tile(a, GRID_W) for a in range(n_bias_rows)]
    for kind in range(3):
        diff, first_key_row = _attn_block_geometry(kind)
        for i in range(ATT_ROWS):
            lo = first_key_row[i]
            sub, isub = divmod(i, ATT_SUB)
            koff = _attn_sub_key_offset(kind, sub)
            for pair in range(ATT_SUB_KROWS // 2):
                j0, j1 = koff + 2 * pair, koff + 2 * pair + 1
                t0 = lo_half[j0 - i + diff + WIN_H - 1] if lo <= j0 < lo + WIN_H else neg
                t1 = hi_half[j1 - i + diff + WIN_H - 1] if lo <= j1 < lo + WIN_H else neg
                tile = jnp.where(in_win, jnp.where(second, t1, t0), neg)
                o_ref[kind, sub, isub * GRID_W:(isub + 1) * GRID_W,
                      pair * 2 * GRID_W:(pair + 1) * 2 * GRID_W] = tile


def _rpb_table(rpb):
    heads, nr, nc = rpb.shape
    rpb_pad = jnp.zeros((heads, 2 * WIN_H, LANES), F32).at[:, :nr, :nc].set(rpb)
    shape = (3, ATT_ROWS // ATT_SUB, ATT_SUB * GRID_W, ATT_SUB_KROWS * GRID_W)
    return pl.pallas_call(
        _rpb_table_kernel,
        name="rpb_table",
        out_shape=jax.ShapeDtypeStruct((heads,) + shape, F32),
        grid=(heads,),
        in_specs=[pl.BlockSpec((None, 2 * WIN_H, LANES), lambda h: (h, 0, 0))],
        out_specs=pl.BlockSpec((None,) + shape, lambda h: (h, 0, 0, 0, 0)),
        compiler_params=_params("parallel"),
    )(rpb_pad)


def _attn_kernel(q_ref, qr_ref, kr_ref, v_ref, kc_ref, vc_ref, bias_ref, o_ref, *, n_grid_rows, n_blocks, scale):
    rb = pl.program_id(2)
    kr0 = jnp.clip(rb * ATT_ROWS - WIN_H // 2, 0, n_grid_rows - ATT_KROWS)
    kc = kc_ref[...]
    vc = vc_ref[...]
    for sub in range(ATT_ROWS // ATT_SUB):
        offs = [_attn_sub_key_offset(kind, sub) for kind in range(3)]
        koff = jnp.where(rb == 0, offs[0], jnp.where(rb == n_blocks - 1, offs[2], offs[1]))
        keys = pl.ds(pl.multiple_of((kr0 + koff) * GRID_W, GRID_W), ATT_SUB_KROWS * GRID_W)
        rows = pl.ds(sub * ATT_SUB * GRID_W, ATT_SUB * GRID_W)
        s = lax.dot_general(qr_ref[rows, :], kr_ref[keys, :], NT_DIMS, preferred_element_type=F32) * scale
        s += bias_ref[sub]
        sc = lax.dot_general(q_ref[rows, :], kc, NT_DIMS, preferred_element_type=F32) * scale
        m = jnp.maximum(jnp.max(s, axis=-1, keepdims=True), jnp.max(sc, axis=-1, keepdims=True))
        p = jnp.exp(s - m)
        pc = jnp.exp(sc - m)
        denom = jnp.sum(p, axis=-1, keepdims=True) + jnp.sum(pc, axis=-1, keepdims=True)
        o = jnp.dot(p.astype(BF16), v_ref[keys, :], preferred_element_type=F32)
        o += jnp.dot(pc.astype(BF16), vc, preferred_element_type=F32)
        o_ref[rows, :] = (o / denom).astype(o_ref.dtype)


def _attention(q, q_rot, k, k_rot, v, bias, n_batch, seq, ctx_len):
    n, width = q.shape
    heads = width // HEAD_DIM
    n_grid_rows = seq // GRID_W
    tq = ATT_ROWS * GRID_W
    blocks = seq // tq
    ctx0 = n_batch * seq // ctx_len
    qspec = pl.BlockSpec((tq, HEAD_DIM), lambda b, h, r: (b * blocks + r, h))
    kspec = pl.BlockSpec((seq, HEAD_DIM), lambda b, h, r: (b, h))
    cspec = pl.BlockSpec((ctx_len, HEAD_DIM), lambda b, h, r: (ctx0 + b, h))

    def kind(r):
        return jnp.where(r == 0, 0, jnp.where(r == blocks - 1, 2, 1))

    return pl.pallas_call(
        functools.partial(_attn_kernel, n_grid_rows=n_grid_rows, n_blocks=blocks, scale=HEAD_DIM ** -0.5),
        name="attn",
        out_shape=jax.ShapeDtypeStruct((n, width), BF16),
        grid=(n_batch, heads, blocks),
        in_specs=[qspec, qspec, kspec, kspec, cspec, cspec,
                  pl.BlockSpec((None, None) + bias.shape[2:], lambda b, h, r: (h, kind(r), 0, 0, 0))],
        out_specs=qspec,
        compiler_params=_params("parallel", "parallel", "arbitrary"),
    )(q, q_rot, k_rot, v, k, v, bias)


def _ctx_attn_kernel(q_ref, k_ref, v_ref, prev_ref, o_ref, *, scale):
    del prev_ref
    s = lax.dot_general(q_ref[...], k_ref[...], NT_DIMS, preferred_element_type=F32) * scale
    m = jnp.max(s, axis=-1, keepdims=True)
    p = jnp.exp(s - m)
    denom = jnp.sum(p, axis=-1, keepdims=True)
    o = jnp.dot(p.astype(BF16), v_ref[...], preferred_element_type=F32)
    o_ref[...] = (o / denom).astype(o_ref.dtype)


def _ctx_attention(q, k, v, o_att, n_batch, seq, ctx_len):
    n, width = q.shape
    heads = width // HEAD_DIM
    ctx0 = n_batch * seq // ctx_len
    spec = pl.BlockSpec((ctx_len, HEAD_DIM), lambda b, h: (ctx0 + b, h))
    return pl.pallas_call(
        functools.partial(_ctx_attn_kernel, scale=HEAD_DIM ** -0.5),
        name="ctx_attn",
        out_shape=jax.ShapeDtypeStruct((n, width), BF16),
        grid=(n_batch, heads),
        in_specs=[spec, spec, spec, pl.BlockSpec(memory_space=pl.ANY)],
        out_specs=spec,
        input_output_aliases={3: 0},
        compiler_params=_params("parallel", "parallel"),
    )(q, k, v, o_att)


def _gmlp_kernel(z_ref, g_ref, b_ref, ws_ref, bs_ref, o_ref, *, width, chunks):
    z = jax.nn.gelu(z_ref[...])
    u = z[:, :width]
    v = (_ln(z[:, width:]) * g_ref[...] + b_ref[...]).astype(BF16)
    for c in range(chunks):
        rows = slice(c * CHUNK, (c + 1) * CHUNK)
        for g in range(width // GMLP_GROUP_DIM):
            cols = slice(g * GMLP_GROUP_DIM, (g + 1) * GMLP_GROUP_DIM)
            mixed = jnp.dot(ws_ref[g], v[rows, cols], preferred_element_type=F32) + bs_ref[:, cols]
            o_ref[rows, cols] = (u[rows, cols] * mixed).astype(o_ref.dtype)


def _gmlp(z, ln_g, ln_b, w_s, b_s, n_rows):
    n, two_w = z.shape
    width = two_w // 2
    groups = width // GMLP_GROUP_DIM
    chunks = TM // CHUNK
    bs = jnp.repeat(b_s.T, GMLP_GROUP_DIM, axis=1)
    vec = pl.BlockSpec((1, width), lambda i: (0, 0))
    return pl.pallas_call(
        functools.partial(_gmlp_kernel, width=width, chunks=chunks),
        name="gmlp",
        out_shape=jax.ShapeDtypeStruct((n, width), BF16),
        grid=(n_rows // TM,),
        in_specs=[pl.BlockSpec((TM, two_w), lambda i: (i, 0)), vec, vec,
                  pl.BlockSpec((groups, CHUNK, CHUNK), lambda i: (0, 0, 0)),
                  pl.BlockSpec((CHUNK, width), lambda i: (0, 0))],
        out_specs=pl.BlockSpec((TM, width), lambda i: (i, 0)),
        compiler_params=_params("parallel"),
    )(z, ln_g.reshape(1, width), ln_b.reshape(1, width), w_s.astype(BF16), bs)


def _s5_gen_kernel(lr_ref, li_ref, ldt_ref, bre_ref, bim_ref, cre_ref, cim_ref, m_ref, wst_ref, et_ref, a_ref):
    hi = lax.Precision.HIGHEST
    nb = SSM_BLOCK
    lane = lax.broadcasted_iota(jnp.int32, (SSM_GROUP, SSM_ROW), 1)
    state_lane = lax.broadcasted_iota(jnp.int32, (SSM_GROUP, 2 * SSM_STATE), 1)
    in_group = [state_lane < SSM_STATE, state_lane >= SSM_STATE]
    m_rows = [[None] * nb for _ in range(2)]
    for d in range(2):
        lr = lr_ref[d:d + 1, :]
        li = li_ref[d:d + 1, :]
        dt = jnp.exp(ldt_ref[d:d + 1, :])
        mag = jnp.exp(lr * dt)
        ar = mag * jnp.cos(li * dt)
        ai = mag * jnp.sin(li * dt)
        den = lr * lr + li * li
        fr = ((ar - 1.0) * lr + ai * li) / den
        fi = (ai * lr - (ar - 1.0) * li) / den
        br = bre_ref[d]
        bi = bim_ref[d]
        bbr = fr * br - fi * bi
        bbi = fr * bi + fi * br
        cr = cre_ref[d]
        ci = cim_ref[d]
        pr = [jnp.ones_like(ar)]
        pi = [jnp.zeros_like(ai)]
        for _ in range(nb):
            pr.append(pr[-1] * ar - pi[-1] * ai)
            pi.append(pr[-2] * ai + pi[-1] * ar)
        a_ref[2 * d] = pr[nb]
        a_ref[2 * d + 1] = pi[nb]
        e_state = [(t + 1) if d == 0 else (nb - t) for t in range(nb)]
        e_in = [(nb - 1 - s) if d == 0 else s for s in range(nb)]
        e_lag = [e if d == 0 else (nb - 1 - e) for e in range(nb)]
        for s in range(nb):
            wr, wi = pr[e_in[s]], pi[e_in[s]]
            er, ei = pr[e_state[s]], pi[e_state[s]]
            parts = (wr * bbr - wi * bbi, wr * bbi + wi * bbr)
            outs = (cr * er - ci * ei, -(cr * ei + ci * er))
            for g in range(2):
                rows = slice(g * SSM_ROW + s * SSM_GROUP, g * SSM_ROW + (s + 1) * SSM_GROUP)
                for part in range(2):
                    cols = slice((2 * d + part) * 2 * SSM_STATE, (2 * d + part + 1) * 2 * SSM_STATE)
                    wst_ref[rows, cols] = jnp.where(in_group[g], parts[part], 0.0).astype(wst_ref.dtype)
                    et_ref[rows, cols] = jnp.where(in_group[g], outs[part], 0.0).astype(et_ref.dtype)
        lag_r = jnp.concatenate([cr * pr[e] - ci * pi[e] for e in e_lag], axis=0)
        lag_i = jnp.concatenate([-(cr * pi[e] + ci * pr[e]) for e in e_lag], axis=0)
        for g in range(2):
            gbr = jnp.where(in_group[g], bbr, 0.0)
            gbi = jnp.where(in_group[g], bbi, 0.0)
            kbase = (lax.dot_general(gbr, lag_r, NT_DIMS, precision=hi, preferred_element_type=F32)
                     + lax.dot_general(gbi, lag_i, NT_DIMS, precision=hi, preferred_element_type=F32))
            for s in range(nb):
                if d == 0:
                    sh = s * SSM_GROUP
                    m_rows[g][s] = kbase if sh == 0 else jnp.where(lane >= sh, pltpu.roll(kbase, sh, 1), 0.0)
                else:
                    sh = (nb - 1 - s) * SSM_GROUP
                    row = kbase if sh == 0 else jnp.where(lane < SSM_ROW - sh, pltpu.roll(kbase, SSM_ROW - sh, 1), 0.0)
                    m_rows[g][s] = m_rows[g][s] + row
    zeros = jnp.zeros((SSM_GROUP, SSM_ROW), m_ref.dtype)
    for g in range(2):
        for s in range(nb):
            rows = slice(g * SSM_ROW + s * SSM_GROUP, g * SSM_ROW + (s + 1) * SSM_GROUP)
            m_ref[rows, g * SSM_ROW:(g + 1) * SSM_ROW] = m_rows[g][s].astype(m_ref.dtype)
            m_ref[rows, (1 - g) * SSM_ROW:(2 - g) * SSM_ROW] = zeros


def _s5_gen(lam_re, lam_im, log_dt, b_re, b_im, c_re, c_im):
    _, groups, p = lam_re.shape
    pairs = groups // 2

    def vec_pairs(x):
        return jnp.swapaxes(x.reshape(2, pairs, 2 * p), 0, 1)

    def mat_pairs(x):
        x = x.reshape(2, pairs, 2, SSM_GROUP, p)
        return jnp.transpose(x, (1, 0, 3, 2, 4)).reshape(pairs, 2, SSM_GROUP, 2 * p)

    ldt = jnp.broadcast_to(log_dt[:, :, None], lam_re.shape)
    vec = pl.BlockSpec((None, 2, 2 * p), lambda g: (g, 0, 0))
    mat = pl.BlockSpec((None, 2, SSM_GROUP, 2 * p), lambda g: (g, 0, 0, 0))
    return pl.pallas_call(
        _s5_gen_kernel,
        name="s5_gen",
        out_shape=[jax.ShapeDtypeStruct((pairs, 2 * SSM_ROW, 2 * SSM_ROW), BF16),
                   jax.ShapeDtypeStruct((pairs, 2 * SSM_ROW, 4 * 2 * p), BF16),
                   jax.ShapeDtypeStruct((pairs, 2 * SSM_ROW, 4 * 2 * p), BF16),
                   jax.ShapeDtypeStruct((pairs, 4, 1, 2 * p), F32)],
        grid=(pairs,),
        in_specs=[vec, vec, vec, mat, mat, mat, mat],
        out_specs=[pl.BlockSpec((None, 2 * SSM_ROW, 2 * SSM_ROW), lambda g: (g, 0, 0)),
                   pl.BlockSpec((None, 2 * SSM_ROW, 4 * 2 * p), lambda g: (g, 0, 0)),
                   pl.BlockSpec((None, 2 * SSM_ROW, 4 * 2 * p), lambda g: (g, 0, 0)),
                   pl.BlockSpec((None, 4, 1, 2 * p), lambda g: (g, 0, 0, 0))],
        compiler_params=_params("parallel"),
    )(vec_pairs(lam_re), vec_pairs(lam_im), vec_pairs(ldt),
      mat_pairs(jnp.swapaxes(b_re, 2, 3)), mat_pairs(jnp.swapaxes(b_im, 2, 3)), mat_pairs(c_re), mat_pairs(c_im))


def _slot_transpose(xs):
    n = SLOTS
    slot = lax.broadcasted_iota(jnp.int32, xs[0].shape, 1) // SSM_GROUP
    diag = []
    for r in range(n):
        w = xs[r % n]
        for a in range(1, n):
            w = jnp.where(slot == a, xs[(a + r) % n], w)
        diag.append(w if r == 0 else pltpu.roll(w, r * SSM_GROUP, 1))
    ys = []
    for a in range(n):
        y = diag[0]
        for r in range(1, n):
            y = jnp.where(slot == (a + r) % n, diag[r], y)
        ys.append(y)
    return ys


def _s5_state_kernel(x_ref, wst_ref, u_ref, *state_refs):
    n_blk = u_ref.shape[0]
    xs = [x_ref[pl.ds(s, n_blk, stride=SSM_BLOCK), :] for s in range(SSM_BLOCK)]
    by_col = [_slot_transpose(xs[col * SLOTS:(col + 1) * SLOTS]) for col in range(SSM_BLOCK // SLOTS)]
    pieces = [by_col[col][gl] for gl in range(SSM_CHUNK_GROUPS) for col in range(SSM_BLOCK // SLOTS)]
    cols_per_pair = 2 * SSM_ROW // LANES
    for pair in range(SSM_CHUNK_GROUPS // 2):
        u = jnp.concatenate(pieces[pair * cols_per_pair:(pair + 1) * cols_per_pair], axis=1).astype(BF16)
        u_ref[:, pair * 2 * SSM_ROW:(pair + 1) * 2 * SSM_ROW] = u
        states = jnp.dot(u, wst_ref[pair], preferred_element_type=F32)
        for k, o_ref in enumerate(state_refs):
            o_ref[:, pair * 2 * SSM_STATE:(pair + 1) * 2 * SSM_STATE] = states[:, k * 2 * SSM_STATE:
                                                                                (k + 1) * 2 * SSM_STATE]


def _s5_row_tile(n_rows):
    return n_rows // 2 if n_rows % 32 == 0 else n_rows


def _s5_states(us, wst_pair, groups):
    n_rows = us.shape[0] // SSM_BLOCK
    chunks = groups // SSM_CHUNK_GROUPS
    pairs = SSM_CHUNK_GROUPS // 2
    rt = _s5_row_tile(n_rows)
    state = jax.ShapeDtypeStruct((n_rows, groups * SSM_STATE), F32)
    sspec = pl.BlockSpec((rt, SSM_CHUNK_GROUPS * SSM_STATE), lambda ch, i: (i, ch))
    uspec = pl.BlockSpec((rt, SSM_CHUNK_GROUPS * SSM_ROW), lambda ch, i: (i, ch))
    return pl.pallas_call(
        _s5_state_kernel,
        name="s5_states",
        out_shape=[jax.ShapeDtypeStruct((n_rows, groups * SSM_ROW), BF16)] + [state] * 4,
        grid=(chunks, n_rows // rt),
        in_specs=[pl.BlockSpec((rt * SSM_BLOCK, LANES), lambda ch, i: (i, ch)),
                  pl.BlockSpec((pairs,) + wst_pair.shape[1:], lambda ch, i: (ch, 0, 0))],
        out_specs=[uspec] + [sspec] * 4,
        compiler_params=_params("parallel", "parallel"),
    )(us, wst_pair)


def _s5_scan_kernel(sfr_ref, sfi_ref, sbr_ref, sbi_ref, a_ref, hfr_ref, hfi_ref, hbr_ref, hbi_ref,
                    *, n_batch, n_lat_blocks, n_ctx_blocks):
    afr, afi, abr, abi = (a_ref[k:k + 1, :] for k in range(4))
    zero = jnp.zeros_like(afr)
    nl, nc = n_lat_blocks, n_ctx_blocks

    def step(t, carry):
        new = []
        in_ctx = t < nc
        for b in range(n_batch):
            lat0, ctx0 = b * nl, n_batch * nl + b * nc
            fr, fi, br, bi = carry[4 * b:4 * b + 4]
            rf = pl.ds(jnp.where(in_ctx, ctx0 + t, lat0 + t - nc), 1)
            rb = pl.ds(jnp.where(in_ctx, ctx0 + nc - 1 - t, lat0 + nl - 1 - (t - nc)), 1)
            hfr_ref[rf, :] = fr
            hfi_ref[rf, :] = fi
            hbr_ref[rb, :] = br
            hbi_ref[rb, :] = bi
            new += [afr * fr - afi * fi + sfr_ref[rf, :], afr * fi + afi * fr + sfi_ref[rf, :],
                    abr * br - abi * bi + sbr_ref[rb, :], abr * bi + abi * br + sbi_ref[rb, :]]
        return tuple(new)

    lax.fori_loop(0, nl + nc, step, (zero,) * (4 * n_batch))


def _s5_scan(s_parts, a_rows, n_batch, n_lat_blocks, n_ctx_blocks):
    rows, lanes = s_parts[0].shape
    tl = min(lanes, 4 * LANES)
    spec = pl.BlockSpec((rows, tl), lambda j: (0, j))
    return pl.pallas_call(
        functools.partial(_s5_scan_kernel, n_batch=n_batch, n_lat_blocks=n_lat_blocks, n_ctx_blocks=n_ctx_blocks),
        name="s5_scan",
        out_shape=[jax.ShapeDtypeStruct((rows, lanes), F32)] * 4,
        grid=(lanes // tl,),
        in_specs=[spec] * 4 + [pl.BlockSpec((4, tl), lambda j: (0, j))],
        out_specs=[spec] * 4,
        compiler_params=_params("parallel"),
    )(*s_parts, a_rows)


def _s5_out_kernel(u_ref, m_ref, et_ref, hfr_ref, hfi_ref, hbr_ref, hbi_ref, y_ref):
    ys = []
    for pair in range(SSM_CHUNK_GROUPS // 2):
        y = jnp.dot(u_ref[:, pair * 2 * SSM_ROW:(pair + 1) * 2 * SSM_ROW], m_ref[pair], preferred_element_type=F32)
        h = jnp.concatenate([h_ref[:, pair * 2 * SSM_STATE:(pair + 1) * 2 * SSM_STATE]
                             for h_ref in (hfr_ref, hfi_ref, hbr_ref, hbi_ref)], axis=1).astype(BF16)
        ys.append(y + lax.dot_general(h, et_ref[pair], NT_DIMS, preferred_element_type=F32))
    n_blk = ys[0].shape[0]
    for col in range(SSM_BLOCK // SLOTS):
        by_group = [ys[gl // 2][:, (gl % 2) * SSM_ROW + col * LANES:(gl % 2) * SSM_ROW + (col + 1) * LANES]
                    for gl in range(SSM_CHUNK_GROUPS)]
        for k, y in enumerate(_slot_transpose(by_group)):
            y_ref[pl.ds(col * SLOTS + k, n_blk, stride=SSM_BLOCK), :] = y


def _s5_outputs(u, m_pair, et_pair, h_parts, groups):
    n_rows = u.shape[0]
    chunks = groups // SSM_CHUNK_GROUPS
    pairs = SSM_CHUNK_GROUPS // 2
    rt = _s5_row_tile(n_rows)
    hspec = pl.BlockSpec((rt, SSM_CHUNK_GROUPS * SSM_STATE), lambda ch, i: (i, ch))
    return pl.pallas_call(
        _s5_out_kernel,
        name="s5_out",
        out_shape=jax.ShapeDtypeStruct((n_rows * SSM_BLOCK, groups * SSM_GROUP), F32),
        grid=(chunks, n_rows // rt),
        in_specs=[pl.BlockSpec((rt, SSM_CHUNK_GROUPS * SSM_ROW), lambda ch, i: (i, ch)),
                  pl.BlockSpec((pairs,) + m_pair.shape[1:], lambda ch, i: (ch, 0, 0)),
                  pl.BlockSpec((pairs,) + et_pair.shape[1:], lambda ch, i: (ch, 0, 0)),
                  hspec, hspec, hspec, hspec],
        out_specs=pl.BlockSpec((rt * SSM_BLOCK, LANES), lambda ch, i: (i, ch)),
        compiler_params=_params("parallel", "parallel"),
    )(u, m_pair, et_pair, *h_parts)


def _s5_glu_kernel(ys_ref, us_ref, d_ref, w_ref, b_ref, o_ref, wb_ref):
    @pl.when(pl.program_id(0) == 0)
    def _():
        wb_ref[...] = w_ref[...].astype(BF16)

    y = jax.nn.gelu(ys_ref[...] + d_ref[...] * us_ref[...])
    gate = _sigmoid(jnp.dot(y.astype(BF16), wb_ref[...], preferred_element_type=F32) + b_ref[...])
    o_ref[...] = (y * gate).astype(o_ref.dtype)


def _s5_glu(y_s, us, d_skip, w_glu, layer, b_glu, n_rows):
    n, w = us.shape
    row = pl.BlockSpec((TM, w), lambda i: (i, 0))
    vec = pl.BlockSpec((1, w), lambda i: (0, 0))
    return pl.pallas_call(
        _s5_glu_kernel,
        name="s5_glu",
        out_shape=jax.ShapeDtypeStruct((n, w), BF16),
        grid=(n_rows // TM,),
        in_specs=[row, row, vec, pl.BlockSpec((None, w, w), lambda i: (layer, 0, 0)), vec],
        out_specs=row,
        scratch_shapes=[pltpu.VMEM((w, w), BF16)],
        compiler_params=_params("arbitrary"),
    )(y_s, us, d_skip.reshape(1, w), w_glu, b_glu.reshape(1, w))


def _s5(us, n_batch, seq, ctx_len, lam_re, lam_im, log_dt, b_re, b_im, c_re, c_im):
    groups = us.shape[1] // SSM_GROUP
    m_pair, wst_pair, et_pair, a_pair = _s5_gen(lam_re, lam_im, log_dt, b_re, b_im, c_re, c_im)
    a_rows = jnp.transpose(a_pair[:, :, 0, :], (1, 0, 2)).reshape(4, groups * SSM_STATE)
    u, *s_parts = _s5_states(us, wst_pair, groups)
    h_parts = _s5_scan(s_parts, a_rows, n_batch, seq // SSM_BLOCK, ctx_len // SSM_BLOCK)
    return _s5_outputs(u, m_pair, et_pair, h_parts, groups)


def _rope_tables(seq):
    quarter = HEAD_DIM // 4
    n_grid_rows = seq // GRID_W
    inv = ROPE_BASE ** (-jnp.arange(quarter, dtype=F32) / quarter)
    shape = (n_grid_rows, GRID_W, quarter)
    ang_r = jnp.arange(n_grid_rows, dtype=F32)[:, None] * inv[None, :]
    ang_c = jnp.arange(GRID_W, dtype=F32)[:, None] * inv[None, :]
    cos_r, sin_r = (jnp.broadcast_to(f(ang_r)[:, None, :], shape) for f in (jnp.cos, jnp.sin))
    cos_c, sin_c = (jnp.broadcast_to(f(ang_c)[None, :, :], shape) for f in (jnp.cos, jnp.sin))
    cos = jnp.concatenate([cos_r, cos_r, cos_c, cos_c], axis=-1).reshape(seq, HEAD_DIM)
    sin = jnp.concatenate([-sin_r, sin_r, -sin_c, sin_c], axis=-1).reshape(seq, HEAD_DIM)
    return cos, sin


def kernel(x, c, ctx, c_ctx, w_mod, b_mod, ln_g, ln_b, ffn1_w_in, ffn1_w_out, ffn2_w_in, ffn2_w_out, w_in, rpb, gm_ln_g, gm_ln_b, gm_ws, gm_bs, ssm_lam_re, ssm_lam_im, ssm_log_dt, ssm_b_re, ssm_b_im, ssm_c_re, ssm_c_im, ssm_d, ssm_w_glu, ssm_b_glu, w_pa, w_pb, w_pc, w_o):
    n_batch, seq, d = x.shape
    ctx_len = ctx.shape[1]
    depth = w_mod.shape[0]
    att_w = w_pb.shape[1]
    gm_w = w_pa.shape[1]
    ssm_w = w_pc.shape[1]
    assert ssm_w % LANES == 0 and seq % SSM_BLOCK == 0 and ctx_len % SSM_BLOCK == 0
    q_off, k_off, v_off = 0, att_w, 2 * att_w
    gm_off = 3 * att_w
    ssm_off = gm_off + 2 * gm_w
    gate_off = ssm_off + ssm_w
    alpha = (2.0 * depth) ** 0.25
    n_lat = n_batch * seq
    n_all = n_lat + n_batch * ctx_len
    assert seq % TM == 0 and (n_batch * ctx_len) % TM == 0 and seq % (ATT_ROWS * GRID_W) == 0
    assert seq // GRID_W >= ATT_KROWS and ctx_len % CHUNK == 0 and n_lat % ctx_len == 0
    assert n_batch + 1 <= MOD_ROWS and rpb.shape[2:] == (2 * WIN_H - 1, 2 * WIN_W - 1)

    seg = _seg_map(seq // TM_ROW, n_batch)
    tiles_per_batch = seq // TM

    x_lat, x_ctx = x.reshape(n_lat, d), ctx.reshape(n_batch * ctx_len, d)
    c_all = jnp.zeros((MOD_ROWS, d), F32).at[:n_batch].set(c).at[n_batch].set(c_ctx)
    mods = _mods(c_all, w_mod, b_mod)[:, :n_batch + 1].reshape(depth, n_batch + 1, N_MOD, d)
    cos, sin = _rope_tables(seq)

    xs = (x_lat, x_ctx)
    h = _modulate(x_lat, x_ctx, mods[0], 0, 1, seg)
    for l in range(depth):
        last = l == depth - 1
        n_mix = n_lat if last else n_all

        y = _mm(_ffn_in(h, ffn1_w_in, l, n_all), ffn1_w_out, l, 0, d, BF16, n_all, "ffn_out")
        xs, h = _norm(xs, y, mods[l], 2, 0.5, ln_g[l, 0], ln_b[l, 0], alpha, seg, n_all,
                      mods_next=mods[l], i_shift=3, i_scale=4)

        q, q_rot = _mm_rope(h, w_in, l, q_off, att_w, cos, sin, tiles_per_batch, n_mix)
        k, k_rot = _mm_rope(h, w_in, l, k_off, att_w, cos, sin, tiles_per_batch, n_all)
        v = _mm(h, w_in, l, v_off, att_w, BF16, n_all, "proj_v")
        z = _mm(h, w_in, l, gm_off, 2 * gm_w, F32, n_mix, "proj_z")
        us = _mm(h, w_in, l, ssm_off, ssm_w, F32, n_all, "proj_us")

        o_att = _attention(q, q_rot, k, k_rot, v, _rpb_table(rpb[l]), n_batch, seq, ctx_len)
        if not last:
            o_att = _ctx_attention(q, k, v, o_att, n_batch, seq, ctx_len)
        o_gm = _gmlp(z, gm_ln_g[l], gm_ln_b[l], gm_ws[l], gm_bs[l], n_mix)
        y_s = _s5(us, n_batch, seq, ctx_len, ssm_lam_re[l], ssm_lam_im[l], ssm_log_dt[l],
                  ssm_b_re[l], ssm_b_im[l], ssm_c_re[l], ssm_c_im[l])
        o_ssm = _s5_glu(y_s, us, ssm_d[l], ssm_w_glu, l, ssm_b_glu[l], n_mix)
        merged = _merge(h, o_gm, o_att, o_ssm, w_in, gate_off, w_pa, w_pb, w_pc, l, n_mix)
        y = _mm(merged, w_o, l, 0, d, BF16, n_mix, "out_proj")
        xs, h = _norm(xs, y, mods[l], 5, 1.0, ln_g[l, 1], ln_b[l, 1], alpha, seg, n_mix,
                      mods_next=mods[l], i_shift=6, i_scale=7)

        y = _mm(_ffn_in(h, ffn2_w_in, l, n_mix), ffn2_w_out, l, 0, d, BF16, n_mix, "ffn_out")
        if last:
            xs, _ = _norm(xs, y, mods[l], 8, 0.5, ln_g[l, 2], ln_b[l, 2], alpha, seg, n_mix, n_out=n_mix)
        else:
            xs, h = _norm(xs, y, mods[l], 8, 0.5, ln_g[l, 2], ln_b[l, 2], alpha, seg, n_mix,
                          mods_next=mods[l + 1], i_shift=0, i_scale=1)
    return xs.reshape(n_batch, seq, d)
```

```python
import functools

import jax
import jax.numpy as jnp
from jax import lax
from jax.experimental import pallas as pl
from jax.experimental.pallas import tpu as pltpu

F32 = jnp.float32
BF16 = jnp.bfloat16

GRID_W = 64
N_SUB = 3
N_MOD = 3 * N_SUB
HEAD_DIM = 128
WIN_H = 8
WIN_W = 16
ROPE_BASE = 10000.0
CHUNK = 128
GMLP_GROUP_DIM = 128
SSM_GROUP = 16
SSM_STATE = 64
N_BRANCH = 3
LN_EPS = 1e-6
NEG_INF = -1e30

LANES = 128
SUBLANES = 8
MIB = 1024 * 1024
V7X_VMEM_BYTES = 64 * MIB
VMEM_LIMIT_BYTES = V7X_VMEM_BYTES - 4 * MIB
VMEM_MARGIN_BYTES = 2 * MIB
SSM_BLOCK = 16
SSM_ROW = SSM_BLOCK * SSM_GROUP
SLOTS = LANES // SSM_GROUP
SSM_CHUNK_GROUPS = SLOTS
MOD_ROWS = SUBLANES

TM = 512
TM_ROW = 256
TN = 512
MERGE_TN = 256
ATT_ROWS = 16
ATT_KROWS = ATT_ROWS + WIN_H
ATT_SUB = 4
ATT_SUB_KROWS = ATT_SUB + WIN_H
NT_DIMS = (((1,), (1,)), ((), ()))


def _params(*sem):
    return pltpu.CompilerParams(dimension_semantics=sem, vmem_limit_bytes=VMEM_LIMIT_BYTES)


def _ln(x):
    mu = jnp.mean(x, axis=-1, keepdims=True)
    xc = x - mu
    var = jnp.mean(xc * xc, axis=-1, keepdims=True)
    return xc * lax.rsqrt(var + LN_EPS)


def _sigmoid(x):
    return 0.5 * jnp.tanh(0.5 * x) + 0.5


def _col_tile(n_cols, col_off=0, max_tn=TN):
    for tn in range(max_tn, 0, -LANES):
        if n_cols % tn == 0 and col_off % tn == 0:
            return tn
    raise ValueError(f"column group ({col_off}, {n_cols}) is not lane aligned")


def _wide_tile(k, out_bytes_per_col):
    tn = 2 * TN
    f32, bf16 = jnp.dtype(F32).itemsize, jnp.dtype(BF16).itemsize
    for buffers in (2, 1):
        need = k * tn * (buffers * f32 + bf16) + 2 * TM * k * bf16 + 2 * TM * tn * out_bytes_per_col
        if need <= VMEM_LIMIT_BYTES - VMEM_MARGIN_BYTES:
            return tn, buffers
    return TN, 2


def _seg_map(tiles_per_batch, n_batch):
    return lambda i: jnp.minimum(i // tiles_per_batch, n_batch)


def _cast_weight_once(w_ref, wb_ref):
    @pl.when(pl.program_id(1) == 0)
    def _():
        wb_ref[...] = w_ref[...].astype(BF16)


def _mods_kernel(c_ref, w_ref, b_ref, o_ref):
    s = jax.nn.silu(c_ref[...]).astype(BF16)
    o_ref[...] = jnp.dot(s, w_ref[...].astype(BF16), preferred_element_type=F32) + b_ref[...]


def _mods(c_all, w_mod, b_mod):
    depth, d, nm = w_mod.shape
    tn = _col_tile(nm, max_tn=2 * TN)
    return pl.pallas_call(
        _mods_kernel,
        name="mods",
        out_shape=jax.ShapeDtypeStruct((depth, MOD_ROWS, nm), F32),
        grid=(depth, nm // tn),
        in_specs=[pl.BlockSpec((MOD_ROWS, d), lambda l, j: (0, 0)),
                  pl.BlockSpec((None, d, tn), lambda l, j: (l, 0, j)),
                  pl.BlockSpec((None, 1, tn), lambda l, j: (l, 0, j))],
        out_specs=pl.BlockSpec((None, MOD_ROWS, tn), lambda l, j: (l, 0, j)),
        compiler_params=_params("parallel", "parallel"),
    )(c_all, w_mod, b_mod.reshape(depth, 1, nm))


def _split_specs(n_lat_tiles, d):
    return [pl.BlockSpec((TM_ROW, d), lambda i: (jnp.minimum(i, n_lat_tiles - 1), 0)),
            pl.BlockSpec((TM_ROW, d), lambda i: (jnp.maximum(i - n_lat_tiles, 0), 0))]


def _modulate_kernel(xl_ref, xc_ref, mod_ref, h_ref, *, n_lat_tiles, i_shift, i_scale):
    x = jnp.where(pl.program_id(0) < n_lat_tiles, xl_ref[...], xc_ref[...])
    h = _ln(x) * (1.0 + mod_ref[i_scale:i_scale + 1, :]) + mod_ref[i_shift:i_shift + 1, :]
    h_ref[...] = h.astype(BF16)


def _modulate(x_lat, x_ctx, mods, i_shift, i_scale, seg):
    n_lat, d = x_lat.shape
    n = n_lat + x_ctx.shape[0]
    n_lat_tiles = n_lat // TM_ROW
    return pl.pallas_call(
        functools.partial(_modulate_kernel, n_lat_tiles=n_lat_tiles, i_shift=i_shift, i_scale=i_scale),
        name="modulate",
        out_shape=jax.ShapeDtypeStruct((n, d), BF16),
        grid=(n // TM_ROW,),
        in_specs=_split_specs(n_lat_tiles, d) + [pl.BlockSpec((None, N_MOD, d), lambda i: (seg(i), 0, 0))],
        out_specs=pl.BlockSpec((TM_ROW, d), lambda i: (i, 0)),
        compiler_params=_params("parallel"),
    )(x_lat, x_ctx, mods)


def _norm_kernel(*refs, n_lat_tiles, alpha, coef, i_gate, i_shift, i_scale):
    if n_lat_tiles is None:
        x_ref, y_ref, modg_ref, g_ref, b_ref, *rest = refs
        x = x_ref[...]
    else:
        xl_ref, xc_ref, y_ref, modg_ref, g_ref, b_ref, *rest = refs
        x = jnp.where(pl.program_id(0) < n_lat_tiles, xl_ref[...], xc_ref[...])
    t = alpha * x + (coef * modg_ref[i_gate:i_gate + 1, :]) * y_ref[...].astype(F32)
    xn = _ln(t) * g_ref[...] + b_ref[...]
    if i_shift is None:
        (xo_ref,) = rest
        xo_ref[...] = xn
    else:
        modn_ref, xo_ref, h_ref = rest
        xo_ref[...] = xn
        h = _ln(xn) * (1.0 + modn_ref[i_scale:i_scale + 1, :]) + modn_ref[i_shift:i_shift + 1, :]
        h_ref[...] = h.astype(BF16)


def _norm(x, y, mods_gate, i_gate, coef, g, b, alpha, seg, n_rows, mods_next=None, i_shift=None, i_scale=None,
          n_out=None):
    split = isinstance(x, tuple)
    d = y.shape[1]
    n_out = y.shape[0] if n_out is None else n_out
    row = pl.BlockSpec((TM_ROW, d), lambda i: (i, 0))
    vec = pl.BlockSpec((1, d), lambda i: (0, 0))
    mod = pl.BlockSpec((None, N_MOD, d), lambda i: (seg(i), 0, 0))
    n_lat_tiles = x[0].shape[0] // TM_ROW if split else None
    in_specs = (_split_specs(n_lat_tiles, d) if split else [row]) + [row, mod, vec, vec]
    args = (list(x) if split else [x]) + [y, mods_gate, g.reshape(1, d), b.reshape(1, d)]
    out_shape = [jax.ShapeDtypeStruct((n_out, d), F32)]
    out_specs = [row]
    if i_shift is not None:
        in_specs.append(mod)
        args.append(mods_next)
        out_shape.append(jax.ShapeDtypeStruct((n_out, d), BF16))
        out_specs.append(row)
    out = pl.pallas_call(
        functools.partial(_norm_kernel, n_lat_tiles=n_lat_tiles, alpha=alpha, coef=coef, i_gate=i_gate,
                          i_shift=i_shift, i_scale=i_scale),
        name="norm",
        out_shape=out_shape,
        grid=(n_rows // TM_ROW,),
        in_specs=in_specs,
        out_specs=out_specs,
        compiler_params=_params("parallel"),
    )(*args)
    return out if i_shift is not None else (out[0], None)


def _ffn_in_kernel(h_ref, wg_ref, wu_ref, o_ref, wgb_ref, wub_ref):
    _cast_weight_once(wg_ref, wgb_ref)
    _cast_weight_once(wu_ref, wub_ref)
    h = h_ref[...]
    g = jnp.dot(h, wgb_ref[...], preferred_element_type=F32)
    u = jnp.dot(h, wub_ref[...], preferred_element_type=F32)
    o_ref[...] = (g * _sigmoid(g) * u).astype(o_ref.dtype)


def _ffn_in(h, w, layer, n_rows):
    n, d = h.shape
    dff = w.shape[2] // 2
    tn = _col_tile(dff)
    return pl.pallas_call(
        _ffn_in_kernel,
        name="ffn_in",
        out_shape=jax.ShapeDtypeStruct((n, dff), BF16),
        grid=(dff // tn, n_rows // TM),
        in_specs=[pl.BlockSpec((TM, d), lambda j, i: (i, 0)),
                  pl.BlockSpec((None, d, tn), lambda j, i: (layer, 0, j)),
                  pl.BlockSpec((None, d, tn), lambda j, i: (layer, 0, j + dff // tn))],
        out_specs=pl.BlockSpec((TM, tn), lambda j, i: (i, j)),
        scratch_shapes=[pltpu.VMEM((d, tn), BF16), pltpu.VMEM((d, tn), BF16)],
        compiler_params=_params("arbitrary", "arbitrary"),
    )(h, w, w)


def _mm_kernel(a_ref, w_ref, o_ref, wb_ref):
    _cast_weight_once(w_ref, wb_ref)
    o_ref[...] = jnp.dot(a_ref[...], wb_ref[...], preferred_element_type=F32).astype(o_ref.dtype)


def _mm(a, w, layer, col_off, n_cols, out_dtype, n_rows, name):
    n, k = a.shape
    max_tn, buffers = _wide_tile(k, jnp.dtype(out_dtype).itemsize)
    tn = _col_tile(n_cols, col_off, max_tn)
    joff = col_off // tn
    return pl.pallas_call(
        _mm_kernel,
        name=name,
        out_shape=jax.ShapeDtypeStruct((n, n_cols), out_dtype),
        grid=(n_cols // tn, n_rows // TM),
        in_specs=[pl.BlockSpec((TM, k), lambda j, i: (i, 0)),
                  pl.BlockSpec((None, k, tn), lambda j, i: (layer, 0, j + joff),
                               pipeline_mode=pl.Buffered(buffers))],
        out_specs=pl.BlockSpec((TM, tn), lambda j, i: (i, j)),
        scratch_shapes=[pltpu.VMEM((k, tn), BF16)],
        compiler_params=_params("arbitrary", "arbitrary"),
    )(a, w)


def _mm_rope_kernel(a_ref, w_ref, cos_ref, sin_ref, o_ref, orot_ref, wb_ref):
    _cast_weight_once(w_ref, wb_ref)
    acc = jnp.dot(a_ref[...], wb_ref[...], preferred_element_type=F32)
    o_ref[...] = acc.astype(o_ref.dtype)
    tn = acc.shape[1]
    reps = tn // HEAD_DIM
    cos = jnp.concatenate([cos_ref[...]] * reps, axis=1)
    sin = jnp.concatenate([sin_ref[...]] * reps, axis=1)
    lane = lax.broadcasted_iota(jnp.int32, acc.shape, 1)
    quarter = HEAD_DIM // 4
    first = (lane % (2 * quarter)) < quarter
    partner = jnp.where(first, pltpu.roll(acc, tn - quarter, 1), pltpu.roll(acc, quarter, 1))
    orot_ref[...] = (acc * cos + partner * sin).astype(orot_ref.dtype)


def _mm_rope(a, w, layer, col_off, n_cols, cos, sin, tiles_per_batch, n_rows):
    n, k = a.shape
    max_tn, buffers = _wide_tile(k, 2 * jnp.dtype(BF16).itemsize)
    tn = _col_tile(n_cols, col_off, max_tn)
    joff = col_off // tn
    out = jax.ShapeDtypeStruct((n, n_cols), BF16)
    tab = pl.BlockSpec((TM, HEAD_DIM), lambda j, i: (i % tiles_per_batch, 0))
    blk = pl.BlockSpec((TM, tn), lambda j, i: (i, j))
    return pl.pallas_call(
        _mm_rope_kernel,
        name="proj_rope",
        out_shape=[out, out],
        grid=(n_cols // tn, n_rows // TM),
        in_specs=[pl.BlockSpec((TM, k), lambda j, i: (i, 0)),
                  pl.BlockSpec((None, k, tn), lambda j, i: (layer, 0, j + joff),
                               pipeline_mode=pl.Buffered(buffers)),
                  tab, tab],
        out_specs=[blk, blk],
        scratch_shapes=[pltpu.VMEM((k, tn), BF16)],
        compiler_params=_params("arbitrary", "arbitrary"),
    )(a, w, cos, sin)


def _merge_kernel(h_ref, oa_ref, ob_ref, oc_ref, wga_ref, wgb_ref, wgc_ref, wa_ref, wb_ref, wc_ref, o_ref,
                  *scratch):
    weights = (wga_ref, wgb_ref, wgc_ref, wa_ref, wb_ref, wc_ref)
    for w_ref, wb16_ref in zip(weights, scratch):
        _cast_weight_once(w_ref, wb16_ref)
    gate_w, proj_w = scratch[:N_BRANCH], scratch[N_BRANCH:]
    h = h_ref[...]
    t = None
    for branch_ref, wg, wp in zip((oa_ref, ob_ref, oc_ref), gate_w, proj_w):
        gate = _sigmoid(jnp.dot(h, wg[...], preferred_element_type=F32))
        term = gate * jnp.dot(branch_ref[...], wp[...], preferred_element_type=F32)
        t = term if t is None else t + term
    o_ref[...] = t.astype(o_ref.dtype)


def _merge(h, o_a, o_b, o_c, w_in, gate_off, w_pa, w_pb, w_pc, layer, n_rows):
    n, d = h.shape
    tn = _col_tile(d, gate_off, MERGE_TN)

    def act(o):
        return pl.BlockSpec((TM, o.shape[1]), lambda j, i: (i, 0))

    def proj(w):
        return pl.BlockSpec((None, w.shape[1], tn), lambda j, i: (layer, 0, j), pipeline_mode=pl.Buffered(1))

    def gate(branch):
        joff = (gate_off + branch * d) // tn
        return pl.BlockSpec((None, d, tn), lambda j, i: (layer, 0, j + joff))

    return pl.pallas_call(
        _merge_kernel,
        name="merge",
        out_shape=jax.ShapeDtypeStruct((n, d), BF16),
        grid=(d // tn, n_rows // TM),
        in_specs=[act(h), act(o_a), act(o_b), act(o_c), gate(0), gate(1), gate(2), proj(w_pa), proj(w_pb), proj(w_pc)],
        out_specs=pl.BlockSpec((TM, tn), lambda j, i: (i, j)),
        scratch_shapes=([pltpu.VMEM((d, tn), BF16)] * N_BRANCH
                        + [pltpu.VMEM((w.shape[1], tn), BF16) for w in (w_pa, w_pb, w_pc)]),
        compiler_params=_params("arbitrary", "arbitrary"),
    )(h, o_a, o_b, o_c, w_in, w_in, w_in, w_pa, w_pb, w_pc)


def _attn_block_geometry(kind):
    half = WIN_H // 2
    if kind == 0:
        return 0, [max(i - half, 0) for i in range(ATT_ROWS)]
    if kind == 1:
        return -half, list(range(ATT_ROWS))
    return -(ATT_KROWS - ATT_ROWS), [min(half + i, ATT_KROWS - WIN_H) for i in range(ATT_ROWS)]


def _attn_sub_key_offset(kind, sub):
    _, first_key_row = _attn_block_geometry(kind)
    return min(min(first_key_row[sub * ATT_SUB:(sub + 1) * ATT_SUB]), ATT_KROWS - ATT_SUB_KROWS)


def _attn_sub_pattern(kind, sub):
    diff, first_key_row = _attn_block_geometry(kind)
    koff = _attn_sub_key_offset(kind, sub)
    rows = range(sub * ATT_SUB, (sub + 1) * ATT_SUB)
    return tuple((first_key_row[i] - koff, koff - i + diff + WIN_H - 1) for i in rows)


def _attn_patterns():
    patterns, index = [], []
    for kind in range(3):
        index.append([])
        for sub in range(ATT_ROWS // ATT_SUB):
            p = _attn_sub_pattern(kind, sub)
            if p not in patterns:
                patterns.append(p)
            index[kind].append(patterns.index(p))
    return patterns, index


def _rpb_table_kernel(rpb_ref, o_ref):
    shape = (GRID_W, 2 * GRID_W)
    cq = lax.broadcasted_iota(jnp.int32, shape, 0)
    lane = lax.broadcasted_iota(jnp.int32, shape, 1)
    ck = lane % GRID_W
    second = lane >= GRID_W
    start = jnp.clip(cq - WIN_W // 2, 0, GRID_W - WIN_W)
    in_win = (ck >= start) & (ck < start + WIN_W)
    neg = jnp.full(shape, NEG_INF, F32)
    n_bias_rows = 2 * WIN_H - 1

    def half_tile(a, lane_off):
        row = jnp.broadcast_to(rpb_ref[a:a + 1, :], shape)
        return pltpu.roll(row, (lane_off - (WIN_W - 1)) % LANES, 1, stride=1, stride_axis=0)

    lo_half = [half_tile(a, 0) for a in range(n_bias_rows)]
    hi_half = [half_tile(a, GRID_W) for a in range(n_bias_rows)]
    for p, pattern in enumerate(_attn_patterns()[0]):
        for isub, (lo, a0) in enumerate(pattern):
            for pair in range(ATT_SUB_KROWS // 2):
                j0, j1 = 2 * pair, 2 * pair + 1
                t0 = lo_half[j0 + a0] if lo <= j0 < lo + WIN_H else neg
                t1 = hi_half[j1 + a0] if lo <= j1 < lo + WIN_H else neg
                tile = jnp.where(in_win, jnp.where(second, t1, t0), neg)
                o_ref[p, isub * GRID_W:(isub + 1) * GRID_W, pair * 2 * GRID_W:(pair + 1) * 2 * GRID_W] = tile


def _rpb_table(rpb):
    heads, nr, nc = rpb.shape
    rpb_pad = jnp.zeros((heads, 2 * WIN_H, LANES), F32).at[:, :nr, :nc].set(rpb)
    shape = (len(_attn_patterns()[0]), ATT_SUB * GRID_W, ATT_SUB_KROWS * GRID_W)
    return pl.pallas_call(
        _rpb_table_kernel,
        name="rpb_table",
        out_shape=jax.ShapeDtypeStruct((heads,) + shape, F32),
        grid=(heads,),
        in_specs=[pl.BlockSpec((None, 2 * WIN_H, LANES), lambda h: (h, 0, 0))],
        out_specs=pl.BlockSpec((None,) + shape, lambda h: (h, 0, 0, 0)),
        compiler_params=_params("parallel"),
    )(rpb_pad)


def _attn_kernel(q_ref, qr_ref, kr_ref, v_ref, kc_ref, vc_ref, bias_ref, o_ref, *, n_grid_rows, n_blocks, scale):
    rb = pl.program_id(2)
    kr0 = jnp.clip(rb * ATT_ROWS - WIN_H // 2, 0, n_grid_rows - ATT_KROWS)
    kc = kc_ref[...]
    vc = vc_ref[...]
    table_index = _attn_patterns()[1]

    def by_kind(values):
        return jnp.where(rb == 0, values[0], jnp.where(rb == n_blocks - 1, values[2], values[1]))

    for sub in range(ATT_ROWS // ATT_SUB):
        koff = by_kind([_attn_sub_key_offset(kind, sub) for kind in range(3)])
        keys = pl.ds(pl.multiple_of((kr0 + koff) * GRID_W, GRID_W), ATT_SUB_KROWS * GRID_W)
        rows = pl.ds(sub * ATT_SUB * GRID_W, ATT_SUB * GRID_W)
        s = lax.dot_general(qr_ref[rows, :], kr_ref[keys, :], NT_DIMS, preferred_element_type=F32) * scale
        s += bias_ref[by_kind([table_index[kind][sub] for kind in range(3)])]
        sc = lax.dot_general(q_ref[rows, :], kc, NT_DIMS, preferred_element_type=F32) * scale
        m = jnp.maximum(jnp.max(s, axis=-1, keepdims=True), jnp.max(sc, axis=-1, keepdims=True))
        p = jnp.exp(s - m)
        pc = jnp.exp(sc - m)
        denom = jnp.sum(p, axis=-1, keepdims=True) + jnp.sum(pc, axis=-1, keepdims=True)
        o = jnp.dot(p.astype(BF16), v_ref[keys, :], preferred_element_type=F32)
        o += jnp.dot(pc.astype(BF16), vc, preferred_element_type=F32)
        o_ref[rows, :] = (o / denom).astype(o_ref.dtype)


def _attention(q, q_rot, k, k_rot, v, bias, n_batch, seq, ctx_len):
    n, width = q.shape
    heads = width // HEAD_DIM
    n_grid_rows = seq // GRID_W
    tq = ATT_ROWS * GRID_W
    blocks = seq // tq
    ctx0 = n_batch * seq // ctx_len
    qspec = pl.BlockSpec((tq, HEAD_DIM), lambda b, h, r: (b * blocks + r, h))
    kspec = pl.BlockSpec((seq, HEAD_DIM), lambda b, h, r: (b, h))
    cspec = pl.BlockSpec((ctx_len, HEAD_DIM), lambda b, h, r: (ctx0 + b, h))

    return pl.pallas_call(
        functools.partial(_attn_kernel, n_grid_rows=n_grid_rows, n_blocks=blocks, scale=HEAD_DIM ** -0.5),
        name="attn",
        out_shape=jax.ShapeDtypeStruct((n, width), BF16),
        grid=(n_batch, heads, blocks),
        in_specs=[qspec, qspec, kspec, kspec, cspec, cspec,
                  pl.BlockSpec((None,) + bias.shape[1:], lambda b, h, r: (h, 0, 0, 0))],
        out_specs=qspec,
        compiler_params=_params("parallel", "parallel", "arbitrary"),
    )(q, q_rot, k_rot, v, k, v, bias)


def _ctx_attn_kernel(q_ref, k_ref, v_ref, prev_ref, o_ref, *, scale):
    del prev_ref
    s = lax.dot_general(q_ref[...], k_ref[...], NT_DIMS, preferred_element_type=F32) * scale
    m = jnp.max(s, axis=-1, keepdims=True)
    p = jnp.exp(s - m)
    denom = jnp.sum(p, axis=-1, keepdims=True)
    o = jnp.dot(p.astype(BF16), v_ref[...], preferred_element_type=F32)
    o_ref[...] = (o / denom).astype(o_ref.dtype)


def _ctx_attention(q, k, v, o_att, n_batch, seq, ctx_len):
    n, width = q.shape
    heads = width // HEAD_DIM
    ctx0 = n_batch * seq // ctx_len
    spec = pl.BlockSpec((ctx_len, HEAD_DIM), lambda b, h: (ctx0 + b, h))
    return pl.pallas_call(
        functools.partial(_ctx_attn_kernel, scale=HEAD_DIM ** -0.5),
        name="ctx_attn",
        out_shape=jax.ShapeDtypeStruct((n, width), BF16),
        grid=(n_batch, heads),
        in_specs=[spec, spec, spec, pl.BlockSpec(memory_space=pl.ANY)],
        out_specs=spec,
        input_output_aliases={3: 0},
        compiler_params=_params("parallel", "parallel"),
    )(q, k, v, o_att)


def _gmlp_kernel(z_ref, g_ref, b_ref, ws_ref, bs_ref, o_ref, *, width, chunks):
    z = jax.nn.gelu(z_ref[...])
    u = z[:, :width]
    v = (_ln(z[:, width:]) * g_ref[...] + b_ref[...]).astype(BF16)
    for c in range(chunks):
        rows = slice(c * CHUNK, (c + 1) * CHUNK)
        for g in range(width // GMLP_GROUP_DIM):
            cols = slice(g * GMLP_GROUP_DIM, (g + 1) * GMLP_GROUP_DIM)
            mixed = jnp.dot(ws_ref[g], v[rows, cols], preferred_element_type=F32) + bs_ref[:, cols]
            o_ref[rows, cols] = (u[rows, cols] * mixed).astype(o_ref.dtype)


def _gmlp(z, ln_g, ln_b, w_s, b_s, n_rows):
    n, two_w = z.shape
    width = two_w // 2
    groups = width // GMLP_GROUP_DIM
    chunks = TM // CHUNK
    bs = jnp.repeat(b_s.T, GMLP_GROUP_DIM, axis=1)
    vec = pl.BlockSpec((1, width), lambda i: (0, 0))
    return pl.pallas_call(
        functools.partial(_gmlp_kernel, width=width, chunks=chunks),
        name="gmlp",
        out_shape=jax.ShapeDtypeStruct((n, width), BF16),
        grid=(n_rows // TM,),
        in_specs=[pl.BlockSpec((TM, two_w), lambda i: (i, 0)), vec, vec,
                  pl.BlockSpec((groups, CHUNK, CHUNK), lambda i: (0, 0, 0)),
                  pl.BlockSpec((CHUNK, width), lambda i: (0, 0))],
        out_specs=pl.BlockSpec((TM, width), lambda i: (i, 0)),
        compiler_params=_params("parallel"),
    )(z, ln_g.reshape(1, width), ln_b.reshape(1, width), w_s.astype(BF16), bs)


def _s5_gen_kernel(lr_ref, li_ref, ldt_ref, bre_ref, bim_ref, cre_ref, cim_ref, m_ref, wst_ref, et_ref, a_ref):
    hi = lax.Precision.HIGHEST
    nb = SSM_BLOCK
    lane = lax.broadcasted_iota(jnp.int32, (SSM_GROUP, SSM_ROW), 1)
    state_lane = lax.broadcasted_iota(jnp.int32, (SSM_GROUP, 2 * SSM_STATE), 1)
    in_group = [state_lane < SSM_STATE, state_lane >= SSM_STATE]
    m_rows = [[None] * nb for _ in range(2)]
    for d in range(2):
        lr = lr_ref[d:d + 1, :]
        li = li_ref[d:d + 1, :]
        dt = jnp.exp(ldt_ref[d:d + 1, :])
        mag = jnp.exp(lr * dt)
        ar = mag * jnp.cos(li * dt)
        ai = mag * jnp.sin(li * dt)
        den = lr * lr + li * li
        fr = ((ar - 1.0) * lr + ai * li) / den
        fi = (ai * lr - (ar - 1.0) * li) / den
        br = bre_ref[d]
        bi = bim_ref[d]
        bbr = fr * br - fi * bi
        bbi = fr * bi + fi * br
        cr = cre_ref[d]
        ci = cim_ref[d]
        pr = [jnp.ones_like(ar)]
        pi = [jnp.zeros_like(ai)]
        for _ in range(nb):
            pr.append(pr[-1] * ar - pi[-1] * ai)
            pi.append(pr[-2] * ai + pi[-1] * ar)
        a_ref[2 * d] = pr[nb]
        a_ref[2 * d + 1] = pi[nb]
        e_state = [(t + 1) if d == 0 else (nb - t) for t in range(nb)]
        e_in = [(nb - 1 - s) if d == 0 else s for s in range(nb)]
        e_lag = [e if d == 0 else (nb - 1 - e) for e in range(nb)]
        for s in range(nb):
            wr, wi = pr[e_in[s]], pi[e_in[s]]
            er, ei = pr[e_state[s]], pi[e_state[s]]
            parts = (wr * bbr - wi * bbi, wr * bbi + wi * bbr)
            outs = (cr * er - ci * ei, -(cr * ei + ci * er))
            for g in range(2):
                rows = slice(g * SSM_ROW + s * SSM_GROUP, g * SSM_ROW + (s + 1) * SSM_GROUP)
                for part in range(2):
                    cols = slice((2 * d + part) * 2 * SSM_STATE, (2 * d + part + 1) * 2 * SSM_STATE)
                    wst_ref[rows, cols] = jnp.where(in_group[g], parts[part], 0.0).astype(wst_ref.dtype)
                    et_ref[rows, cols] = jnp.where(in_group[g], outs[part], 0.0).astype(et_ref.dtype)
        lag_r = jnp.concatenate([cr * pr[e] - ci * pi[e] for e in e_lag], axis=0)
        lag_i = jnp.concatenate([-(cr * pi[e] + ci * pr[e]) for e in e_lag], axis=0)
        for g in range(2):
            gbr = jnp.where(in_group[g], bbr, 0.0)
            gbi = jnp.where(in_group[g], bbi, 0.0)
            kbase = (lax.dot_general(gbr, lag_r, NT_DIMS, precision=hi, preferred_element_type=F32)
                     + lax.dot_general(gbi, lag_i, NT_DIMS, precision=hi, preferred_element_type=F32))
            for s in range(nb):
                if d == 0:
                    sh = s * SSM_GROUP
                    m_rows[g][s] = kbase if sh == 0 else jnp.where(lane >= sh, pltpu.roll(kbase, sh, 1), 0.0)
                else:
                    sh = (nb - 1 - s) * SSM_GROUP
                    row = kbase if sh == 0 else jnp.where(lane < SSM_ROW - sh, pltpu.roll(kbase, SSM_ROW - sh, 1), 0.0)
                    m_rows[g][s] = m_rows[g][s] + row
    zeros = jnp.zeros((SSM_GROUP, SSM_ROW), m_ref.dtype)
    for g in range(2):
        for s in range(nb):
            rows = slice(g * SSM_ROW + s * SSM_GROUP, g * SSM_ROW + (s + 1) * SSM_GROUP)
            m_ref[rows, g * SSM_ROW:(g + 1) * SSM_ROW] = m_rows[g][s].astype(m_ref.dtype)
            m_ref[rows, (1 - g) * SSM_ROW:(2 - g) * SSM_ROW] = zeros


def _s5_gen(lam_re, lam_im, log_dt, b_re, b_im, c_re, c_im):
    _, groups, p = lam_re.shape
    pairs = groups // 2

    def vec_pairs(x):
        return jnp.swapaxes(x.reshape(2, pairs, 2 * p), 0, 1)

    def mat_pairs(x):
        x = x.reshape(2, pairs, 2, SSM_GROUP, p)
        return jnp.transpose(x, (1, 0, 3, 2, 4)).reshape(pairs, 2, SSM_GROUP, 2 * p)

    ldt = jnp.broadcast_to(log_dt[:, :, None], lam_re.shape)
    vec = pl.BlockSpec((None, 2, 2 * p), lambda g: (g, 0, 0))
    mat = pl.BlockSpec((None, 2, SSM_GROUP, 2 * p), lambda g: (g, 0, 0, 0))
    return pl.pallas_call(
        _s5_gen_kernel,
        name="s5_gen",
        out_shape=[jax.ShapeDtypeStruct((pairs, 2 * SSM_ROW, 2 * SSM_ROW), BF16),
                   jax.ShapeDtypeStruct((pairs, 2 * SSM_ROW, 4 * 2 * p), BF16),
                   jax.ShapeDtypeStruct((pairs, 2 * SSM_ROW, 4 * 2 * p), BF16),
                   jax.ShapeDtypeStruct((pairs, 4, 1, 2 * p), F32)],
        grid=(pairs,),
        in_specs=[vec, vec, vec, mat, mat, mat, mat],
        out_specs=[pl.BlockSpec((None, 2 * SSM_ROW, 2 * SSM_ROW), lambda g: (g, 0, 0)),
                   pl.BlockSpec((None, 2 * SSM_ROW, 4 * 2 * p), lambda g: (g, 0, 0)),
                   pl.BlockSpec((None, 2 * SSM_ROW, 4 * 2 * p), lambda g: (g, 0, 0)),
                   pl.BlockSpec((None, 4, 1, 2 * p), lambda g: (g, 0, 0, 0))],
        compiler_params=_params("parallel"),
    )(vec_pairs(lam_re), vec_pairs(lam_im), vec_pairs(ldt),
      mat_pairs(jnp.swapaxes(b_re, 2, 3)), mat_pairs(jnp.swapaxes(b_im, 2, 3)), mat_pairs(c_re), mat_pairs(c_im))


def _slot_transpose(xs):
    n = SLOTS
    slot = lax.broadcasted_iota(jnp.int32, xs[0].shape, 1) // SSM_GROUP
    diag = []
    for r in range(n):
        w = xs[r % n]
        for a in range(1, n):
            w = jnp.where(slot == a, xs[(a + r) % n], w)
        diag.append(w if r == 0 else pltpu.roll(w, r * SSM_GROUP, 1))
    ys = []
    for a in range(n):
        y = diag[0]
        for r in range(1, n):
            y = jnp.where(slot == (a + r) % n, diag[r], y)
        ys.append(y)
    return ys


def _s5_state_kernel(x_ref, wst_ref, u_ref, *state_refs):
    n_blk = u_ref.shape[0]
    xs = [x_ref[pl.ds(s, n_blk, stride=SSM_BLOCK), :] for s in range(SSM_BLOCK)]
    by_col = [_slot_transpose(xs[col * SLOTS:(col + 1) * SLOTS]) for col in range(SSM_BLOCK // SLOTS)]
    pieces = [by_col[col][gl] for gl in range(SSM_CHUNK_GROUPS) for col in range(SSM_BLOCK // SLOTS)]
    cols_per_pair = 2 * SSM_ROW // LANES
    for pair in range(SSM_CHUNK_GROUPS // 2):
        u = jnp.concatenate(pieces[pair * cols_per_pair:(pair + 1) * cols_per_pair], axis=1).astype(BF16)
        u_ref[:, pair * 2 * SSM_ROW:(pair + 1) * 2 * SSM_ROW] = u
        states = jnp.dot(u, wst_ref[pair], preferred_element_type=F32)
        for k, o_ref in enumerate(state_refs):
            o_ref[:, pair * 2 * SSM_STATE:(pair + 1) * 2 * SSM_STATE] = states[:, k * 2 * SSM_STATE:
                                                                                (k + 1) * 2 * SSM_STATE]


def _s5_row_tile(n_rows):
    return n_rows // 2 if n_rows % 32 == 0 else n_rows


def _s5_states(us, wst_pair, groups):
    n_rows = us.shape[0] // SSM_BLOCK
    chunks = groups // SSM_CHUNK_GROUPS
    pairs = SSM_CHUNK_GROUPS // 2
    rt = _s5_row_tile(n_rows)
    state = jax.ShapeDtypeStruct((n_rows, groups * SSM_STATE), F32)
    sspec = pl.BlockSpec((rt, SSM_CHUNK_GROUPS * SSM_STATE), lambda ch, i: (i, ch))
    uspec = pl.BlockSpec((rt, SSM_CHUNK_GROUPS * SSM_ROW), lambda ch, i: (i, ch))
    return pl.pallas_call(
        _s5_state_kernel,
        name="s5_states",
        out_shape=[jax.ShapeDtypeStruct((n_rows, groups * SSM_ROW), BF16)] + [state] * 4,
        grid=(chunks, n_rows // rt),
        in_specs=[pl.BlockSpec((rt * SSM_BLOCK, LANES), lambda ch, i: (i, ch)),
                  pl.BlockSpec((pairs,) + wst_pair.shape[1:], lambda ch, i: (ch, 0, 0))],
        out_specs=[uspec] + [sspec] * 4,
        compiler_params=_params("parallel", "parallel"),
    )(us, wst_pair)


def _s5_scan_kernel(sfr_ref, sfi_ref, sbr_ref, sbi_ref, a_ref, hfr_ref, hfi_ref, hbr_ref, hbi_ref,
                    *, n_batch, n_lat_blocks, n_ctx_blocks):
    afr, afi, abr, abi = (a_ref[k:k + 1, :] for k in range(4))
    zero = jnp.zeros_like(afr)
    nl, nc = n_lat_blocks, n_ctx_blocks

    def step(t, carry):
        new = []
        in_ctx = t < nc
        for b in range(n_batch):
            lat0, ctx0 = b * nl, n_batch * nl + b * nc
            fr, fi, br, bi = carry[4 * b:4 * b + 4]
            rf = pl.ds(jnp.where(in_ctx, ctx0 + t, lat0 + t - nc), 1)
            rb = pl.ds(jnp.where(in_ctx, ctx0 + nc - 1 - t, lat0 + nl - 1 - (t - nc)), 1)
            hfr_ref[rf, :] = fr
            hfi_ref[rf, :] = fi
            hbr_ref[rb, :] = br
            hbi_ref[rb, :] = bi
            new += [afr * fr - afi * fi + sfr_ref[rf, :], afr * fi + afi * fr + sfi_ref[rf, :],
                    abr * br - abi * bi + sbr_ref[rb, :], abr * bi + abi * br + sbi_ref[rb, :]]
        return tuple(new)

    lax.fori_loop(0, nl + nc, step, (zero,) * (4 * n_batch))


def _s5_scan(s_parts, a_rows, n_batch, n_lat_blocks, n_ctx_blocks):
    rows, lanes = s_parts[0].shape
    tl = min(lanes, 4 * LANES)
    spec = pl.BlockSpec((rows, tl), lambda j: (0, j))
    return pl.pallas_call(
        functools.partial(_s5_scan_kernel, n_batch=n_batch, n_lat_blocks=n_lat_blocks, n_ctx_blocks=n_ctx_blocks),
        name="s5_scan",
        out_shape=[jax.ShapeDtypeStruct((rows, lanes), F32)] * 4,
        grid=(lanes // tl,),
        in_specs=[spec] * 4 + [pl.BlockSpec((4, tl), lambda j: (0, j))],
        out_specs=[spec] * 4,
        compiler_params=_params("parallel"),
    )(*s_parts, a_rows)


def _s5_out_kernel(u_ref, m_ref, et_ref, hfr_ref, hfi_ref, hbr_ref, hbi_ref, y_ref):
    ys = []
    for pair in range(SSM_CHUNK_GROUPS // 2):
        y = jnp.dot(u_ref[:, pair * 2 * SSM_ROW:(pair + 1) * 2 * SSM_ROW], m_ref[pair], preferred_element_type=F32)
        h = jnp.concatenate([h_ref[:, pair * 2 * SSM_STATE:(pair + 1) * 2 * SSM_STATE]
                             for h_ref in (hfr_ref, hfi_ref, hbr_ref, hbi_ref)], axis=1).astype(BF16)
        ys.append(y + lax.dot_general(h, et_ref[pair], NT_DIMS, preferred_element_type=F32))
    n_blk = ys[0].shape[0]
    for col in range(SSM_BLOCK // SLOTS):
        by_group = [ys[gl // 2][:, (gl % 2) * SSM_ROW + col * LANES:(gl % 2) * SSM_ROW + (col + 1) * LANES]
                    for gl in range(SSM_CHUNK_GROUPS)]
        for k, y in enumerate(_slot_transpose(by_group)):
            y_ref[pl.ds(col * SLOTS + k, n_blk, stride=SSM_BLOCK), :] = y


def _s5_outputs(u, m_pair, et_pair, h_parts, groups):
    n_rows = u.shape[0]
    chunks = groups // SSM_CHUNK_GROUPS
    pairs = SSM_CHUNK_GROUPS // 2
    rt = _s5_row_tile(n_rows)
    hspec = pl.BlockSpec((rt, SSM_CHUNK_GROUPS * SSM_STATE), lambda ch, i: (i, ch))
    return pl.pallas_call(
        _s5_out_kernel,
        name="s5_out",
        out_shape=jax.ShapeDtypeStruct((n_rows * SSM_BLOCK, groups * SSM_GROUP), F32),
        grid=(chunks, n_rows // rt),
        in_specs=[pl.BlockSpec((rt, SSM_CHUNK_GROUPS * SSM_ROW), lambda ch, i: (i, ch)),
                  pl.BlockSpec((pairs,) + m_pair.shape[1:], lambda ch, i: (ch, 0, 0)),
                  pl.BlockSpec((pairs,) + et_pair.shape[1:], lambda ch, i: (ch, 0, 0)),
                  hspec, hspec, hspec, hspec],
        out_specs=pl.BlockSpec((rt * SSM_BLOCK, LANES), lambda ch, i: (i, ch)),
        compiler_params=_params("parallel", "parallel"),
    )(u, m_pair, et_pair, *h_parts)


def _s5_glu_kernel(ys_ref, us_ref, d_ref, w_ref, b_ref, o_ref, wb_ref):
    @pl.when(pl.program_id(0) == 0)
    def _():
        wb_ref[...] = w_ref[...].astype(BF16)

    y = jax.nn.gelu(ys_ref[...] + d_ref[...] * us_ref[...])
    gate = _sigmoid(jnp.dot(y.astype(BF16), wb_ref[...], preferred_element_type=F32) + b_ref[...])
    o_ref[...] = (y * gate).astype(o_ref.dtype)


def _s5_glu(y_s, us, d_skip, w_glu, layer, b_glu, n_rows):
    n, w = us.shape
    row = pl.BlockSpec((TM, w), lambda i: (i, 0))
    vec = pl.BlockSpec((1, w), lambda i: (0, 0))
    return pl.pallas_call(
        _s5_glu_kernel,
        name="s5_glu",
        out_shape=jax.ShapeDtypeStruct((n, w), BF16),
        grid=(n_rows // TM,),
        in_specs=[row, row, vec, pl.BlockSpec((None, w, w), lambda i: (layer, 0, 0)), vec],
        out_specs=row,
        scratch_shapes=[pltpu.VMEM((w, w), BF16)],
        compiler_params=_params("arbitrary"),
    )(y_s, us, d_skip.reshape(1, w), w_glu, b_glu.reshape(1, w))


def _s5(us, n_batch, seq, ctx_len, lam_re, lam_im, log_dt, b_re, b_im, c_re, c_im):
    groups = us.shape[1] // SSM_GROUP
    m_pair, wst_pair, et_pair, a_pair = _s5_gen(lam_re, lam_im, log_dt, b_re, b_im, c_re, c_im)
    a_rows = jnp.transpose(a_pair[:, :, 0, :], (1, 0, 2)).reshape(4, groups * SSM_STATE)
    u, *s_parts = _s5_states(us, wst_pair, groups)
    h_parts = _s5_scan(s_parts, a_rows, n_batch, seq // SSM_BLOCK, ctx_len // SSM_BLOCK)
    return _s5_outputs(u, m_pair, et_pair, h_parts, groups)


def _rope_tables(seq):
    quarter = HEAD_DIM // 4
    n_grid_rows = seq // GRID_W
    inv = ROPE_BASE ** (-jnp.arange(quarter, dtype=F32) / quarter)
    shape = (n_grid_rows, GRID_W, quarter)
    ang_r = jnp.arange(n_grid_rows, dtype=F32)[:, None] * inv[None, :]
    ang_c = jnp.arange(GRID_W, dtype=F32)[:, None] * inv[None, :]
    cos_r, sin_r = (jnp.broadcast_to(f(ang_r)[:, None, :], shape) for f in (jnp.cos, jnp.sin))
    cos_c, sin_c = (jnp.broadcast_to(f(ang_c)[None, :, :], shape) for f in (jnp.cos, jnp.sin))
    cos = jnp.concatenate([cos_r, cos_r, cos_c, cos_c], axis=-1).reshape(seq, HEAD_DIM)
    sin = jnp.concatenate([-sin_r, sin_r, -sin_c, sin_c], axis=-1).reshape(seq, HEAD_DIM)
    return cos, sin


def kernel(x, c, ctx, c_ctx, w_mod, b_mod, ln_g, ln_b, ffn1_w_in, ffn1_w_out, ffn2_w_in, ffn2_w_out, w_in, rpb, gm_ln_g, gm_ln_b, gm_ws, gm_bs, ssm_lam_re, ssm_lam_im, ssm_log_dt, ssm_b_re, ssm_b_im, ssm_c_re, ssm_c_im, ssm_d, ssm_w_glu, ssm_b_glu, w_pa, w_pb, w_pc, w_o):
    n_batch, seq, d = x.shape
    ctx_len = ctx.shape[1]
    depth = w_mod.shape[0]
    att_w = w_pb.shape[1]
    gm_w = w_pa.shape[1]
    ssm_w = w_pc.shape[1]
    assert ssm_w % LANES == 0 and seq % SSM_BLOCK == 0 and ctx_len % SSM_BLOCK == 0
    q_off, k_off, v_off = 0, att_w, 2 * att_w
    gm_off = 3 * att_w
    ssm_off = gm_off + 2 * gm_w
    gate_off = ssm_off + ssm_w
    alpha = (2.0 * depth) ** 0.25
    n_lat = n_batch * seq
    n_all = n_lat + n_batch * ctx_len
    assert seq % TM == 0 and (n_batch * ctx_len) % TM == 0 and seq % (ATT_ROWS * GRID_W) == 0
    assert seq // GRID_W >= ATT_KROWS and ctx_len % CHUNK == 0 and n_lat % ctx_len == 0
    assert n_batch + 1 <= MOD_ROWS and rpb.shape[2:] == (2 * WIN_H - 1, 2 * WIN_W - 1)

    seg = _seg_map(seq // TM_ROW, n_batch)
    tiles_per_batch = seq // TM

    x_lat, x_ctx = x.reshape(n_lat, d), ctx.reshape(n_batch * ctx_len, d)
    c_all = jnp.zeros((MOD_ROWS, d), F32).at[:n_batch].set(c).at[n_batch].set(c_ctx)
    mods = _mods(c_all, w_mod, b_mod)[:, :n_batch + 1].reshape(depth, n_batch + 1, N_MOD, d)
    cos, sin = _rope_tables(seq)

    xs = (x_lat, x_ctx)
    h = _modulate(x_lat, x_ctx, mods[0], 0, 1, seg)
    for l in range(depth):
        last = l == depth - 1
        n_mix = n_lat if last else n_all

        y = _mm(_ffn_in(h, ffn1_w_in, l, n_all), ffn1_w_out, l, 0, d, BF16, n_all, "ffn_out")
        xs, h = _norm(xs, y, mods[l], 2, 0.5, ln_g[l, 0], ln_b[l, 0], alpha, seg, n_all,
                      mods_next=mods[l], i_shift=3, i_scale=4)

        q, q_rot = _mm_rope(h, w_in, l, q_off, att_w, cos, sin, tiles_per_batch, n_mix)
        k, k_rot = _mm_rope(h, w_in, l, k_off, att_w, cos, sin, tiles_per_batch, n_all)
        v = _mm(h, w_in, l, v_off, att_w, BF16, n_all, "proj_v")
        z = _mm(h, w_in, l, gm_off, 2 * gm_w, F32, n_mix, "proj_z")
        us = _mm(h, w_in, l, ssm_off, ssm_w, F32, n_all, "proj_us")

        o_att = _attention(q, q_rot, k, k_rot, v, _rpb_table(rpb[l]), n_batch, seq, ctx_len)
        if not last:
            o_att = _ctx_attention(q, k, v, o_att, n_batch, seq, ctx_len)
        o_gm = _gmlp(z, gm_ln_g[l], gm_ln_b[l], gm_ws[l], gm_bs[l], n_mix)
        y_s = _s5(us, n_batch, seq, ctx_len, ssm_lam_re[l], ssm_lam_im[l], ssm_log_dt[l],
                  ssm_b_re[l], ssm_b_im[l], ssm_c_re[l], ssm_c_im[l])
        o_ssm = _s5_glu(y_s, us, ssm_d[l], ssm_w_glu, l, ssm_b_glu[l], n_mix)
        merged = _merge(h, o_gm, o_att, o_ssm, w_in, gate_off, w_pa, w_pb, w_pc, l, n_mix)
        y = _mm(merged, w_o, l, 0, d, BF16, n_mix, "out_proj")
        xs, h = _norm(xs, y, mods[l], 5, 1.0, ln_g[l, 1], ln_b[l, 1], alpha, seg, n_mix,
                      mods_next=mods[l], i_shift=6, i_scale=7)

        y = _mm(_ffn_in(h, ffn2_w_in, l, n_mix), ffn2_w_out, l, 0, d, BF16, n_mix, "ffn_out")
        if last:
            xs, _ = _norm(xs, y, mods[l], 8, 0.5, ln_g[l, 2], ln_b[l, 2], alpha, seg, n_mix, n_out=n_mix)
        else:
            xs, h = _norm(xs, y, mods[l], 8, 0.5, ln_g[l, 2], ln_b[l, 2], alpha, seg, n_mix,
                          mods_next=mods[l + 1], i_shift=0, i_scale=1)
    return xs.reshape(n_batch, seq, d)
```

```python
import functools

import jax
import jax.numpy as jnp
from jax import lax
from jax.experimental import pallas as pl
from jax.experimental.pallas import tpu as pltpu

F32 = jnp.float32
BF16 = jnp.bfloat16

GRID_W = 64
N_SUB = 3
N_MOD = 3 * N_SUB
HEAD_DIM = 128
WIN_H = 8
WIN_W = 16
ROPE_BASE = 10000.0
CHUNK = 128
GMLP_GROUP_DIM = 128
SSM_GROUP = 16
SSM_STATE = 64
N_BRANCH = 3
LN_EPS = 1e-6
NEG_INF = -1e30

LANES = 128
SUBLANES = 8
MIB = 1024 * 1024
V7X_VMEM_BYTES = 64 * MIB
VMEM_LIMIT_BYTES = V7X_VMEM_BYTES - 4 * MIB
VMEM_MARGIN_BYTES = 2 * MIB
SSM_BLOCK = 16
SSM_ROW = SSM_BLOCK * SSM_GROUP
SLOTS = LANES // SSM_GROUP
SSM_CHUNK_GROUPS = SLOTS
MOD_ROWS = SUBLANES

TM = 512
TM_ROW = 256
TN = 512
MERGE_TN = 256
ATT_ROWS = 32
ATT_KROWS = ATT_ROWS + WIN_H
ATT_SUB = 4
ATT_SUB_KROWS = ATT_SUB + WIN_H
NT_DIMS = (((1,), (1,)), ((), ()))


def _params(*sem):
    return pltpu.CompilerParams(dimension_semantics=sem, vmem_limit_bytes=VMEM_LIMIT_BYTES)


def _ln(x):
    mu = jnp.mean(x, axis=-1, keepdims=True)
    xc = x - mu
    var = jnp.mean(xc * xc, axis=-1, keepdims=True)
    return xc * lax.rsqrt(var + LN_EPS)


def _sigmoid(x):
    return 0.5 * jnp.tanh(0.5 * x) + 0.5


def _col_tile(n_cols, col_off=0, max_tn=TN):
    for tn in range(max_tn, 0, -LANES):
        if n_cols % tn == 0 and col_off % tn == 0:
            return tn
    raise ValueError(f"column group ({col_off}, {n_cols}) is not lane aligned")


def _wide_tile(k, out_bytes_per_col):
    tn = 2 * TN
    f32, bf16 = jnp.dtype(F32).itemsize, jnp.dtype(BF16).itemsize
    for buffers in (2, 1):
        need = k * tn * (buffers * f32 + bf16) + 2 * TM * k * bf16 + 2 * TM * tn * out_bytes_per_col
        if need <= VMEM_LIMIT_BYTES - VMEM_MARGIN_BYTES:
            return tn, buffers
    return TN, 2


def _seg_map(tiles_per_batch, n_batch):
    return lambda i: jnp.minimum(i // tiles_per_batch, n_batch)


def _cast_weight_once(w_ref, wb_ref):
    @pl.when(pl.program_id(1) == 0)
    def _():
        wb_ref[...] = w_ref[...].astype(BF16)


def _mods_kernel(c_ref, w_ref, b_ref, o_ref):
    s = jax.nn.silu(c_ref[...]).astype(BF16)
    o_ref[...] = jnp.dot(s, w_ref[...].astype(BF16), preferred_element_type=F32) + b_ref[...]


def _mods(c_all, w_mod, b_mod):
    depth, d, nm = w_mod.shape
    tn = _col_tile(nm, max_tn=2 * TN)
    return pl.pallas_call(
        _mods_kernel,
        name="mods",
        out_shape=jax.ShapeDtypeStruct((depth, MOD_ROWS, nm), F32),
        grid=(depth, nm // tn),
        in_specs=[pl.BlockSpec((MOD_ROWS, d), lambda l, j: (0, 0)),
                  pl.BlockSpec((None, d, tn), lambda l, j: (l, 0, j)),
                  pl.BlockSpec((None, 1, tn), lambda l, j: (l, 0, j))],
        out_specs=pl.BlockSpec((None, MOD_ROWS, tn), lambda l, j: (l, 0, j)),
        compiler_params=_params("parallel", "parallel"),
    )(c_all, w_mod, b_mod.reshape(depth, 1, nm))


def _split_specs(n_lat_tiles, d):
    return [pl.BlockSpec((TM_ROW, d), lambda i: (jnp.minimum(i, n_lat_tiles - 1), 0)),
            pl.BlockSpec((TM_ROW, d), lambda i: (jnp.maximum(i - n_lat_tiles, 0), 0))]


def _modulate_kernel(xl_ref, xc_ref, mod_ref, h_ref, *, n_lat_tiles, i_shift, i_scale):
    x = jnp.where(pl.program_id(0) < n_lat_tiles, xl_ref[...], xc_ref[...])
    h = _ln(x) * (1.0 + mod_ref[i_scale:i_scale + 1, :]) + mod_ref[i_shift:i_shift + 1, :]
    h_ref[...] = h.astype(BF16)


def _modulate(x_lat, x_ctx, mods, i_shift, i_scale, seg):
    n_lat, d = x_lat.shape
    n = n_lat + x_ctx.shape[0]
    n_lat_tiles = n_lat // TM_ROW
    return pl.pallas_call(
        functools.partial(_modulate_kernel, n_lat_tiles=n_lat_tiles, i_shift=i_shift, i_scale=i_scale),
        name="modulate",
        out_shape=jax.ShapeDtypeStruct((n, d), BF16),
        grid=(n // TM_ROW,),
        in_specs=_split_specs(n_lat_tiles, d) + [pl.BlockSpec((None, N_MOD, d), lambda i: (seg(i), 0, 0))],
        out_specs=pl.BlockSpec((TM_ROW, d), lambda i: (i, 0)),
        compiler_params=_params("parallel"),
    )(x_lat, x_ctx, mods)


def _norm_kernel(*refs, n_lat_tiles, alpha, coef, i_gate, i_shift, i_scale):
    if n_lat_tiles is None:
        x_ref, y_ref, modg_ref, g_ref, b_ref, *rest = refs
        x = x_ref[...]
    else:
        xl_ref, xc_ref, y_ref, modg_ref, g_ref, b_ref, *rest = refs
        x = jnp.where(pl.program_id(0) < n_lat_tiles, xl_ref[...], xc_ref[...])
    t = alpha * x + (coef * modg_ref[i_gate:i_gate + 1, :]) * y_ref[...].astype(F32)
    xn = _ln(t) * g_ref[...] + b_ref[...]
    if i_shift is None:
        (xo_ref,) = rest
        xo_ref[...] = xn
    else:
        modn_ref, xo_ref, h_ref = rest
        xo_ref[...] = xn
        h = _ln(xn) * (1.0 + modn_ref[i_scale:i_scale + 1, :]) + modn_ref[i_shift:i_shift + 1, :]
        h_ref[...] = h.astype(BF16)


def _norm(x, y, mods_gate, i_gate, coef, g, b, alpha, seg, n_rows, mods_next=None, i_shift=None, i_scale=None,
          n_out=None):
    split = isinstance(x, tuple)
    d = y.shape[1]
    n_out = y.shape[0] if n_out is None else n_out
    row = pl.BlockSpec((TM_ROW, d), lambda i: (i, 0))
    vec = pl.BlockSpec((1, d), lambda i: (0, 0))
    mod = pl.BlockSpec((None, N_MOD, d), lambda i: (seg(i), 0, 0))
    n_lat_tiles = x[0].shape[0] // TM_ROW if split else None
    in_specs = (_split_specs(n_lat_tiles, d) if split else [row]) + [row, mod, vec, vec]
    args = (list(x) if split else [x]) + [y, mods_gate, g.reshape(1, d), b.reshape(1, d)]
    out_shape = [jax.ShapeDtypeStruct((n_out, d), F32)]
    out_specs = [row]
    if i_shift is not None:
        in_specs.append(mod)
        args.append(mods_next)
        out_shape.append(jax.ShapeDtypeStruct((n_out, d), BF16))
        out_specs.append(row)
    out = pl.pallas_call(
        functools.partial(_norm_kernel, n_lat_tiles=n_lat_tiles, alpha=alpha, coef=coef, i_gate=i_gate,
                          i_shift=i_shift, i_scale=i_scale),
        name="norm",
        out_shape=out_shape,
        grid=(n_rows // TM_ROW,),
        in_specs=in_specs,
        out_specs=out_specs,
        compiler_params=_params("parallel"),
    )(*args)
    return out if i_shift is not None else (out[0], None)


def _ffn_in_kernel(h_ref, wg_ref, wu_ref, o_ref, wgb_ref, wub_ref):
    _cast_weight_once(wg_ref, wgb_ref)
    _cast_weight_once(wu_ref, wub_ref)
    h = h_ref[...]
    g = jnp.dot(h, wgb_ref[...], preferred_element_type=F32)
    u = jnp.dot(h, wub_ref[...], preferred_element_type=F32)
    o_ref[...] = (g * _sigmoid(g) * u).astype(o_ref.dtype)


def _ffn_in(h, w, layer, n_rows):
    n, d = h.shape
    dff = w.shape[2] // 2
    tn = _col_tile(dff)
    return pl.pallas_call(
        _ffn_in_kernel,
        name="ffn_in",
        out_shape=jax.ShapeDtypeStruct((n, dff), BF16),
        grid=(dff // tn, n_rows // TM),
        in_specs=[pl.BlockSpec((TM, d), lambda j, i: (i, 0)),
                  pl.BlockSpec((None, d, tn), lambda j, i: (layer, 0, j)),
                  pl.BlockSpec((None, d, tn), lambda j, i: (layer, 0, j + dff // tn))],
        out_specs=pl.BlockSpec((TM, tn), lambda j, i: (i, j)),
        scratch_shapes=[pltpu.VMEM((d, tn), BF16), pltpu.VMEM((d, tn), BF16)],
        compiler_params=_params("arbitrary", "arbitrary"),
    )(h, w, w)


def _mm_kernel(a_ref, w_ref, o_ref, wb_ref):
    _cast_weight_once(w_ref, wb_ref)
    o_ref[...] = jnp.dot(a_ref[...], wb_ref[...], preferred_element_type=F32).astype(o_ref.dtype)


def _mm(a, w, layer, col_off, n_cols, out_dtype, n_rows, name):
    n, k = a.shape
    max_tn, buffers = _wide_tile(k, jnp.dtype(out_dtype).itemsize)
    tn = _col_tile(n_cols, col_off, max_tn)
    joff = col_off // tn
    return pl.pallas_call(
        _mm_kernel,
        name=name,
        out_shape=jax.ShapeDtypeStruct((n, n_cols), out_dtype),
        grid=(n_cols // tn, n_rows // TM),
        in_specs=[pl.BlockSpec((TM, k), lambda j, i: (i, 0)),
                  pl.BlockSpec((None, k, tn), lambda j, i: (layer, 0, j + joff),
                               pipeline_mode=pl.Buffered(buffers))],
        out_specs=pl.BlockSpec((TM, tn), lambda j, i: (i, j)),
        scratch_shapes=[pltpu.VMEM((k, tn), BF16)],
        compiler_params=_params("arbitrary", "arbitrary"),
    )(a, w)


def _mm_rope_kernel(a_ref, w_ref, cos_ref, sin_ref, o_ref, orot_ref, wb_ref):
    _cast_weight_once(w_ref, wb_ref)
    acc = jnp.dot(a_ref[...], wb_ref[...], preferred_element_type=F32)
    o_ref[...] = acc.astype(o_ref.dtype)
    tn = acc.shape[1]
    reps = tn // HEAD_DIM
    cos = jnp.concatenate([cos_ref[...]] * reps, axis=1)
    sin = jnp.concatenate([sin_ref[...]] * reps, axis=1)
    lane = lax.broadcasted_iota(jnp.int32, acc.shape, 1)
    quarter = HEAD_DIM // 4
    first = (lane % (2 * quarter)) < quarter
    partner = jnp.where(first, pltpu.roll(acc, tn - quarter, 1), pltpu.roll(acc, quarter, 1))
    orot_ref[...] = (acc * cos + partner * sin).astype(orot_ref.dtype)


def _mm_rope(a, w, layer, col_off, n_cols, cos, sin, tiles_per_batch, n_rows):
    n, k = a.shape
    max_tn, buffers = _wide_tile(k, 2 * jnp.dtype(BF16).itemsize)
    tn = _col_tile(n_cols, col_off, max_tn)
    joff = col_off // tn
    out = jax.ShapeDtypeStruct((n, n_cols), BF16)
    tab = pl.BlockSpec((TM, HEAD_DIM), lambda j, i: (i % tiles_per_batch, 0))
    blk = pl.BlockSpec((TM, tn), lambda j, i: (i, j))
    return pl.pallas_call(
        _mm_rope_kernel,
        name="proj_rope",
        out_shape=[out, out],
        grid=(n_cols // tn, n_rows // TM),
        in_specs=[pl.BlockSpec((TM, k), lambda j, i: (i, 0)),
                  pl.BlockSpec((None, k, tn), lambda j, i: (layer, 0, j + joff),
                               pipeline_mode=pl.Buffered(buffers)),
                  tab, tab],
        out_specs=[blk, blk],
        scratch_shapes=[pltpu.VMEM((k, tn), BF16)],
        compiler_params=_params("arbitrary", "arbitrary"),
    )(a, w, cos, sin)


def _merge_kernel(h_ref, oa_ref, ob_ref, oc_ref, wga_ref, wgb_ref, wgc_ref, wa_ref, wb_ref, wc_ref, o_ref,
                  *scratch):
    weights = (wga_ref, wgb_ref, wgc_ref, wa_ref, wb_ref, wc_ref)
    for w_ref, wb16_ref in zip(weights, scratch):
        _cast_weight_once(w_ref, wb16_ref)
    gate_w, proj_w = scratch[:N_BRANCH], scratch[N_BRANCH:]
    h = h_ref[...]
    t = None
    for branch_ref, wg, wp in zip((oa_ref, ob_ref, oc_ref), gate_w, proj_w):
        gate = _sigmoid(jnp.dot(h, wg[...], preferred_element_type=F32))
        term = gate * jnp.dot(branch_ref[...], wp[...], preferred_element_type=F32)
        t = term if t is None else t + term
    o_ref[...] = t.astype(o_ref.dtype)


def _merge(h, o_a, o_b, o_c, w_in, gate_off, w_pa, w_pb, w_pc, layer, n_rows):
    n, d = h.shape
    tn = _col_tile(d, gate_off, MERGE_TN)

    def act(o):
        return pl.BlockSpec((TM, o.shape[1]), lambda j, i: (i, 0))

    def proj(w):
        return pl.BlockSpec((None, w.shape[1], tn), lambda j, i: (layer, 0, j), pipeline_mode=pl.Buffered(1))

    def gate(branch):
        joff = (gate_off + branch * d) // tn
        return pl.BlockSpec((None, d, tn), lambda j, i: (layer, 0, j + joff))

    return pl.pallas_call(
        _merge_kernel,
        name="merge",
        out_shape=jax.ShapeDtypeStruct((n, d), BF16),
        grid=(d // tn, n_rows // TM),
        in_specs=[act(h), act(o_a), act(o_b), act(o_c), gate(0), gate(1), gate(2), proj(w_pa), proj(w_pb), proj(w_pc)],
        out_specs=pl.BlockSpec((TM, tn), lambda j, i: (i, j)),
        scratch_shapes=([pltpu.VMEM((d, tn), BF16)] * N_BRANCH
                        + [pltpu.VMEM((w.shape[1], tn), BF16) for w in (w_pa, w_pb, w_pc)]),
        compiler_params=_params("arbitrary", "arbitrary"),
    )(h, o_a, o_b, o_c, w_in, w_in, w_in, w_pa, w_pb, w_pc)


def _attn_block_geometry(kind):
    half = WIN_H // 2
    if kind == 0:
        return 0, [max(i - half, 0) for i in range(ATT_ROWS)]
    if kind == 1:
        return -half, list(range(ATT_ROWS))
    return -(ATT_KROWS - ATT_ROWS), [min(half + i, ATT_KROWS - WIN_H) for i in range(ATT_ROWS)]


def _attn_sub_key_offset(kind, sub):
    _, first_key_row = _attn_block_geometry(kind)
    return min(min(first_key_row[sub * ATT_SUB:(sub + 1) * ATT_SUB]), ATT_KROWS - ATT_SUB_KROWS)


def _attn_sub_pattern(kind, sub):
    diff, first_key_row = _attn_block_geometry(kind)
    koff = _attn_sub_key_offset(kind, sub)
    rows = range(sub * ATT_SUB, (sub + 1) * ATT_SUB)
    return tuple((first_key_row[i] - koff, koff - i + diff + WIN_H - 1) for i in rows)


def _attn_patterns():
    patterns, index = [], []
    for kind in range(3):
        index.append([])
        for sub in range(ATT_ROWS // ATT_SUB):
            p = _attn_sub_pattern(kind, sub)
            if p not in patterns:
                patterns.append(p)
            index[kind].append(patterns.index(p))
    return patterns, index


def _rpb_table_kernel(rpb_ref, o_ref):
    shape = (GRID_W, 2 * GRID_W)
    cq = lax.broadcasted_iota(jnp.int32, shape, 0)
    lane = lax.broadcasted_iota(jnp.int32, shape, 1)
    ck = lane % GRID_W
    second = lane >= GRID_W
    start = jnp.clip(cq - WIN_W // 2, 0, GRID_W - WIN_W)
    in_win = (ck >= start) & (ck < start + WIN_W)
    neg = jnp.full(shape, NEG_INF, F32)
    n_bias_rows = 2 * WIN_H - 1

    def half_tile(a, lane_off):
        row = jnp.broadcast_to(rpb_ref[a:a + 1, :], shape)
        return pltpu.roll(row, (lane_off - (WIN_W - 1)) % LANES, 1, stride=1, stride_axis=0)

    lo_half = [half_tile(a, 0) for a in range(n_bias_rows)]
    hi_half = [half_tile(a, GRID_W) for a in range(n_bias_rows)]
    for p, pattern in enumerate(_attn_patterns()[0]):
        for isub, (lo, a0) in enumerate(pattern):
            for pair in range(ATT_SUB_KROWS // 2):
                j0, j1 = 2 * pair, 2 * pair + 1
                t0 = lo_half[j0 + a0] if lo <= j0 < lo + WIN_H else neg
                t1 = hi_half[j1 + a0] if lo <= j1 < lo + WIN_H else neg
                tile = jnp.where(in_win, jnp.where(second, t1, t0), neg)
                o_ref[p, isub * GRID_W:(isub + 1) * GRID_W, pair * 2 * GRID_W:(pair + 1) * 2 * GRID_W] = tile


def _rpb_table(rpb):
    heads, nr, nc = rpb.shape
    rpb_pad = jnp.zeros((heads, 2 * WIN_H, LANES), F32).at[:, :nr, :nc].set(rpb)
    shape = (len(_attn_patterns()[0]), ATT_SUB * GRID_W, ATT_SUB_KROWS * GRID_W)
    return pl.pallas_call(
        _rpb_table_kernel,
        name="rpb_table",
        out_shape=jax.ShapeDtypeStruct((heads,) + shape, F32),
        grid=(heads,),
        in_specs=[pl.BlockSpec((None, 2 * WIN_H, LANES), lambda h: (h, 0, 0))],
        out_specs=pl.BlockSpec((None,) + shape, lambda h: (h, 0, 0, 0)),
        compiler_params=_params("parallel"),
    )(rpb_pad)


def _attn_kernel(q_ref, qr_ref, kr_ref, v_ref, kc_ref, vc_ref, bias_ref, o_ref, *, n_grid_rows, n_blocks, scale):
    rb = pl.program_id(2)
    kr0 = jnp.clip(rb * ATT_ROWS - WIN_H // 2, 0, n_grid_rows - ATT_KROWS)
    kc = kc_ref[...]
    vc = vc_ref[...]
    table_index = _attn_patterns()[1]

    def by_kind(values):
        return jnp.where(rb == 0, values[0], jnp.where(rb == n_blocks - 1, values[2], values[1]))

    for sub in range(ATT_ROWS // ATT_SUB):
        koff = by_kind([_attn_sub_key_offset(kind, sub) for kind in range(3)])
        keys = pl.ds(pl.multiple_of((kr0 + koff) * GRID_W, GRID_W), ATT_SUB_KROWS * GRID_W)
        rows = pl.ds(sub * ATT_SUB * GRID_W, ATT_SUB * GRID_W)
        s = lax.dot_general(qr_ref[rows, :], kr_ref[keys, :], NT_DIMS, preferred_element_type=F32) * scale
        s += bias_ref[by_kind([table_index[kind][sub] for kind in range(3)])]
        sc = lax.dot_general(q_ref[rows, :], kc, NT_DIMS, preferred_element_type=F32) * scale
        m = jnp.maximum(jnp.max(s, axis=-1, keepdims=True), jnp.max(sc, axis=-1, keepdims=True))
        p = jnp.exp(s - m)
        pc = jnp.exp(sc - m)
        denom = jnp.sum(p, axis=-1, keepdims=True) + jnp.sum(pc, axis=-1, keepdims=True)
        o = jnp.dot(p.astype(BF16), v_ref[keys, :], preferred_element_type=F32)
        o += jnp.dot(pc.astype(BF16), vc, preferred_element_type=F32)
        o_ref[rows, :] = (o / denom).astype(o_ref.dtype)


def _attention(q, q_rot, k, k_rot, v, bias, n_batch, seq, ctx_len):
    n, width = q.shape
    heads = width // HEAD_DIM
    n_grid_rows = seq // GRID_W
    tq = ATT_ROWS * GRID_W
    blocks = seq // tq
    ctx0 = n_batch * seq // ctx_len
    qspec = pl.BlockSpec((tq, HEAD_DIM), lambda b, h, r: (b * blocks + r, h))
    kspec = pl.BlockSpec((seq, HEAD_DIM), lambda b, h, r: (b, h))
    cspec = pl.BlockSpec((ctx_len, HEAD_DIM), lambda b, h, r: (ctx0 + b, h))

    return pl.pallas_call(
        functools.partial(_attn_kernel, n_grid_rows=n_grid_rows, n_blocks=blocks, scale=HEAD_DIM ** -0.5),
        name="attn",
        out_shape=jax.ShapeDtypeStruct((n, width), BF16),
        grid=(n_batch, heads, blocks),
        in_specs=[qspec, qspec, kspec, kspec, cspec, cspec,
                  pl.BlockSpec((None,) + bias.shape[1:], lambda b, h, r: (h, 0, 0, 0))],
        out_specs=qspec,
        compiler_params=_params("parallel", "parallel", "arbitrary"),
    )(q, q_rot, k_rot, v, k, v, bias)


def _ctx_attn_kernel(q_ref, k_ref, v_ref, prev_ref, o_ref, *, scale):
    del prev_ref
    s = lax.dot_general(q_ref[...], k_ref[...], NT_DIMS, preferred_element_type=F32) * scale
    m = jnp.max(s, axis=-1, keepdims=True)
    p = jnp.exp(s - m)
    denom = jnp.sum(p, axis=-1, keepdims=True)
    o = jnp.dot(p.astype(BF16), v_ref[...], preferred_element_type=F32)
    o_ref[...] = (o / denom).astype(o_ref.dtype)


def _ctx_attention(q, k, v, o_att, n_batch, seq, ctx_len):
    n, width = q.shape
    heads = width // HEAD_DIM
    ctx0 = n_batch * seq // ctx_len
    spec = pl.BlockSpec((ctx_len, HEAD_DIM), lambda b, h: (ctx0 + b, h))
    return pl.pallas_call(
        functools.partial(_ctx_attn_kernel, scale=HEAD_DIM ** -0.5),
        name="ctx_attn",
        out_shape=jax.ShapeDtypeStruct((n, width), BF16),
        grid=(n_batch, heads),
        in_specs=[spec, spec, spec, pl.BlockSpec(memory_space=pl.ANY)],
        out_specs=spec,
        input_output_aliases={3: 0},
        compiler_params=_params("parallel", "parallel"),
    )(q, k, v, o_att)


def _gmlp_kernel(z_ref, g_ref, b_ref, ws_ref, bs_ref, o_ref, *, width, chunks):
    z = jax.nn.gelu(z_ref[...])
    u = z[:, :width]
    v = (_ln(z[:, width:]) * g_ref[...] + b_ref[...]).astype(BF16)
    for c in range(chunks):
        rows = slice(c * CHUNK, (c + 1) * CHUNK)
        for g in range(width // GMLP_GROUP_DIM):
            cols = slice(g * GMLP_GROUP_DIM, (g + 1) * GMLP_GROUP_DIM)
            mixed = jnp.dot(ws_ref[g], v[rows, cols], preferred_element_type=F32) + bs_ref[:, cols]
            o_ref[rows, cols] = (u[rows, cols] * mixed).astype(o_ref.dtype)


def _gmlp(z, ln_g, ln_b, w_s, b_s, n_rows):
    n, two_w = z.shape
    width = two_w // 2
    groups = width // GMLP_GROUP_DIM
    chunks = TM // CHUNK
    bs = jnp.repeat(b_s.T, GMLP_GROUP_DIM, axis=1)
    vec = pl.BlockSpec((1, width), lambda i: (0, 0))
    return pl.pallas_call(
        functools.partial(_gmlp_kernel, width=width, chunks=chunks),
        name="gmlp",
        out_shape=jax.ShapeDtypeStruct((n, width), BF16),
        grid=(n_rows // TM,),
        in_specs=[pl.BlockSpec((TM, two_w), lambda i: (i, 0)), vec, vec,
                  pl.BlockSpec((groups, CHUNK, CHUNK), lambda i: (0, 0, 0)),
                  pl.BlockSpec((CHUNK, width), lambda i: (0, 0))],
        out_specs=pl.BlockSpec((TM, width), lambda i: (i, 0)),
        compiler_params=_params("parallel"),
    )(z, ln_g.reshape(1, width), ln_b.reshape(1, width), w_s.astype(BF16), bs)


def _s5_gen_kernel(lr_ref, li_ref, ldt_ref, bre_ref, bim_ref, cre_ref, cim_ref, m_ref, wst_ref, et_ref, a_ref):
    hi = lax.Precision.HIGHEST
    nb = SSM_BLOCK
    lane = lax.broadcasted_iota(jnp.int32, (SSM_GROUP, SSM_ROW), 1)
    state_lane = lax.broadcasted_iota(jnp.int32, (SSM_GROUP, 2 * SSM_STATE), 1)
    in_group = [state_lane < SSM_STATE, state_lane >= SSM_STATE]
    m_rows = [[None] * nb for _ in range(2)]
    for d in range(2):
        lr = lr_ref[d:d + 1, :]
        li = li_ref[d:d + 1, :]
        dt = jnp.exp(ldt_ref[d:d + 1, :])
        mag = jnp.exp(lr * dt)
        ar = mag * jnp.cos(li * dt)
        ai = mag * jnp.sin(li * dt)
        den = lr * lr + li * li
        fr = ((ar - 1.0) * lr + ai * li) / den
        fi = (ai * lr - (ar - 1.0) * li) / den
        br = bre_ref[d]
        bi = bim_ref[d]
        bbr = fr * br - fi * bi
        bbi = fr * bi + fi * br
        cr = cre_ref[d]
        ci = cim_ref[d]
        pr = [jnp.ones_like(ar)]
        pi = [jnp.zeros_like(ai)]
        for _ in range(nb):
            pr.append(pr[-1] * ar - pi[-1] * ai)
            pi.append(pr[-2] * ai + pi[-1] * ar)
        a_ref[2 * d] = pr[nb]
        a_ref[2 * d + 1] = pi[nb]
        e_state = [(t + 1) if d == 0 else (nb - t) for t in range(nb)]
        e_in = [(nb - 1 - s) if d == 0 else s for s in range(nb)]
        e_lag = [e if d == 0 else (nb - 1 - e) for e in range(nb)]
        for s in range(nb):
            wr, wi = pr[e_in[s]], pi[e_in[s]]
            er, ei = pr[e_state[s]], pi[e_state[s]]
            parts = (wr * bbr - wi * bbi, wr * bbi + wi * bbr)
            outs = (cr * er - ci * ei, -(cr * ei + ci * er))
            for g in range(2):
                rows = slice(g * SSM_ROW + s * SSM_GROUP, g * SSM_ROW + (s + 1) * SSM_GROUP)
                for part in range(2):
                    cols = slice((2 * d + part) * 2 * SSM_STATE, (2 * d + part + 1) * 2 * SSM_STATE)
                    wst_ref[rows, cols] = jnp.where(in_group[g], parts[part], 0.0).astype(wst_ref.dtype)
                    et_ref[rows, cols] = jnp.where(in_group[g], outs[part], 0.0).astype(et_ref.dtype)
        lag_r = jnp.concatenate([cr * pr[e] - ci * pi[e] for e in e_lag], axis=0)
        lag_i = jnp.concatenate([-(cr * pi[e] + ci * pr[e]) for e in e_lag], axis=0)
        for g in range(2):
            gbr = jnp.where(in_group[g], bbr, 0.0)
            gbi = jnp.where(in_group[g], bbi, 0.0)
            kbase = (lax.dot_general(gbr, lag_r, NT_DIMS, precision=hi, preferred_element_type=F32)
                     + lax.dot_general(gbi, lag_i, NT_DIMS, precision=hi, preferred_element_type=F32))
            for s in range(nb):
                if d == 0:
                    sh = s * SSM_GROUP
                    m_rows[g][s] = kbase if sh == 0 else jnp.where(lane >= sh, pltpu.roll(kbase, sh, 1), 0.0)
                else:
                    sh = (nb - 1 - s) * SSM_GROUP
                    row = kbase if sh == 0 else jnp.where(lane < SSM_ROW - sh, pltpu.roll(kbase, SSM_ROW - sh, 1), 0.0)
                    m_rows[g][s] = m_rows[g][s] + row
    zeros = jnp.zeros((SSM_GROUP, SSM_ROW), m_ref.dtype)
    for g in range(2):
        for s in range(nb):
            rows = slice(g * SSM_ROW + s * SSM_GROUP, g * SSM_ROW + (s + 1) * SSM_GROUP)
            m_ref[rows, g * SSM_ROW:(g + 1) * SSM_ROW] = m_rows[g][s].astype(m_ref.dtype)
            m_ref[rows, (1 - g) * SSM_ROW:(2 - g) * SSM_ROW] = zeros


def _s5_gen(lam_re, lam_im, log_dt, b_re, b_im, c_re, c_im):
    _, groups, p = lam_re.shape
    pairs = groups // 2

    def vec_pairs(x):
        return jnp.swapaxes(x.reshape(2, pairs, 2 * p), 0, 1)

    def mat_pairs(x):
        x = x.reshape(2, pairs, 2, SSM_GROUP, p)
        return jnp.transpose(x, (1, 0, 3, 2, 4)).reshape(pairs, 2, SSM_GROUP, 2 * p)

    ldt = jnp.broadcast_to(log_dt[:, :, None], lam_re.shape)
    vec = pl.BlockSpec((None, 2, 2 * p), lambda g: (g, 0, 0))
    mat = pl.BlockSpec((None, 2, SSM_GROUP, 2 * p), lambda g: (g, 0, 0, 0))
    return pl.pallas_call(
        _s5_gen_kernel,
        name="s5_gen",
        out_shape=[jax.ShapeDtypeStruct((pairs, 2 * SSM_ROW, 2 * SSM_ROW), BF16),
                   jax.ShapeDtypeStruct((pairs, 2 * SSM_ROW, 4 * 2 * p), BF16),
                   jax.ShapeDtypeStruct((pairs, 2 * SSM_ROW, 4 * 2 * p), BF16),
                   jax.ShapeDtypeStruct((pairs, 4, 1, 2 * p), F32)],
        grid=(pairs,),
        in_specs=[vec, vec, vec, mat, mat, mat, mat],
        out_specs=[pl.BlockSpec((None, 2 * SSM_ROW, 2 * SSM_ROW), lambda g: (g, 0, 0)),
                   pl.BlockSpec((None, 2 * SSM_ROW, 4 * 2 * p), lambda g: (g, 0, 0)),
                   pl.BlockSpec((None, 2 * SSM_ROW, 4 * 2 * p), lambda g: (g, 0, 0)),
                   pl.BlockSpec((None, 4, 1, 2 * p), lambda g: (g, 0, 0, 0))],
        compiler_params=_params("parallel"),
    )(vec_pairs(lam_re), vec_pairs(lam_im), vec_pairs(ldt),
      mat_pairs(jnp.swapaxes(b_re, 2, 3)), mat_pairs(jnp.swapaxes(b_im, 2, 3)), mat_pairs(c_re), mat_pairs(c_im))


def _slot_transpose(xs):
    n = SLOTS
    slot = lax.broadcasted_iota(jnp.int32, xs[0].shape, 1) // SSM_GROUP
    diag = []
    for r in range(n):
        w = xs[r % n]
        for a in range(1, n):
            w = jnp.where(slot == a, xs[(a + r) % n], w)
        diag.append(w if r == 0 else pltpu.roll(w, r * SSM_GROUP, 1))
    ys = []
    for a in range(n):
        y = diag[0]
        for r in range(1, n):
            y = jnp.where(slot == (a + r) % n, diag[r], y)
        ys.append(y)
    return ys


def _s5_state_kernel(x_ref, wst_ref, u_ref, *state_refs):
    n_blk = u_ref.shape[0]
    xs = [x_ref[pl.ds(s, n_blk, stride=SSM_BLOCK), :] for s in range(SSM_BLOCK)]
    by_col = [_slot_transpose(xs[col * SLOTS:(col + 1) * SLOTS]) for col in range(SSM_BLOCK // SLOTS)]
    pieces = [by_col[col][gl] for gl in range(SSM_CHUNK_GROUPS) for col in range(SSM_BLOCK // SLOTS)]
    cols_per_pair = 2 * SSM_ROW // LANES
    for pair in range(SSM_CHUNK_GROUPS // 2):
        u = jnp.concatenate(pieces[pair * cols_per_pair:(pair + 1) * cols_per_pair], axis=1).astype(BF16)
        u_ref[:, pair * 2 * SSM_ROW:(pair + 1) * 2 * SSM_ROW] = u
        states = jnp.dot(u, wst_ref[pair], preferred_element_type=F32)
        for k, o_ref in enumerate(state_refs):
            o_ref[:, pair * 2 * SSM_STATE:(pair + 1) * 2 * SSM_STATE] = states[:, k * 2 * SSM_STATE:
                                                                                (k + 1) * 2 * SSM_STATE]


def _s5_row_tile(n_rows):
    return n_rows // 2 if n_rows % 32 == 0 else n_rows


def _s5_states(us, wst_pair, groups):
    n_rows = us.shape[0] // SSM_BLOCK
    chunks = groups // SSM_CHUNK_GROUPS
    pairs = SSM_CHUNK_GROUPS // 2
    rt = _s5_row_tile(n_rows)
    state = jax.ShapeDtypeStruct((n_rows, groups * SSM_STATE), F32)
    sspec = pl.BlockSpec((rt, SSM_CHUNK_GROUPS * SSM_STATE), lambda ch, i: (i, ch))
    uspec = pl.BlockSpec((rt, SSM_CHUNK_GROUPS * SSM_ROW), lambda ch, i: (i, ch))
    return pl.pallas_call(
        _s5_state_kernel,
        name="s5_states",
        out_shape=[jax.ShapeDtypeStruct((n_rows, groups * SSM_ROW), BF16)] + [state] * 4,
        grid=(chunks, n_rows // rt),
        in_specs=[pl.BlockSpec((rt * SSM_BLOCK, LANES), lambda ch, i: (i, ch)),
                  pl.BlockSpec((pairs,) + wst_pair.shape[1:], lambda ch, i: (ch, 0, 0))],
        out_specs=[uspec] + [sspec] * 4,
        compiler_params=_params("parallel", "parallel"),
    )(us, wst_pair)


def _s5_scan_kernel(sfr_ref, sfi_ref, sbr_ref, sbi_ref, a_ref, hfr_ref, hfi_ref, hbr_ref, hbi_ref,
                    *, n_batch, n_lat_blocks, n_ctx_blocks):
    afr, afi, abr, abi = (a_ref[k:k + 1, :] for k in range(4))
    zero = jnp.zeros_like(afr)
    nl, nc = n_lat_blocks, n_ctx_blocks

    def step(t, carry):
        new = []
        in_ctx = t < nc
        for b in range(n_batch):
            lat0, ctx0 = b * nl, n_batch * nl + b * nc
            fr, fi, br, bi = carry[4 * b:4 * b + 4]
            rf = pl.ds(jnp.where(in_ctx, ctx0 + t, lat0 + t - nc), 1)
            rb = pl.ds(jnp.where(in_ctx, ctx0 + nc - 1 - t, lat0 + nl - 1 - (t - nc)), 1)
            hfr_ref[rf, :] = fr
            hfi_ref[rf, :] = fi
            hbr_ref[rb, :] = br
            hbi_ref[rb, :] = bi
            new += [afr * fr - afi * fi + sfr_ref[rf, :], afr * fi + afi * fr + sfi_ref[rf, :],
                    abr * br - abi * bi + sbr_ref[rb, :], abr * bi + abi * br + sbi_ref[rb, :]]
        return tuple(new)

    lax.fori_loop(0, nl + nc, step, (zero,) * (4 * n_batch))


def _s5_scan(s_parts, a_rows, n_batch, n_lat_blocks, n_ctx_blocks):
    rows, lanes = s_parts[0].shape
    tl = min(lanes, 4 * LANES)
    spec = pl.BlockSpec((rows, tl), lambda j: (0, j))
    return pl.pallas_call(
        functools.partial(_s5_scan_kernel, n_batch=n_batch, n_lat_blocks=n_lat_blocks, n_ctx_blocks=n_ctx_blocks),
        name="s5_scan",
        out_shape=[jax.ShapeDtypeStruct((rows, lanes), F32)] * 4,
        grid=(lanes // tl,),
        in_specs=[spec] * 4 + [pl.BlockSpec((4, tl), lambda j: (0, j))],
        out_specs=[spec] * 4,
        compiler_params=_params("parallel"),
    )(*s_parts, a_rows)


def _s5_out_kernel(u_ref, m_ref, et_ref, hfr_ref, hfi_ref, hbr_ref, hbi_ref, y_ref):
    ys = []
    for pair in range(SSM_CHUNK_GROUPS // 2):
        y = jnp.dot(u_ref[:, pair * 2 * SSM_ROW:(pair + 1) * 2 * SSM_ROW], m_ref[pair], preferred_element_type=F32)
        h = jnp.concatenate([h_ref[:, pair * 2 * SSM_STATE:(pair + 1) * 2 * SSM_STATE]
                             for h_ref in (hfr_ref, hfi_ref, hbr_ref, hbi_ref)], axis=1).astype(BF16)
        ys.append(y + lax.dot_general(h, et_ref[pair], NT_DIMS, preferred_element_type=F32))
    n_blk = ys[0].shape[0]
    for col in range(SSM_BLOCK // SLOTS):
        by_group = [ys[gl // 2][:, (gl % 2) * SSM_ROW + col * LANES:(gl % 2) * SSM_ROW + (col + 1) * LANES]
                    for gl in range(SSM_CHUNK_GROUPS)]
        for k, y in enumerate(_slot_transpose(by_group)):
            y_ref[pl.ds(col * SLOTS + k, n_blk, stride=SSM_BLOCK), :] = y


def _s5_outputs(u, m_pair, et_pair, h_parts, groups):
    n_rows = u.shape[0]
    chunks = groups // SSM_CHUNK_GROUPS
    pairs = SSM_CHUNK_GROUPS // 2
    rt = _s5_row_tile(n_rows)
    hspec = pl.BlockSpec((rt, SSM_CHUNK_GROUPS * SSM_STATE), lambda ch, i: (i, ch))
    return pl.pallas_call(
        _s5_out_kernel,
        name="s5_out",
        out_shape=jax.ShapeDtypeStruct((n_rows * SSM_BLOCK, groups * SSM_GROUP), F32),
        grid=(chunks, n_rows // rt),
        in_specs=[pl.BlockSpec((rt, SSM_CHUNK_GROUPS * SSM_ROW), lambda ch, i: (i, ch)),
                  pl.BlockSpec((pairs,) + m_pair.shape[1:], lambda ch, i: (ch, 0, 0)),
                  pl.BlockSpec((pairs,) + et_pair.shape[1:], lambda ch, i: (ch, 0, 0)),
                  hspec, hspec, hspec, hspec],
        out_specs=pl.BlockSpec((rt * SSM_BLOCK, LANES), lambda ch, i: (i, ch)),
        compiler_params=_params("parallel", "parallel"),
    )(u, m_pair, et_pair, *h_parts)


def _s5_glu_kernel(ys_ref, us_ref, d_ref, w_ref, b_ref, o_ref, wb_ref):
    @pl.when(pl.program_id(0) == 0)
    def _():
        wb_ref[...] = w_ref[...].astype(BF16)

    y = jax.nn.gelu(ys_ref[...] + d_ref[...] * us_ref[...])
    gate = _sigmoid(jnp.dot(y.astype(BF16), wb_ref[...], preferred_element_type=F32) + b_ref[...])
    o_ref[...] = (y * gate).astype(o_ref.dtype)


def _s5_glu(y_s, us, d_skip, w_glu, layer, b_glu, n_rows):
    n, w = us.shape
    row = pl.BlockSpec((TM, w), lambda i: (i, 0))
    vec = pl.BlockSpec((1, w), lambda i: (0, 0))
    return pl.pallas_call(
        _s5_glu_kernel,
        name="s5_glu",
        out_shape=jax.ShapeDtypeStruct((n, w), BF16),
        grid=(n_rows // TM,),
        in_specs=[row, row, vec, pl.BlockSpec((None, w, w), lambda i: (layer, 0, 0)), vec],
        out_specs=row,
        scratch_shapes=[pltpu.VMEM((w, w), BF16)],
        compiler_params=_params("arbitrary"),
    )(y_s, us, d_skip.reshape(1, w), w_glu, b_glu.reshape(1, w))


def _s5(us, n_batch, seq, ctx_len, lam_re, lam_im, log_dt, b_re, b_im, c_re, c_im):
    groups = us.shape[1] // SSM_GROUP
    m_pair, wst_pair, et_pair, a_pair = _s5_gen(lam_re, lam_im, log_dt, b_re, b_im, c_re, c_im)
    a_rows = jnp.transpose(a_pair[:, :, 0, :], (1, 0, 2)).reshape(4, groups * SSM_STATE)
    u, *s_parts = _s5_states(us, wst_pair, groups)
    h_parts = _s5_scan(s_parts, a_rows, n_batch, seq // SSM_BLOCK, ctx_len // SSM_BLOCK)
    return _s5_outputs(u, m_pair, et_pair, h_parts, groups)


def _rope_tables(seq):
    quarter = HEAD_DIM // 4
    n_grid_rows = seq // GRID_W
    inv = ROPE_BASE ** (-jnp.arange(quarter, dtype=F32) / quarter)
    shape = (n_grid_rows, GRID_W, quarter)
    ang_r = jnp.arange(n_grid_rows, dtype=F32)[:, None] * inv[None, :]
    ang_c = jnp.arange(GRID_W, dtype=F32)[:, None] * inv[None, :]
    cos_r, sin_r = (jnp.broadcast_to(f(ang_r)[:, None, :], shape) for f in (jnp.cos, jnp.sin))
    cos_c, sin_c = (jnp.broadcast_to(f(ang_c)[None, :, :], shape) for f in (jnp.cos, jnp.sin))
    cos = jnp.concatenate([cos_r, cos_r, cos_c, cos_c], axis=-1).reshape(seq, HEAD_DIM)
    sin = jnp.concatenate([-sin_r, sin_r, -sin_c, sin_c], axis=-1).reshape(seq, HEAD_DIM)
    return cos, sin


def kernel(x, c, ctx, c_ctx, w_mod, b_mod, ln_g, ln_b, ffn1_w_in, ffn1_w_out, ffn2_w_in, ffn2_w_out, w_in, rpb, gm_ln_g, gm_ln_b, gm_ws, gm_bs, ssm_lam_re, ssm_lam_im, ssm_log_dt, ssm_b_re, ssm_b_im, ssm_c_re, ssm_c_im, ssm_d, ssm_w_glu, ssm_b_glu, w_pa, w_pb, w_pc, w_o):
    n_batch, seq, d = x.shape
    ctx_len = ctx.shape[1]
    depth = w_mod.shape[0]
    att_w = w_pb.shape[1]
    gm_w = w_pa.shape[1]
    ssm_w = w_pc.shape[1]
    assert ssm_w % LANES == 0 and seq % SSM_BLOCK == 0 and ctx_len % SSM_BLOCK == 0
    q_off, k_off, v_off = 0, att_w, 2 * att_w
    gm_off = 3 * att_w
    ssm_off = gm_off + 2 * gm_w
    gate_off = ssm_off + ssm_w
    alpha = (2.0 * depth) ** 0.25
    n_lat = n_batch * seq
    n_all = n_lat + n_batch * ctx_len
    assert seq % TM == 0 and (n_batch * ctx_len) % TM == 0 and seq % (ATT_ROWS * GRID_W) == 0
    assert seq // GRID_W >= ATT_KROWS and ctx_len % CHUNK == 0 and n_lat % ctx_len == 0
    assert n_batch + 1 <= MOD_ROWS and rpb.shape[2:] == (2 * WIN_H - 1, 2 * WIN_W - 1)

    seg = _seg_map(seq // TM_ROW, n_batch)
    tiles_per_batch = seq // TM

    x_lat, x_ctx = x.reshape(n_lat, d), ctx.reshape(n_batch * ctx_len, d)
    c_all = jnp.zeros((MOD_ROWS, d), F32).at[:n_batch].set(c).at[n_batch].set(c_ctx)
    mods = _mods(c_all, w_mod, b_mod)[:, :n_batch + 1].reshape(depth, n_batch + 1, N_MOD, d)
    cos, sin = _rope_tables(seq)

    xs = (x_lat, x_ctx)
    h = _modulate(x_lat, x_ctx, mods[0], 0, 1, seg)
    for l in range(depth):
        last = l == depth - 1
        n_mix = n_lat if last else n_all

        y = _mm(_ffn_in(h, ffn1_w_in, l, n_all), ffn1_w_out, l, 0, d, BF16, n_all, "ffn_out")
        xs, h = _norm(xs, y, mods[l], 2, 0.5, ln_g[l, 0], ln_b[l, 0], alpha, seg, n_all,
                      mods_next=mods[l], i_shift=3, i_scale=4)

        q, q_rot = _mm_rope(h, w_in, l, q_off, att_w, cos, sin, tiles_per_batch, n_mix)
        k, k_rot = _mm_rope(h, w_in, l, k_off, att_w, cos, sin, tiles_per_batch, n_all)
        v = _mm(h, w_in, l, v_off, att_w, BF16, n_all, "proj_v")
        z = _mm(h, w_in, l, gm_off, 2 * gm_w, F32, n_mix, "proj_z")
        us = _mm(h, w_in, l, ssm_off, ssm_w, F32, n_all, "proj_us")

        o_att = _attention(q, q_rot, k, k_rot, v, _rpb_table(rpb[l]), n_batch, seq, ctx_len)
        if not last:
            o_att = _ctx_attention(q, k, v, o_att, n_batch, seq, ctx_len)
        o_gm = _gmlp(z, gm_ln_g[l], gm_ln_b[l], gm_ws[l], gm_bs[l], n_mix)
        y_s = _s5(us, n_batch, seq, ctx_len, ssm_lam_re[l], ssm_lam_im[l], ssm_log_dt[l],
                  ssm_b_re[l], ssm_b_im[l], ssm_c_re[l], ssm_c_im[l])
        o_ssm = _s5_glu(y_s, us, ssm_d[l], ssm_w_glu, l, ssm_b_glu[l], n_mix)
        merged = _merge(h, o_gm, o_att, o_ssm, w_in, gate_off, w_pa, w_pb, w_pc, l, n_mix)
        y = _mm(merged, w_o, l, 0, d, BF16, n_mix, "out_proj")
        xs, h = _norm(xs, y, mods[l], 5, 1.0, ln_g[l, 1], ln_b[l, 1], alpha, seg, n_mix,
                      mods_next=mods[l], i_shift=6, i_scale=7)

        y = _mm(_ffn_in(h, ffn2_w_in, l, n_mix), ffn2_w_out, l, 0, d, BF16, n_mix, "ffn_out")
        if last:
            xs, _ = _norm(xs, y, mods[l], 8, 0.5, ln_g[l, 2], ln_b[l, 2], alpha, seg, n_mix, n_out=n_mix)
        else:
            xs, h = _norm(xs, y, mods[l], 8, 0.5, ln_g[l, 2], ln_b[l, 2], alpha, seg, n_mix,
                          mods_next=mods[l + 1], i_shift=0, i_scale=1)
    return xs.reshape(n_batch, seq, d)
```

```python
import functools

import jax
import jax.numpy as jnp
from jax import lax
from jax.experimental import pallas as pl
from jax.experimental.pallas import tpu as pltpu

F32 = jnp.float32
BF16 = jnp.bfloat16

GRID_W = 64
N_SUB = 3
N_MOD = 3 * N_SUB
HEAD_DIM = 128
WIN_H = 8
WIN_W = 16
ROPE_BASE = 10000.0
CHUNK = 128
GMLP_GROUP_DIM = 128
SSM_GROUP = 16
SSM_STATE = 64
N_BRANCH = 3
LN_EPS = 1e-6
NEG_INF = -1e30

LANES = 128
SUBLANES = 8
MIB = 1024 * 1024
V7X_VMEM_BYTES = 64 * MIB
VMEM_LIMIT_BYTES = V7X_VMEM_BYTES - 4 * MIB
VMEM_MARGIN_BYTES = 2 * MIB
SSM_BLOCK = 16
SSM_ROW = SSM_BLOCK * SSM_GROUP
SLOTS = LANES // SSM_GROUP
SSM_CHUNK_GROUPS = SLOTS
MOD_ROWS = SUBLANES

TM = 512
TM_ROW = 256
TN = 512
MERGE_TN = 256
ATT_ROWS = 64
ATT_KROWS = ATT_ROWS + WIN_H
ATT_SUB = 4
ATT_SUB_KROWS = ATT_SUB + WIN_H
NT_DIMS = (((1,), (1,)), ((), ()))


def _params(*sem):
    return pltpu.CompilerParams(dimension_semantics=sem, vmem_limit_bytes=VMEM_LIMIT_BYTES)


def _ln(x):
    mu = jnp.mean(x, axis=-1, keepdims=True)
    xc = x - mu
    var = jnp.mean(xc * xc, axis=-1, keepdims=True)
    return xc * lax.rsqrt(var + LN_EPS)


def _sigmoid(x):
    return 0.5 * jnp.tanh(0.5 * x) + 0.5


def _col_tile(n_cols, col_off=0, max_tn=TN):
    for tn in range(max_tn, 0, -LANES):
        if n_cols % tn == 0 and col_off % tn == 0:
            return tn
    raise ValueError(f"column group ({col_off}, {n_cols}) is not lane aligned")


def _wide_tile(k, out_bytes_per_col):
    tn = 2 * TN
    f32, bf16 = jnp.dtype(F32).itemsize, jnp.dtype(BF16).itemsize
    for buffers in (2, 1):
        need = k * tn * (buffers * f32 + bf16) + 2 * TM * k * bf16 + 2 * TM * tn * out_bytes_per_col
        if need <= VMEM_LIMIT_BYTES - VMEM_MARGIN_BYTES:
            return tn, buffers
    return TN, 2


def _seg_map(tiles_per_batch, n_batch):
    return lambda i: jnp.minimum(i // tiles_per_batch, n_batch)


def _cast_weight_once(w_ref, wb_ref):
    @pl.when(pl.program_id(1) == 0)
    def _():
        wb_ref[...] = w_ref[...].astype(BF16)


def _mods_kernel(c_ref, w_ref, b_ref, o_ref):
    s = jax.nn.silu(c_ref[...]).astype(BF16)
    o_ref[...] = jnp.dot(s, w_ref[...].astype(BF16), preferred_element_type=F32) + b_ref[...]


def _mods(c_all, w_mod, b_mod):
    depth, d, nm = w_mod.shape
    tn = _col_tile(nm, max_tn=2 * TN)
    return pl.pallas_call(
        _mods_kernel,
        name="mods",
        out_shape=jax.ShapeDtypeStruct((depth, MOD_ROWS, nm), F32),
        grid=(depth, nm // tn),
        in_specs=[pl.BlockSpec((MOD_ROWS, d), lambda l, j: (0, 0)),
                  pl.BlockSpec((None, d, tn), lambda l, j: (l, 0, j)),
                  pl.BlockSpec((None, 1, tn), lambda l, j: (l, 0, j))],
        out_specs=pl.BlockSpec((None, MOD_ROWS, tn), lambda l, j: (l, 0, j)),
        compiler_params=_params("parallel", "parallel"),
    )(c_all, w_mod, b_mod.reshape(depth, 1, nm))


def _split_specs(n_lat_tiles, d):
    return [pl.BlockSpec((TM_ROW, d), lambda i: (jnp.minimum(i, n_lat_tiles - 1), 0)),
            pl.BlockSpec((TM_ROW, d), lambda i: (jnp.maximum(i - n_lat_tiles, 0), 0))]


def _modulate_kernel(xl_ref, xc_ref, mod_ref, h_ref, *, n_lat_tiles, i_shift, i_scale):
    x = jnp.where(pl.program_id(0) < n_lat_tiles, xl_ref[...], xc_ref[...])
    h = _ln(x) * (1.0 + mod_ref[i_scale:i_scale + 1, :]) + mod_ref[i_shift:i_shift + 1, :]
    h_ref[...] = h.astype(BF16)


def _modulate(x_lat, x_ctx, mods, i_shift, i_scale, seg):
    n_lat, d = x_lat.shape
    n = n_lat + x_ctx.shape[0]
    n_lat_tiles = n_lat // TM_ROW
    return pl.pallas_call(
        functools.partial(_modulate_kernel, n_lat_tiles=n_lat_tiles, i_shift=i_shift, i_scale=i_scale),
        name="modulate",
        out_shape=jax.ShapeDtypeStruct((n, d), BF16),
        grid=(n // TM_ROW,),
        in_specs=_split_specs(n_lat_tiles, d) + [pl.BlockSpec((None, N_MOD, d), lambda i: (seg(i), 0, 0))],
        out_specs=pl.BlockSpec((TM_ROW, d), lambda i: (i, 0)),
        compiler_params=_params("parallel"),
    )(x_lat, x_ctx, mods)


def _norm_kernel(*refs, n_lat_tiles, alpha, coef, i_gate, i_shift, i_scale):
    if n_lat_tiles is None:
        x_ref, y_ref, modg_ref, g_ref, b_ref, *rest = refs
        x = x_ref[...]
    else:
        xl_ref, xc_ref, y_ref, modg_ref, g_ref, b_ref, *rest = refs
        x = jnp.where(pl.program_id(0) < n_lat_tiles, xl_ref[...], xc_ref[...])
    t = alpha * x + (coef * modg_ref[i_gate:i_gate + 1, :]) * y_ref[...].astype(F32)
    xn = _ln(t) * g_ref[...] + b_ref[...]
    if i_shift is None:
        (xo_ref,) = rest
        xo_ref[...] = xn
    else:
        modn_ref, xo_ref, h_ref = rest
        xo_ref[...] = xn
        h = _ln(xn) * (1.0 + modn_ref[i_scale:i_scale + 1, :]) + modn_ref[i_shift:i_shift + 1, :]
        h_ref[...] = h.astype(BF16)


def _norm(x, y, mods_gate, i_gate, coef, g, b, alpha, seg, n_rows, mods_next=None, i_shift=None, i_scale=None,
          n_out=None):
    split = isinstance(x, tuple)
    d = y.shape[1]
    n_out = y.shape[0] if n_out is None else n_out
    row = pl.BlockSpec((TM_ROW, d), lambda i: (i, 0))
    vec = pl.BlockSpec((1, d), lambda i: (0, 0))
    mod = pl.BlockSpec((None, N_MOD, d), lambda i: (seg(i), 0, 0))
    n_lat_tiles = x[0].shape[0] // TM_ROW if split else None
    in_specs = (_split_specs(n_lat_tiles, d) if split else [row]) + [row, mod, vec, vec]
    args = (list(x) if split else [x]) + [y, mods_gate, g.reshape(1, d), b.reshape(1, d)]
    out_shape = [jax.ShapeDtypeStruct((n_out, d), F32)]
    out_specs = [row]
    if i_shift is not None:
        in_specs.append(mod)
        args.append(mods_next)
        out_shape.append(jax.ShapeDtypeStruct((n_out, d), BF16))
        out_specs.append(row)
    out = pl.pallas_call(
        functools.partial(_norm_kernel, n_lat_tiles=n_lat_tiles, alpha=alpha, coef=coef, i_gate=i_gate,
                          i_shift=i_shift, i_scale=i_scale),
        name="norm",
        out_shape=out_shape,
        grid=(n_rows // TM_ROW,),
        in_specs=in_specs,
        out_specs=out_specs,
        compiler_params=_params("parallel"),
    )(*args)
    return out if i_shift is not None else (out[0], None)


def _ffn_in_kernel(h_ref, wg_ref, wu_ref, o_ref, wgb_ref, wub_ref):
    _cast_weight_once(wg_ref, wgb_ref)
    _cast_weight_once(wu_ref, wub_ref)
    h = h_ref[...]
    g = jnp.dot(h, wgb_ref[...], preferred_element_type=F32)
    u = jnp.dot(h, wub_ref[...], preferred_element_type=F32)
    o_ref[...] = (g * _sigmoid(g) * u).astype(o_ref.dtype)


def _ffn_in(h, w, layer, n_rows):
    n, d = h.shape
    dff = w.shape[2] // 2
    tn = _col_tile(dff)
    return pl.pallas_call(
        _ffn_in_kernel,
        name="ffn_in",
        out_shape=jax.ShapeDtypeStruct((n, dff), BF16),
        grid=(dff // tn, n_rows // TM),
        in_specs=[pl.BlockSpec((TM, d), lambda j, i: (i, 0)),
                  pl.BlockSpec((None, d, tn), lambda j, i: (layer, 0, j)),
                  pl.BlockSpec((None, d, tn), lambda j, i: (layer, 0, j + dff // tn))],
        out_specs=pl.BlockSpec((TM, tn), lambda j, i: (i, j)),
        scratch_shapes=[pltpu.VMEM((d, tn), BF16), pltpu.VMEM((d, tn), BF16)],
        compiler_params=_params("arbitrary", "arbitrary"),
    )(h, w, w)


def _mm_kernel(a_ref, w_ref, o_ref, wb_ref):
    _cast_weight_once(w_ref, wb_ref)
    o_ref[...] = jnp.dot(a_ref[...], wb_ref[...], preferred_element_type=F32).astype(o_ref.dtype)


def _mm(a, w, layer, col_off, n_cols, out_dtype, n_rows, name):
    n, k = a.shape
    max_tn, buffers = _wide_tile(k, jnp.dtype(out_dtype).itemsize)
    tn = _col_tile(n_cols, col_off, max_tn)
    joff = col_off // tn
    return pl.pallas_call(
        _mm_kernel,
        name=name,
        out_shape=jax.ShapeDtypeStruct((n, n_cols), out_dtype),
        grid=(n_cols // tn, n_rows // TM),
        in_specs=[pl.BlockSpec((TM, k), lambda j, i: (i, 0)),
                  pl.BlockSpec((None, k, tn), lambda j, i: (layer, 0, j + joff),
                               pipeline_mode=pl.Buffered(buffers))],
        out_specs=pl.BlockSpec((TM, tn), lambda j, i: (i, j)),
        scratch_shapes=[pltpu.VMEM((k, tn), BF16)],
        compiler_params=_params("arbitrary", "arbitrary"),
    )(a, w)


def _mm_rope_kernel(a_ref, w_ref, cos_ref, sin_ref, o_ref, orot_ref, wb_ref):
    _cast_weight_once(w_ref, wb_ref)
    acc = jnp.dot(a_ref[...], wb_ref[...], preferred_element_type=F32)
    o_ref[...] = acc.astype(o_ref.dtype)
    tn = acc.shape[1]
    reps = tn // HEAD_DIM
    cos = jnp.concatenate([cos_ref[...]] * reps, axis=1)
    sin = jnp.concatenate([sin_ref[...]] * reps, axis=1)
    lane = lax.broadcasted_iota(jnp.int32, acc.shape, 1)
    quarter = HEAD_DIM // 4
    first = (lane % (2 * quarter)) < quarter
    partner = jnp.where(first, pltpu.roll(acc, tn - quarter, 1), pltpu.roll(acc, quarter, 1))
    orot_ref[...] = (acc * cos + partner * sin).astype(orot_ref.dtype)


def _mm_rope(a, w, layer, col_off, n_cols, cos, sin, tiles_per_batch, n_rows):
    n, k = a.shape
    max_tn, buffers = _wide_tile(k, 2 * jnp.dtype(BF16).itemsize)
    tn = _col_tile(n_cols, col_off, max_tn)
    joff = col_off // tn
    out = jax.ShapeDtypeStruct((n, n_cols), BF16)
    tab = pl.BlockSpec((TM, HEAD_DIM), lambda j, i: (i % tiles_per_batch, 0))
    blk = pl.BlockSpec((TM, tn), lambda j, i: (i, j))
    return pl.pallas_call(
        _mm_rope_kernel,
        name="proj_rope",
        out_shape=[out, out],
        grid=(n_cols // tn, n_rows // TM),
        in_specs=[pl.BlockSpec((TM, k), lambda j, i: (i, 0)),
                  pl.BlockSpec((None, k, tn), lambda j, i: (layer, 0, j + joff),
                               pipeline_mode=pl.Buffered(buffers)),
                  tab, tab],
        out_specs=[blk, blk],
        scratch_shapes=[pltpu.VMEM((k, tn), BF16)],
        compiler_params=_params("arbitrary", "arbitrary"),
    )(a, w, cos, sin)


def _merge_kernel(h_ref, oa_ref, ob_ref, oc_ref, wga_ref, wgb_ref, wgc_ref, wa_ref, wb_ref, wc_ref, o_ref,
                  *scratch):
    weights = (wga_ref, wgb_ref, wgc_ref, wa_ref, wb_ref, wc_ref)
    for w_ref, wb16_ref in zip(weights, scratch):
        _cast_weight_once(w_ref, wb16_ref)
    gate_w, proj_w = scratch[:N_BRANCH], scratch[N_BRANCH:]
    h = h_ref[...]
    t = None
    for branch_ref, wg, wp in zip((oa_ref, ob_ref, oc_ref), gate_w, proj_w):
        gate = _sigmoid(jnp.dot(h, wg[...], preferred_element_type=F32))
        term = gate * jnp.dot(branch_ref[...], wp[...], preferred_element_type=F32)
        t = term if t is None else t + term
    o_ref[...] = t.astype(o_ref.dtype)


def _merge(h, o_a, o_b, o_c, w_in, gate_off, w_pa, w_pb, w_pc, layer, n_rows):
    n, d = h.shape
    tn = _col_tile(d, gate_off, MERGE_TN)

    def act(o):
        return pl.BlockSpec((TM, o.shape[1]), lambda j, i: (i, 0))

    def proj(w):
        return pl.BlockSpec((None, w.shape[1], tn), lambda j, i: (layer, 0, j), pipeline_mode=pl.Buffered(1))

    def gate(branch):
        joff = (gate_off + branch * d) // tn
        return pl.BlockSpec((None, d, tn), lambda j, i: (layer, 0, j + joff))

    return pl.pallas_call(
        _merge_kernel,
        name="merge",
        out_shape=jax.ShapeDtypeStruct((n, d), BF16),
        grid=(d // tn, n_rows // TM),
        in_specs=[act(h), act(o_a), act(o_b), act(o_c), gate(0), gate(1), gate(2), proj(w_pa), proj(w_pb), proj(w_pc)],
        out_specs=pl.BlockSpec((TM, tn), lambda j, i: (i, j)),
        scratch_shapes=([pltpu.VMEM((d, tn), BF16)] * N_BRANCH
                        + [pltpu.VMEM((w.shape[1], tn), BF16) for w in (w_pa, w_pb, w_pc)]),
        compiler_params=_params("arbitrary", "arbitrary"),
    )(h, o_a, o_b, o_c, w_in, w_in, w_in, w_pa, w_pb, w_pc)


def _attn_block_geometry(kind):
    half = WIN_H // 2
    if kind == 0:
        return 0, [max(i - half, 0) for i in range(ATT_ROWS)]
    if kind == 1:
        return -half, list(range(ATT_ROWS))
    return -(ATT_KROWS - ATT_ROWS), [min(half + i, ATT_KROWS - WIN_H) for i in range(ATT_ROWS)]


def _attn_sub_key_offset(kind, sub):
    _, first_key_row = _attn_block_geometry(kind)
    return min(min(first_key_row[sub * ATT_SUB:(sub + 1) * ATT_SUB]), ATT_KROWS - ATT_SUB_KROWS)


def _attn_sub_pattern(kind, sub):
    diff, first_key_row = _attn_block_geometry(kind)
    koff = _attn_sub_key_offset(kind, sub)
    rows = range(sub * ATT_SUB, (sub + 1) * ATT_SUB)
    return tuple((first_key_row[i] - koff, koff - i + diff + WIN_H - 1) for i in rows)


def _attn_patterns():
    patterns, index = [], []
    for kind in range(3):
        index.append([])
        for sub in range(ATT_ROWS // ATT_SUB):
            p = _attn_sub_pattern(kind, sub)
            if p not in patterns:
                patterns.append(p)
            index[kind].append(patterns.index(p))
    return patterns, index


def _rpb_table_kernel(rpb_ref, o_ref):
    shape = (GRID_W, 2 * GRID_W)
    cq = lax.broadcasted_iota(jnp.int32, shape, 0)
    lane = lax.broadcasted_iota(jnp.int32, shape, 1)
    ck = lane % GRID_W
    second = lane >= GRID_W
    start = jnp.clip(cq - WIN_W // 2, 0, GRID_W - WIN_W)
    in_win = (ck >= start) & (ck < start + WIN_W)
    neg = jnp.full(shape, NEG_INF, F32)
    n_bias_rows = 2 * WIN_H - 1

    def half_tile(a, lane_off):
        row = jnp.broadcast_to(rpb_ref[a:a + 1, :], shape)
        return pltpu.roll(row, (lane_off - (WIN_W - 1)) % LANES, 1, stride=1, stride_axis=0)

    lo_half = [half_tile(a, 0) for a in range(n_bias_rows)]
    hi_half = [half_tile(a, GRID_W) for a in range(n_bias_rows)]
    for p, pattern in enumerate(_attn_patterns()[0]):
        for isub, (lo, a0) in enumerate(pattern):
            for pair in range(ATT_SUB_KROWS // 2):
                j0, j1 = 2 * pair, 2 * pair + 1
                t0 = lo_half[j0 + a0] if lo <= j0 < lo + WIN_H else neg
                t1 = hi_half[j1 + a0] if lo <= j1 < lo + WIN_H else neg
                tile = jnp.where(in_win, jnp.where(second, t1, t0), neg)
                o_ref[p, isub * GRID_W:(isub + 1) * GRID_W, pair * 2 * GRID_W:(pair + 1) * 2 * GRID_W] = tile


def _rpb_table(rpb):
    heads, nr, nc = rpb.shape
    rpb_pad = jnp.zeros((heads, 2 * WIN_H, LANES), F32).at[:, :nr, :nc].set(rpb)
    shape = (len(_attn_patterns()[0]), ATT_SUB * GRID_W, ATT_SUB_KROWS * GRID_W)
    return pl.pallas_call(
        _rpb_table_kernel,
        name="rpb_table",
        out_shape=jax.ShapeDtypeStruct((heads,) + shape, F32),
        grid=(heads,),
        in_specs=[pl.BlockSpec((None, 2 * WIN_H, LANES), lambda h: (h, 0, 0))],
        out_specs=pl.BlockSpec((None,) + shape, lambda h: (h, 0, 0, 0)),
        compiler_params=_params("parallel"),
    )(rpb_pad)


def _attn_kernel(q_ref, qr_ref, kr_ref, v_ref, kc_ref, vc_ref, bias_ref, o_ref, *, n_grid_rows, n_blocks, scale):
    rb = pl.program_id(2)
    kr0 = jnp.clip(rb * ATT_ROWS - WIN_H // 2, 0, n_grid_rows - ATT_KROWS)
    kc = kc_ref[...]
    vc = vc_ref[...]
    table_index = _attn_patterns()[1]

    def by_kind(values):
        return jnp.where(rb == 0, values[0], jnp.where(rb == n_blocks - 1, values[2], values[1]))

    for sub in range(ATT_ROWS // ATT_SUB):
        koff = by_kind([_attn_sub_key_offset(kind, sub) for kind in range(3)])
        keys = pl.ds(pl.multiple_of((kr0 + koff) * GRID_W, GRID_W), ATT_SUB_KROWS * GRID_W)
        rows = pl.ds(sub * ATT_SUB * GRID_W, ATT_SUB * GRID_W)
        s = lax.dot_general(qr_ref[rows, :], kr_ref[keys, :], NT_DIMS, preferred_element_type=F32) * scale
        s += bias_ref[by_kind([table_index[kind][sub] for kind in range(3)])]
        sc = lax.dot_general(q_ref[rows, :], kc, NT_DIMS, preferred_element_type=F32) * scale
        m = jnp.maximum(jnp.max(s, axis=-1, keepdims=True), jnp.max(sc, axis=-1, keepdims=True))
        p = jnp.exp(s - m)
        pc = jnp.exp(sc - m)
        denom = jnp.sum(p, axis=-1, keepdims=True) + jnp.sum(pc, axis=-1, keepdims=True)
        o = jnp.dot(p.astype(BF16), v_ref[keys, :], preferred_element_type=F32)
        o += jnp.dot(pc.astype(BF16), vc, preferred_element_type=F32)
        o_ref[rows, :] = (o / denom).astype(o_ref.dtype)


def _attention(q, q_rot, k, k_rot, v, bias, n_batch, seq, ctx_len):
    n, width = q.shape
    heads = width // HEAD_DIM
    n_grid_rows = seq // GRID_W
    tq = ATT_ROWS * GRID_W
    blocks = seq // tq
    ctx0 = n_batch * seq // ctx_len
    qspec = pl.BlockSpec((tq, HEAD_DIM), lambda b, h, r: (b * blocks + r, h))
    kspec = pl.BlockSpec((seq, HEAD_DIM), lambda b, h, r: (b, h))
    cspec = pl.BlockSpec((ctx_len, HEAD_DIM), lambda b, h, r: (ctx0 + b, h))

    return pl.pallas_call(
        functools.partial(_attn_kernel, n_grid_rows=n_grid_rows, n_blocks=blocks, scale=HEAD_DIM ** -0.5),
        name="attn",
        out_shape=jax.ShapeDtypeStruct((n, width), BF16),
        grid=(n_batch, heads, blocks),
        in_specs=[qspec, qspec, kspec, kspec, cspec, cspec,
                  pl.BlockSpec((None,) + bias.shape[1:], lambda b, h, r: (h, 0, 0, 0))],
        out_specs=qspec,
        compiler_params=_params("parallel", "parallel", "arbitrary"),
    )(q, q_rot, k_rot, v, k, v, bias)


def _ctx_attn_kernel(q_ref, k_ref, v_ref, prev_ref, o_ref, *, scale):
    del prev_ref
    s = lax.dot_general(q_ref[...], k_ref[...], NT_DIMS, preferred_element_type=F32) * scale
    m = jnp.max(s, axis=-1, keepdims=True)
    p = jnp.exp(s - m)
    denom = jnp.sum(p, axis=-1, keepdims=True)
    o = jnp.dot(p.astype(BF16), v_ref[...], preferred_element_type=F32)
    o_ref[...] = (o / denom).astype(o_ref.dtype)


def _ctx_attention(q, k, v, o_att, n_batch, seq, ctx_len):
    n, width = q.shape
    heads = width // HEAD_DIM
    ctx0 = n_batch * seq // ctx_len
    spec = pl.BlockSpec((ctx_len, HEAD_DIM), lambda b, h: (ctx0 + b, h))
    return pl.pallas_call(
        functools.partial(_ctx_attn_kernel, scale=HEAD_DIM ** -0.5),
        name="ctx_attn",
        out_shape=jax.ShapeDtypeStruct((n, width), BF16),
        grid=(n_batch, heads),
        in_specs=[spec, spec, spec, pl.BlockSpec(memory_space=pl.ANY)],
        out_specs=spec,
        input_output_aliases={3: 0},
        compiler_params=_params("parallel", "parallel"),
    )(q, k, v, o_att)


def _gmlp_kernel(z_ref, g_ref, b_ref, ws_ref, bs_ref, o_ref, *, width, chunks):
    z = jax.nn.gelu(z_ref[...])
    u = z[:, :width]
    v = (_ln(z[:, width:]) * g_ref[...] + b_ref[...]).astype(BF16)
    for c in range(chunks):
        rows = slice(c * CHUNK, (c + 1) * CHUNK)
        for g in range(width // GMLP_GROUP_DIM):
            cols = slice(g * GMLP_GROUP_DIM, (g + 1) * GMLP_GROUP_DIM)
            mixed = jnp.dot(ws_ref[g], v[rows, cols], preferred_element_type=F32) + bs_ref[:, cols]
            o_ref[rows, cols] = (u[rows, cols] * mixed).astype(o_ref.dtype)


def _gmlp(z, ln_g, ln_b, w_s, b_s, n_rows):
    n, two_w = z.shape
    width = two_w // 2
    groups = width // GMLP_GROUP_DIM
    chunks = TM // CHUNK
    bs = jnp.repeat(b_s.T, GMLP_GROUP_DIM, axis=1)
    vec = pl.BlockSpec((1, width), lambda i: (0, 0))
    return pl.pallas_call(
        functools.partial(_gmlp_kernel, width=width, chunks=chunks),
        name="gmlp",
        out_shape=jax.ShapeDtypeStruct((n, width), BF16),
        grid=(n_rows // TM,),
        in_specs=[pl.BlockSpec((TM, two_w), lambda i: (i, 0)), vec, vec,
                  pl.BlockSpec((groups, CHUNK, CHUNK), lambda i: (0, 0, 0)),
                  pl.BlockSpec((CHUNK, width), lambda i: (0, 0))],
        out_specs=pl.BlockSpec((TM, width), lambda i: (i, 0)),
        compiler_params=_params("parallel"),
    )(z, ln_g.reshape(1, width), ln_b.reshape(1, width), w_s.astype(BF16), bs)


def _s5_gen_kernel(lr_ref, li_ref, ldt_ref, bre_ref, bim_ref, cre_ref, cim_ref, m_ref, wst_ref, et_ref, a_ref):
    hi = lax.Precision.HIGHEST
    nb = SSM_BLOCK
    lane = lax.broadcasted_iota(jnp.int32, (SSM_GROUP, SSM_ROW), 1)
    state_lane = lax.broadcasted_iota(jnp.int32, (SSM_GROUP, 2 * SSM_STATE), 1)
    in_group = [state_lane < SSM_STATE, state_lane >= SSM_STATE]
    m_rows = [[None] * nb for _ in range(2)]
    for d in range(2):
        lr = lr_ref[d:d + 1, :]
        li = li_ref[d:d + 1, :]
        dt = jnp.exp(ldt_ref[d:d + 1, :])
        mag = jnp.exp(lr * dt)
        ar = mag * jnp.cos(li * dt)
        ai = mag * jnp.sin(li * dt)
        den = lr * lr + li * li
        fr = ((ar - 1.0) * lr + ai * li) / den
        fi = (ai * lr - (ar - 1.0) * li) / den
        br = bre_ref[d]
        bi = bim_ref[d]
        bbr = fr * br - fi * bi
        bbi = fr * bi + fi * br
        cr = cre_ref[d]
        ci = cim_ref[d]
        pr = [jnp.ones_like(ar)]
        pi = [jnp.zeros_like(ai)]
        for _ in range(nb):
            pr.append(pr[-1] * ar - pi[-1] * ai)
            pi.append(pr[-2] * ai + pi[-1] * ar)
        a_ref[2 * d] = pr[nb]
        a_ref[2 * d + 1] = pi[nb]
        e_state = [(t + 1) if d == 0 else (nb - t) for t in range(nb)]
        e_in = [(nb - 1 - s) if d == 0 else s for s in range(nb)]
        e_lag = [e if d == 0 else (nb - 1 - e) for e in range(nb)]
        for s in range(nb):
            wr, wi = pr[e_in[s]], pi[e_in[s]]
            er, ei = pr[e_state[s]], pi[e_state[s]]
            parts = (wr * bbr - wi * bbi, wr * bbi + wi * bbr)
            outs = (cr * er - ci * ei, -(cr * ei + ci * er))
            for g in range(2):
                rows = slice(g * SSM_ROW + s * SSM_GROUP, g * SSM_ROW + (s + 1) * SSM_GROUP)
                for part in range(2):
                    cols = slice((2 * d + part) * 2 * SSM_STATE, (2 * d + part + 1) * 2 * SSM_STATE)
                    wst_ref[rows, cols] = jnp.where(in_group[g], parts[part], 0.0).astype(wst_ref.dtype)
                    et_ref[rows, cols] = jnp.where(in_group[g], outs[part], 0.0).astype(et_ref.dtype)
        lag_r = jnp.concatenate([cr * pr[e] - ci * pi[e] for e in e_lag], axis=0)
        lag_i = jnp.concatenate([-(cr * pi[e] + ci * pr[e]) for e in e_lag], axis=0)
        for g in range(2):
            gbr = jnp.where(in_group[g], bbr, 0.0)
            gbi = jnp.where(in_group[g], bbi, 0.0)
            kbase = (lax.dot_general(gbr, lag_r, NT_DIMS, precision=hi, preferred_element_type=F32)
                     + lax.dot_general(gbi, lag_i, NT_DIMS, precision=hi, preferred_element_type=F32))
            for s in range(nb):
                if d == 0:
                    sh = s * SSM_GROUP
                    m_rows[g][s] = kbase if sh == 0 else jnp.where(lane >= sh, pltpu.roll(kbase, sh, 1), 0.0)
                else:
                    sh = (nb - 1 - s) * SSM_GROUP
                    row = kbase if sh == 0 else jnp.where(lane < SSM_ROW - sh, pltpu.roll(kbase, SSM_ROW - sh, 1), 0.0)
                    m_rows[g][s] = m_rows[g][s] + row
    zeros = jnp.zeros((SSM_GROUP, SSM_ROW), m_ref.dtype)
    for g in range(2):
        for s in range(nb):
            rows = slice(g * SSM_ROW + s * SSM_GROUP, g * SSM_ROW + (s + 1) * SSM_GROUP)
            m_ref[rows, g * SSM_ROW:(g + 1) * SSM_ROW] = m_rows[g][s].astype(m_ref.dtype)
            m_ref[rows, (1 - g) * SSM_ROW:(2 - g) * SSM_ROW] = zeros


def _s5_gen(lam_re, lam_im, log_dt, b_re, b_im, c_re, c_im):
    _, groups, p = lam_re.shape
    pairs = groups // 2

    def vec_pairs(x):
        return jnp.swapaxes(x.reshape(2, pairs, 2 * p), 0, 1)

    def mat_pairs(x):
        x = x.reshape(2, pairs, 2, SSM_GROUP, p)
        return jnp.transpose(x, (1, 0, 3, 2, 4)).reshape(pairs, 2, SSM_GROUP, 2 * p)

    ldt = jnp.broadcast_to(log_dt[:, :, None], lam_re.shape)
    vec = pl.BlockSpec((None, 2, 2 * p), lambda g: (g, 0, 0))
    mat = pl.BlockSpec((None, 2, SSM_GROUP, 2 * p), lambda g: (g, 0, 0, 0))
    return pl.pallas_call(
        _s5_gen_kernel,
        name="s5_gen",
        out_shape=[jax.ShapeDtypeStruct((pairs, 2 * SSM_ROW, 2 * SSM_ROW), BF16),
                   jax.ShapeDtypeStruct((pairs, 2 * SSM_ROW, 4 * 2 * p), BF16),
                   jax.ShapeDtypeStruct((pairs, 2 * SSM_ROW, 4 * 2 * p), BF16),
                   jax.ShapeDtypeStruct((pairs, 4, 1, 2 * p), F32)],
        grid=(pairs,),
        in_specs=[vec, vec, vec, mat, mat, mat, mat],
        out_specs=[pl.BlockSpec((None, 2 * SSM_ROW, 2 * SSM_ROW), lambda g: (g, 0, 0)),
                   pl.BlockSpec((None, 2 * SSM_ROW, 4 * 2 * p), lambda g: (g, 0, 0)),
                   pl.BlockSpec((None, 2 * SSM_ROW, 4 * 2 * p), lambda g: (g, 0, 0)),
                   pl.BlockSpec((None, 4, 1, 2 * p), lambda g: (g, 0, 0, 0))],
        compiler_params=_params("parallel"),
    )(vec_pairs(lam_re), vec_pairs(lam_im), vec_pairs(ldt),
      mat_pairs(jnp.swapaxes(b_re, 2, 3)), mat_pairs(jnp.swapaxes(b_im, 2, 3)), mat_pairs(c_re), mat_pairs(c_im))


def _slot_transpose(xs):
    n = SLOTS
    slot = lax.broadcasted_iota(jnp.int32, xs[0].shape, 1) // SSM_GROUP
    diag = []
    for r in range(n):
        w = xs[r % n]
        for a in range(1, n):
            w = jnp.where(slot == a, xs[(a + r) % n], w)
        diag.append(w if r == 0 else pltpu.roll(w, r * SSM_GROUP, 1))
    ys = []
    for a in range(n):
        y = diag[0]
        for r in range(1, n):
            y = jnp.where(slot == (a + r) % n, diag[r], y)
        ys.append(y)
    return ys


def _s5_state_kernel(x_ref, wst_ref, u_ref, *state_refs):
    n_blk = u_ref.shape[0]
    xs = [x_ref[pl.ds(s, n_blk, stride=SSM_BLOCK), :] for s in range(SSM_BLOCK)]
    by_col = [_slot_transpose(xs[col * SLOTS:(col + 1) * SLOTS]) for col in range(SSM_BLOCK // SLOTS)]
    pieces = [by_col[col][gl] for gl in range(SSM_CHUNK_GROUPS) for col in range(SSM_BLOCK // SLOTS)]
    cols_per_pair = 2 * SSM_ROW // LANES
    for pair in range(SSM_CHUNK_GROUPS // 2):
        u = jnp.concatenate(pieces[pair * cols_per_pair:(pair + 1) * cols_per_pair], axis=1).astype(BF16)
        u_ref[:, pair * 2 * SSM_ROW:(pair + 1) * 2 * SSM_ROW] = u
        states = jnp.dot(u, wst_ref[pair], preferred_element_type=F32)
        for k, o_ref in enumerate(state_refs):
            o_ref[:, pair * 2 * SSM_STATE:(pair + 1) * 2 * SSM_STATE] = states[:, k * 2 * SSM_STATE:
                                                                                (k + 1) * 2 * SSM_STATE]


def _s5_row_tile(n_rows):
    return n_rows // 2 if n_rows % 32 == 0 else n_rows


def _s5_states(us, wst_pair, groups):
    n_rows = us.shape[0] // SSM_BLOCK
    chunks = groups // SSM_CHUNK_GROUPS
    pairs = SSM_CHUNK_GROUPS // 2
    rt = _s5_row_tile(n_rows)
    state = jax.ShapeDtypeStruct((n_rows, groups * SSM_STATE), F32)
    sspec = pl.BlockSpec((rt, SSM_CHUNK_GROUPS * SSM_STATE), lambda ch, i: (i, ch))
    uspec = pl.BlockSpec((rt, SSM_CHUNK_GROUPS * SSM_ROW), lambda ch, i: (i, ch))
    return pl.pallas_call(
        _s5_state_kernel,
        name="s5_states",
        out_shape=[jax.ShapeDtypeStruct((n_rows, groups * SSM_ROW), BF16)] + [state] * 4,
        grid=(chunks, n_rows // rt),
        in_specs=[pl.BlockSpec((rt * SSM_BLOCK, LANES), lambda ch, i: (i, ch)),
                  pl.BlockSpec((pairs,) + wst_pair.shape[1:], lambda ch, i: (ch, 0, 0))],
        out_specs=[uspec] + [sspec] * 4,
        compiler_params=_params("parallel", "parallel"),
    )(us, wst_pair)


def _s5_scan_kernel(sfr_ref, sfi_ref, sbr_ref, sbi_ref, a_ref, hfr_ref, hfi_ref, hbr_ref, hbi_ref,
                    *, n_batch, n_lat_blocks, n_ctx_blocks):
    afr, afi, abr, abi = (a_ref[k:k + 1, :] for k in range(4))
    zero = jnp.zeros_like(afr)
    nl, nc = n_lat_blocks, n_ctx_blocks

    def step(t, carry):
        new = []
        in_ctx = t < nc
        for b in range(n_batch):
            lat0, ctx0 = b * nl, n_batch * nl + b * nc
            fr, fi, br, bi = carry[4 * b:4 * b + 4]
            rf = pl.ds(jnp.where(in_ctx, ctx0 + t, lat0 + t - nc), 1)
            rb = pl.ds(jnp.where(in_ctx, ctx0 + nc - 1 - t, lat0 + nl - 1 - (t - nc)), 1)
            hfr_ref[rf, :] = fr
            hfi_ref[rf, :] = fi
            hbr_ref[rb, :] = br
            hbi_ref[rb, :] = bi
            new += [afr * fr - afi * fi + sfr_ref[rf, :], afr * fi + afi * fr + sfi_ref[rf, :],
                    abr * br - abi * bi + sbr_ref[rb, :], abr * bi + abi * br + sbi_ref[rb, :]]
        return tuple(new)

    lax.fori_loop(0, nl + nc, step, (zero,) * (4 * n_batch))


def _s5_scan(s_parts, a_rows, n_batch, n_lat_blocks, n_ctx_blocks):
    rows, lanes = s_parts[0].shape
    tl = min(lanes, 4 * LANES)
    spec = pl.BlockSpec((rows, tl), lambda j: (0, j))
    return pl.pallas_call(
        functools.partial(_s5_scan_kernel, n_batch=n_batch, n_lat_blocks=n_lat_blocks, n_ctx_blocks=n_ctx_blocks),
        name="s5_scan",
        out_shape=[jax.ShapeDtypeStruct((rows, lanes), F32)] * 4,
        grid=(lanes // tl,),
        in_specs=[spec] * 4 + [pl.BlockSpec((4, tl), lambda j: (0, j))],
        out_specs=[spec] * 4,
        compiler_params=_params("parallel"),
    )(*s_parts, a_rows)


def _s5_out_kernel(u_ref, m_ref, et_ref, hfr_ref, hfi_ref, hbr_ref, hbi_ref, y_ref):
    ys = []
    for pair in range(SSM_CHUNK_GROUPS // 2):
        y = jnp.dot(u_ref[:, pair * 2 * SSM_ROW:(pair + 1) * 2 * SSM_ROW], m_ref[pair], preferred_element_type=F32)
        h = jnp.concatenate([h_ref[:, pair * 2 * SSM_STATE:(pair + 1) * 2 * SSM_STATE]
                             for h_ref in (hfr_ref, hfi_ref, hbr_ref, hbi_ref)], axis=1).astype(BF16)
        ys.append(y + lax.dot_general(h, et_ref[pair], NT_DIMS, preferred_element_type=F32))
    n_blk = ys[0].shape[0]
    for col in range(SSM_BLOCK // SLOTS):
        by_group = [ys[gl // 2][:, (gl % 2) * SSM_ROW + col * LANES:(gl % 2) * SSM_ROW + (col + 1) * LANES]
                    for gl in range(SSM_CHUNK_GROUPS)]
        for k, y in enumerate(_slot_transpose(by_group)):
            y_ref[pl.ds(col * SLOTS + k, n_blk, stride=SSM_BLOCK), :] = y


def _s5_outputs(u, m_pair, et_pair, h_parts, groups):
    n_rows = u.shape[0]
    chunks = groups // SSM_CHUNK_GROUPS
    pairs = SSM_CHUNK_GROUPS // 2
    rt = _s5_row_tile(n_rows)
    hspec = pl.BlockSpec((rt, SSM_CHUNK_GROUPS * SSM_STATE), lambda ch, i: (i, ch))
    return pl.pallas_call(
        _s5_out_kernel,
        name="s5_out",
        out_shape=jax.ShapeDtypeStruct((n_rows * SSM_BLOCK, groups * SSM_GROUP), F32),
        grid=(chunks, n_rows // rt),
        in_specs=[pl.BlockSpec((rt, SSM_CHUNK_GROUPS * SSM_ROW), lambda ch, i: (i, ch)),
                  pl.BlockSpec((pairs,) + m_pair.shape[1:], lambda ch, i: (ch, 0, 0)),
                  pl.BlockSpec((pairs,) + et_pair.shape[1:], lambda ch, i: (ch, 0, 0)),
                  hspec, hspec, hspec, hspec],
        out_specs=pl.BlockSpec((rt * SSM_BLOCK, LANES), lambda ch, i: (i, ch)),
        compiler_params=_params("parallel", "parallel"),
    )(u, m_pair, et_pair, *h_parts)


def _s5_glu_kernel(ys_ref, us_ref, d_ref, w_ref, b_ref, o_ref, wb_ref):
    @pl.when(pl.program_id(0) == 0)
    def _():
        wb_ref[...] = w_ref[...].astype(BF16)

    y = jax.nn.gelu(ys_ref[...] + d_ref[...] * us_ref[...])
    gate = _sigmoid(jnp.dot(y.astype(BF16), wb_ref[...], preferred_element_type=F32) + b_ref[...])
    o_ref[...] = (y * gate).astype(o_ref.dtype)


def _s5_glu(y_s, us, d_skip, w_glu, layer, b_glu, n_rows):
    n, w = us.shape
    row = pl.BlockSpec((TM, w), lambda i: (i, 0))
    vec = pl.BlockSpec((1, w), lambda i: (0, 0))
    return pl.pallas_call(
        _s5_glu_kernel,
        name="s5_glu",
        out_shape=jax.ShapeDtypeStruct((n, w), BF16),
        grid=(n_rows // TM,),
        in_specs=[row, row, vec, pl.BlockSpec((None, w, w), lambda i: (layer, 0, 0)), vec],
        out_specs=row,
        scratch_shapes=[pltpu.VMEM((w, w), BF16)],
        compiler_params=_params("arbitrary"),
    )(y_s, us, d_skip.reshape(1, w), w_glu, b_glu.reshape(1, w))


def _s5(us, n_batch, seq, ctx_len, lam_re, lam_im, log_dt, b_re, b_im, c_re, c_im):
    groups = us.shape[1] // SSM_GROUP
    m_pair, wst_pair, et_pair, a_pair = _s5_gen(lam_re, lam_im, log_dt, b_re, b_im, c_re, c_im)
    a_rows = jnp.transpose(a_pair[:, :, 0, :], (1, 0, 2)).reshape(4, groups * SSM_STATE)
    u, *s_parts = _s5_states(us, wst_pair, groups)
    h_parts = _s5_scan(s_parts, a_rows, n_batch, seq // SSM_BLOCK, ctx_len // SSM_BLOCK)
    return _s5_outputs(u, m_pair, et_pair, h_parts, groups)


def _rope_tables(seq):
    quarter = HEAD_DIM // 4
    n_grid_rows = seq // GRID_W
    inv = ROPE_BASE ** (-jnp.arange(quarter, dtype=F32) / quarter)
    shape = (n_grid_rows, GRID_W, quarter)
    ang_r = jnp.arange(n_grid_rows, dtype=F32)[:, None] * inv[None, :]
    ang_c = jnp.arange(GRID_W, dtype=F32)[:, None] * inv[None, :]
    cos_r, sin_r = (jnp.broadcast_to(f(ang_r)[:, None, :], shape) for f in (jnp.cos, jnp.sin))
    cos_c, sin_c = (jnp.broadcast_to(f(ang_c)[None, :, :], shape) for f in (jnp.cos, jnp.sin))
    cos = jnp.concatenate([cos_r, cos_r, cos_c, cos_c], axis=-1).reshape(seq, HEAD_DIM)
    sin = jnp.concatenate([-sin_r, sin_r, -sin_c, sin_c], axis=-1).reshape(seq, HEAD_DIM)
    return cos, sin


def kernel(x, c, ctx, c_ctx, w_mod, b_mod, ln_g, ln_b, ffn1_w_in, ffn1_w_out, ffn2_w_in, ffn2_w_out, w_in, rpb, gm_ln_g, gm_ln_b, gm_ws, gm_bs, ssm_lam_re, ssm_lam_im, ssm_log_dt, ssm_b_re, ssm_b_im, ssm_c_re, ssm_c_im, ssm_d, ssm_w_glu, ssm_b_glu, w_pa, w_pb, w_pc, w_o):
    n_batch, seq, d = x.shape
    ctx_len = ctx.shape[1]
    depth = w_mod.shape[0]
    att_w = w_pb.shape[1]
    gm_w = w_pa.shape[1]
    ssm_w = w_pc.shape[1]
    assert ssm_w % LANES == 0 and seq % SSM_BLOCK == 0 and ctx_len % SSM_BLOCK == 0
    q_off, k_off, v_off = 0, att_w, 2 * att_w
    gm_off = 3 * att_w
    ssm_off = gm_off + 2 * gm_w
    gate_off = ssm_off + ssm_w
    alpha = (2.0 * depth) ** 0.25
    n_lat = n_batch * seq
    n_all = n_lat + n_batch * ctx_len
    assert seq % TM == 0 and (n_batch * ctx_len) % TM == 0 and seq % (ATT_ROWS * GRID_W) == 0
    assert seq // GRID_W >= ATT_KROWS and ctx_len % CHUNK == 0 and n_lat % ctx_len == 0
    assert n_batch + 1 <= MOD_ROWS and rpb.shape[2:] == (2 * WIN_H - 1, 2 * WIN_W - 1)

    seg = _seg_map(seq // TM_ROW, n_batch)
    tiles_per_batch = seq // TM

    x_lat, x_ctx = x.reshape(n_lat, d), ctx.reshape(n_batch * ctx_len, d)
    c_all = jnp.zeros((MOD_ROWS, d), F32).at[:n_batch].set(c).at[n_batch].set(c_ctx)
    mods = _mods(c_all, w_mod, b_mod)[:, :n_batch + 1].reshape(depth, n_batch + 1, N_MOD, d)
    cos, sin = _rope_tables(seq)

    xs = (x_lat, x_ctx)
    h = _modulate(x_lat, x_ctx, mods[0], 0, 1, seg)
    for l in range(depth):
        last = l == depth - 1
        n_mix = n_lat if last else n_all

        y = _mm(_ffn_in(h, ffn1_w_in, l, n_all), ffn1_w_out, l, 0, d, BF16, n_all, "ffn_out")
        xs, h = _norm(xs, y, mods[l], 2, 0.5, ln_g[l, 0], ln_b[l, 0], alpha, seg, n_all,
                      mods_next=mods[l], i_shift=3, i_scale=4)

        q, q_rot = _mm_rope(h, w_in, l, q_off, att_w, cos, sin, tiles_per_batch, n_mix)
        k, k_rot = _mm_rope(h, w_in, l, k_off, att_w, cos, sin, tiles_per_batch, n_all)
        v = _mm(h, w_in, l, v_off, att_w, BF16, n_all, "proj_v")
        z = _mm(h, w_in, l, gm_off, 2 * gm_w, F32, n_mix, "proj_z")
        us = _mm(h, w_in, l, ssm_off, ssm_w, F32, n_all, "proj_us")

        o_att = _attention(q, q_rot, k, k_rot, v, _rpb_table(rpb[l]), n_batch, seq, ctx_len)
        if not last:
            o_att = _ctx_attention(q, k, v, o_att, n_batch, seq, ctx_len)
        o_gm = _gmlp(z, gm_ln_g[l], gm_ln_b[l], gm_ws[l], gm_bs[l], n_mix)
        y_s = _s5(us, n_batch, seq, ctx_len, ssm_lam_re[l], ssm_lam_im[l], ssm_log_dt[l],
                  ssm_b_re[l], ssm_b_im[l], ssm_c_re[l], ssm_c_im[l])
        o_ssm = _s5_glu(y_s, us, ssm_d[l], ssm_w_glu, l, ssm_b_glu[l], n_mix)
        merged = _merge(h, o_gm, o_att, o_ssm, w_in, gate_off, w_pa, w_pb, w_pc, l, n_mix)
        y = _mm(merged, w_o, l, 0, d, BF16, n_mix, "out_proj")
        xs, h = _norm(xs, y, mods[l], 5, 1.0, ln_g[l, 1], ln_b[l, 1], alpha, seg, n_mix,
                      mods_next=mods[l], i_shift=6, i_scale=7)

        y = _mm(_ffn_in(h, ffn2_w_in, l, n_mix), ffn2_w_out, l, 0, d, BF16, n_mix, "ffn_out")
        if last:
            xs, _ = _norm(xs, y, mods[l], 8, 0.5, ln_g[l, 2], ln_b[l, 2], alpha, seg, n_mix, n_out=n_mix)
        else:
            xs, h = _norm(xs, y, mods[l], 8, 0.5, ln_g[l, 2], ln_b[l, 2], alpha, seg, n_mix,
                          mods_next=mods[l + 1], i_shift=0, i_scale=1)
    return xs.reshape(n_batch, seq, d)
```
